```python
import jax, jax.numpy as jnp
from jax import lax
import numpy as np

D_MODEL = 1024
BATCH = 8
SEQ = 2048
DEPTH = 2

CTX_LEN = 256
GRID_W = 64
EPS = 1e-6
NEG_INF = -1e30
ROPE_THETA = 10000.0
D_FF = ((8 * D_MODEL // 3) + 127) // 128 * 128
D_MIX = D_MODEL

A_W = D_MIX // 4
A_VDIM = 64
A_HEADS = A_W // A_VDIM
A_DIM = A_VDIM // 2
A_QK = A_HEADS * 2 * A_DIM
A_QBLOCK = 128
B_W = 3 * D_MIX // 8
B_DIM = 64
B_HEADS = B_W // B_DIM
B_KV_HEADS = B_HEADS // 3
B_WINDOW = 128
B_BLOCK = 128
C_W = D_MIX - A_W - B_W
C_DK = 64
C_DV = 64
C_HEADS = C_W // C_DV
C_CONV = 5
C_CHUNK = 64

IN_SIZES = (A_QK, A_QK, A_W, B_HEADS * B_DIM, B_KV_HEADS * B_DIM, B_KV_HEADS * B_DIM,
            3 * C_W, C_W, 2 * C_HEADS, 2 * C_HEADS)
IN_OFFSETS = tuple(sum(IN_SIZES[:i + 1]) for i in range(len(IN_SIZES) - 1))
IN_COLS = sum(IN_SIZES)

kernel_name = 'hybrid_diffusion_headgroup_trunk'

f32 = jnp.float32


def rms_norm(x, w=None):
    xf = x.astype(f32)
    y = xf * lax.rsqrt(jnp.mean(xf * xf, axis=-1, keepdims=True) + EPS)
    if w is not None:
        y = y * w.astype(f32)
    return y.astype(x.dtype)


def l2norm(x):
    return x * lax.rsqrt(jnp.sum(x * x, axis=-1, keepdims=True) + EPS)


def modulate(h, mod, i):
    return rms_norm(h) * (1.0 + mod[:, :, i + 1]) + mod[:, :, i]


def swiglu(x, w1, w2):
    gate, up = jnp.split(x @ w1, 2, axis=-1)
    return (jax.nn.silu(gate) * up) @ w2


def ffn_half_step(h, mod, i, w1, w2):
    return h + 0.5 * mod[:, :, i + 2] * swiglu(modulate(h, mod, i), w1, w2)


def apply_rope_2d(x):
    n, d = x.shape[1], x.shape[-1]
    rows = n // GRID_W
    row = jnp.repeat(jnp.arange(rows, dtype=f32), GRID_W)
    col = jnp.tile(jnp.arange(GRID_W, dtype=f32), rows)
    half = d // 2
    inv = ROPE_THETA ** (-jnp.arange(0, half, 2, dtype=f32) / half)
    bshape = (n,) + (1,) * (x.ndim - 3) + (half // 2,)

    def rot(xa, pos):
        ang = (pos[:, None] * inv).reshape(bshape)
        cs, sn = jnp.cos(ang), jnp.sin(ang)
        x1, x2 = xa[..., :half // 2], xa[..., half // 2:]
        return jnp.concatenate([x1 * cs - x2 * sn, x2 * cs + x1 * sn], axis=-1)

    xf = x.astype(f32)
    out = jnp.concatenate([rot(xf[..., :half], row), rot(xf[..., half:], col)], axis=-1)
    return out.astype(x.dtype)


def diff_attention(q, k, v, lam):
    s = jnp.einsum('bqhmd,bkhmd->bhmqk', q, k).astype(f32) * (A_DIM ** -0.5)
    p = jax.nn.softmax(s, axis=-1)
    w = (p[:, :, 0] - lam * p[:, :, 1]).astype(v.dtype)
    return jnp.einsum('bhqk,bkhd->bqhd', w, v)


def diff_attention_blocked(q, k, v, lam):
    bn, n = q.shape[:2]
    nb = n // A_QBLOCK
    qb = jnp.moveaxis(q.reshape((bn, nb, A_QBLOCK) + q.shape[2:]), 1, 0)
    ob = lax.map(lambda qq: diff_attention(qq, k, v, lam), qb)
    return jnp.moveaxis(ob, 0, 1).reshape((bn, n) + ob.shape[3:])


def window_sink_attention_latent(q, k, v, k_ctx, v_ctx, sink):
    bn, n = q.shape[:2]
    nb = n // B_BLOCK
    G = B_HEADS // B_KV_HEADS
    L = B_BLOCK
    qb = q.reshape(bn, nb, L, B_KV_HEADS, G, B_DIM)

    def band(t):
        tb = t.reshape(bn, nb, L, B_KV_HEADS, B_DIM)
        tp = jnp.pad(tb, ((0, 0), (1, 1), (0, 0), (0, 0), (0, 0)))
        return jnp.concatenate([tp[:, :-2], tp[:, 1:-1], tp[:, 2:]], axis=2)

    kb, vb = band(k), band(v)
    qpos = jnp.arange(n).reshape(nb, L)
    kpos = (jnp.arange(nb)[:, None] - 1) * L + jnp.arange(3 * L)[None, :]
    valid = ((kpos >= 0) & (kpos < n))[:, None, :] & (jnp.abs(qpos[:, :, None] - kpos[:, None, :]) <= B_WINDOW)
    scale = B_DIM ** -0.5
    s_band = jnp.einsum('bnqhgd,bnkhd->bnhgqk', qb, kb).astype(f32) * scale
    s_band = jnp.where(valid[None, :, None, None], s_band, NEG_INF)
    s_ctx = jnp.einsum('bnqhgd,bkhd->bnhgqk', qb, k_ctx).astype(f32) * scale
    s_sink = jnp.broadcast_to(sink.astype(f32).reshape(1, 1, B_KV_HEADS, G, 1, 1), s_ctx.shape[:-1] + (1,))
    p = jax.nn.softmax(jnp.concatenate([s_sink, s_ctx, s_band], axis=-1), axis=-1)
    m = k_ctx.shape[1]
    p_ctx = p[..., 1:1 + m].astype(v.dtype)
    p_band = p[..., 1 + m:].astype(v.dtype)
    o = (jnp.einsum('bnhgqk,bkhd->bnqhgd', p_ctx, v_ctx)
         + jnp.einsum('bnhgqk,bnkhd->bnqhgd', p_band, vb))
    return o.reshape(bn, n, B_HEADS * B_DIM)


def sink_attention_context(q, k, v, sink):
    bn, m = q.shape[:2]
    G = B_HEADS // B_KV_HEADS
    qg = q.reshape(bn, m, B_KV_HEADS, G, B_DIM)
    s = jnp.einsum('bqhgd,bkhd->bhgqk', qg, k).astype(f32) * (B_DIM ** -0.5)
    s_sink = jnp.broadcast_to(sink.astype(f32).reshape(1, B_KV_HEADS, G, 1, 1), s.shape[:-1] + (1,))
    p = jax.nn.softmax(jnp.concatenate([s_sink, s], axis=-1), axis=-1)[..., 1:]
    o = jnp.einsum('bhgqk,bkhd->bqhgd', p.astype(v.dtype), v)
    return o.reshape(bn, m, B_HEADS * B_DIM)


def short_conv(x, w):
    return lax.conv_general_dilated(x, w[:, None, :].astype(x.dtype), window_strides=(1,), padding='SAME',
                                    dimension_numbers=('NWC', 'WIO', 'NWC'), feature_group_count=x.shape[-1])


def gdn_chunk_scan(q, k, v, g, beta, s0):
    bn, T, H, _ = q.shape
    dv = v.shape[-1]
    L = C_CHUNK
    nc = T // L

    def chunk(t):
        return jnp.swapaxes(t.reshape((bn, nc, L) + t.shape[2:]), 2, 3)

    q, k, v, g, beta = chunk(q), chunk(k), chunk(v), chunk(g), chunk(beta)
    Gc = jnp.cumsum(g, axis=-1)
    incl = jnp.tril(jnp.ones((L, L), bool))
    strict = jnp.tril(jnp.ones((L, L), bool), -1)
    decay = jnp.exp(jnp.where(incl, Gc[..., :, None] - Gc[..., None, :], NEG_INF))
    kk = jnp.einsum('bchid,bchjd->bchij', k, k)
    A = jnp.where(strict, beta[..., :, None] * kk * decay, 0.0)
    eye = jnp.eye(L, dtype=f32)
    Tm = lax.linalg.triangular_solve(A + eye, jnp.broadcast_to(eye, A.shape), left_side=True, lower=True)
    eG = jnp.exp(Gc)[..., None]
    u = Tm @ (beta[..., None] * v)
    w = Tm @ (beta[..., None] * eG * k)
    qk = jnp.where(incl, jnp.einsum('bchid,bchjd->bchij', q, k) * decay, 0.0)
    q_dec = q * eG
    k_dec = k * jnp.exp(Gc[..., -1:] - Gc)[..., None]
    g_last = jnp.exp(Gc[..., -1])
    xs = tuple(jnp.moveaxis(t, 1, 0) for t in (u, w, qk, q_dec, k_dec, g_last))

    def step(S, inp):
        u_c, w_c, qk_c, qd_c, kd_c, gl_c = inp
        v_new = u_c - w_c @ S
        o_c = qd_c @ S + qk_c @ v_new
        S = S * gl_c[..., None, None] + jnp.einsum('bhld,bhle->bhde', kd_c, v_new)
        return S, o_c

    s_fin, o = lax.scan(step, s0, xs)
    o = jnp.transpose(o, (1, 0, 3, 2, 4)).reshape(bn, T, H, dv)
    return o, s_fin


def gdn_inputs(p, conv_w):
    bn, T = p[6].shape[:2]
    qkv = jax.nn.silu(short_conv(p[6], conv_w)).astype(f32)
    q, k, v = jnp.split(qkv, 3, axis=-1)
    q = l2norm(q.reshape(bn, T, C_HEADS, C_DK)) * (C_DK ** -0.5)
    k = l2norm(k.reshape(bn, T, C_HEADS, C_DK))
    v = v.reshape(bn, T, C_HEADS, C_DV)
    a = p[8].astype(f32).reshape(bn, T, 2, C_HEADS)
    b = p[9].astype(f32).reshape(bn, T, 2, C_HEADS)
    return q, k, v, a, b


def gdn_bidirectional(ctx_in, lat_in, A_log, dt_bias):
    qc, kc, vc, ac, bc = ctx_in
    ql, kl, vl, al, bl = lat_in
    bn = ql.shape[0]
    o_c = 0.0
    o_l = 0.0
    for d in range(2):
        fl = (lambda t: t[:, ::-1]) if d == 1 else (lambda t: t)
        a_rate = jnp.exp(A_log[d].astype(f32))
        dtb = dt_bias[d].astype(f32)
        gc = -a_rate * jax.nn.softplus(ac[:, :, d] + dtb)
        gl = -a_rate * jax.nn.softplus(al[:, :, d] + dtb)
        s0 = jnp.zeros((bn, C_HEADS, C_DK, C_DV), f32)
        oc, s_ctx = gdn_chunk_scan(fl(qc), fl(kc), fl(vc), fl(gc), fl(jax.nn.sigmoid(bc[:, :, d])), s0)
        ol, _ = gdn_chunk_scan(fl(ql), fl(kl), fl(vl), fl(gl), fl(jax.nn.sigmoid(bl[:, :, d])), s_ctx)
        o_c = o_c + fl(oc)
        o_l = o_l + fl(ol)
    return o_c, o_l


def hybrid_mixer(hn_ctx, hn_lat, w_in, w_out, a_qnorm, a_knorm, a_lambda, a_subln, lam_init,
                 b_qnorm, b_knorm, b_sink, c_conv, c_A_log, c_dt_bias, c_onorm, need_ctx):
    dt = hn_lat.dtype
    bn, n = hn_lat.shape[:2]
    m = hn_ctx.shape[1]
    pl = jnp.split(hn_lat @ w_in, IN_OFFSETS, axis=-1)
    pc = jnp.split(hn_ctx @ w_in, IN_OFFSETS, axis=-1)

    def a_heads(t, w):
        return rms_norm(t.reshape(t.shape[:2] + (A_HEADS, 2, A_DIM)), w)

    qa_l = apply_rope_2d(a_heads(pl[0], a_qnorm))
    ka_l = apply_rope_2d(a_heads(pl[1], a_knorm))
    qa_c = a_heads(pc[0], a_qnorm)
    ka_c = a_heads(pc[1], a_knorm)
    va_l = pl[2].reshape(bn, n, A_HEADS, A_VDIM)
    va_c = pc[2].reshape(bn, m, A_HEADS, A_VDIM)
    lf = a_lambda.astype(f32)
    lam = jnp.exp(jnp.sum(lf[0] * lf[1])) - jnp.exp(jnp.sum(lf[2] * lf[3])) + lam_init

    def a_out(o):
        return (rms_norm(o, a_subln) * (1.0 - lam_init)).reshape(o.shape[:2] + (A_W,)).astype(dt)

    ya_l = a_out(diff_attention_blocked(qa_l, jnp.concatenate([ka_c, ka_l], axis=1),
                                        jnp.concatenate([va_c, va_l], axis=1), lam))

    qb_l = apply_rope_2d(rms_norm(pl[3].reshape(bn, n, B_HEADS, B_DIM), b_qnorm))
    kb_l = apply_rope_2d(rms_norm(pl[4].reshape(bn, n, B_KV_HEADS, B_DIM), b_knorm))
    vb_l = pl[5].reshape(bn, n, B_KV_HEADS, B_DIM)
    kb_c = rms_norm(pc[4].reshape(bn, m, B_KV_HEADS, B_DIM), b_knorm)
    vb_c = pc[5].reshape(bn, m, B_KV_HEADS, B_DIM)
    yb_l = window_sink_attention_latent(qb_l, kb_l, vb_l, kb_c, vb_c, b_sink).astype(dt)

    oc_c, oc_l = gdn_bidirectional(gdn_inputs(pc, c_conv), gdn_inputs(pl, c_conv), c_A_log, c_dt_bias)

    def c_out(o, gate):
        gz = jax.nn.silu(gate.astype(f32).reshape(o.shape))
        return (rms_norm(o, c_onorm) * gz).reshape(o.shape[:2] + (C_W,)).astype(dt)

    yc_l = c_out(oc_l, pl[7])
    y_lat = jnp.concatenate([ya_l, yb_l, yc_l], axis=-1) @ w_out

    if not need_ctx:
        return None, y_lat
    ya_c = a_out(diff_attention(qa_c, ka_c, va_c, lam))
    qb_c = rms_norm(pc[3].reshape(bn, m, B_HEADS, B_DIM), b_qnorm)
    yb_c = sink_attention_context(qb_c, kb_c, vb_c, b_sink).astype(dt)
    yc_c = c_out(oc_c, pc[7])
    y_ctx = jnp.concatenate([ya_c, yb_c, yc_c], axis=-1) @ w_out
    return y_ctx, y_lat


def setup_inputs(seed: int = 0) -> dict:
    key = jax.random.key(seed)
    ks = jax.random.split(key, 24)
    D = D_MODEL

    def nrm(k, shape, scale):
        return jax.random.normal(k, shape, f32) * scale

    dt = jnp.exp(jax.random.uniform(ks[22], (DEPTH, 2, C_HEADS), f32, np.log(1e-3), np.log(1e-1)))
    return {
        'x': nrm(ks[0], (BATCH, SEQ, D), 1.0),
        'c': nrm(ks[1], (BATCH, D), 1.0),
        'ctx': nrm(ks[2], (BATCH, CTX_LEN, D), 1.0),
        'c_ctx': nrm(ks[3], (D,), 1.0),
        'w_mod': nrm(ks[4], (DEPTH, D, 9 * D), 0.5 * D ** -0.5),
        'b_mod': nrm(ks[5], (DEPTH, 9 * D), 0.02),
        'ffn1_w1': nrm(ks[6], (DEPTH, D, 2 * D_FF), D ** -0.5),
        'ffn1_w2': nrm(ks[7], (DEPTH, D_FF, D), D_FF ** -0.5),
        'ffn2_w1': nrm(ks[8], (DEPTH, D, 2 * D_FF), D ** -0.5),
        'ffn2_w2': nrm(ks[9], (DEPTH, D_FF, D), D_FF ** -0.5),
        'w_in': nrm(ks[10], (DEPTH, D, IN_COLS), D ** -0.5),
        'w_out': nrm(ks[11], (DEPTH, D_MIX, D), D_MIX ** -0.5),
        'a_qnorm': 1.0 + nrm(ks[12], (DEPTH, A_DIM), 0.02),
        'a_knorm': 1.0 + nrm(ks[13], (DEPTH, A_DIM), 0.02),
        'a_lambda': nrm(ks[14], (DEPTH, 4, A_DIM), 0.1),
        'a_subln': 1.0 + nrm(ks[15], (DEPTH, A_VDIM), 0.02),
        'b_qnorm': 1.0 + nrm(ks[16], (DEPTH, B_DIM), 0.02),
        'b_knorm': 1.0 + nrm(ks[17], (DEPTH, B_DIM), 0.02),
        'b_sink': nrm(ks[18], (DEPTH, B_HEADS), 0.5),
        'c_conv': nrm(ks[19], (DEPTH, C_CONV, 3 * C_W), C_CONV ** -0.5),
        'c_A_log': jnp.log(jax.random.uniform(ks[20], (DEPTH, 2, C_HEADS), f32, 1.0, 16.0)),
        'c_dt_bias': dt + jnp.log(-jnp.expm1(-dt)),
        'c_onorm': 1.0 + nrm(ks[21], (DEPTH, C_DV), 0.02),
    }


def reference(x, c, ctx, c_ctx, w_mod, b_mod, ffn1_w1, ffn1_w2, ffn2_w1, ffn2_w2, w_in, w_out,
              a_qnorm, a_knorm, a_lambda, a_subln, b_qnorm, b_knorm, b_sink,
              c_conv, c_A_log, c_dt_bias, c_onorm):
    h_lat, h_ctx = x, ctx
    silu_c = jax.nn.silu(c)
    silu_cc = jax.nn.silu(c_ctx)
    for l in range(DEPTH):
        last = l == DEPTH - 1
        lam_init = 0.8 - 0.6 * float(np.exp(-0.3 * l))
        mod_lat = (silu_c @ w_mod[l] + b_mod[l]).reshape(c.shape[0], 1, 9, D_MODEL)
        mod_ctx = (silu_cc @ w_mod[l] + b_mod[l]).reshape(1, 1, 9, D_MODEL)
        h_lat = ffn_half_step(h_lat, mod_lat, 0, ffn1_w1[l], ffn1_w2[l])
        h_ctx = ffn_half_step(h_ctx, mod_ctx, 0, ffn1_w1[l], ffn1_w2[l])
        y_ctx, y_lat = hybrid_mixer(modulate(h_ctx, mod_ctx, 3), modulate(h_lat, mod_lat, 3),
                                    w_in[l], w_out[l], a_qnorm[l], a_knorm[l], a_lambda[l], a_subln[l], lam_init,
                                    b_qnorm[l], b_knorm[l], b_sink[l], c_conv[l], c_A_log[l], c_dt_bias[l],
                                    c_onorm[l], not last)
        h_lat = h_lat + mod_lat[:, :, 5] * y_lat
        h_lat = ffn_half_step(h_lat, mod_lat, 6, ffn2_w1[l], ffn2_w2[l])
        if not last:
            h_ctx = h_ctx + mod_ctx[:, :, 5] * y_ctx
            h_ctx = ffn_half_step(h_ctx, mod_ctx, 6, ffn2_w1[l], ffn2_w2[l])
    return h_lat
```

```python
import functools

import numpy as np
import jax
import jax.numpy as jnp
from jax import lax
from jax.experimental import pallas as pl
from jax.experimental.pallas import tpu as pltpu

f32 = jnp.float32
bf16 = jnp.bfloat16

D_MODEL = 1024
DEPTH = 2
GRID_W = 64
EPS = 1e-6
NEG_INF = -1e30
ROPE_THETA = 10000.0
D_FF = 2816
A_HEADS, A_DIM, A_VDIM = 4, 32, 64
A_W = A_HEADS * A_VDIM
B_HEADS, B_KV_HEADS, B_DIM = 6, 2, 64
B_GROUP = B_HEADS // B_KV_HEADS
B_BLOCK = 128
C_HEADS, C_DK, C_CONV, C_CHUNK = 6, 64, 5, 64
C_W = C_HEADS * C_DK
IN_SIZES = (256, 256, 256, 384, 128, 128, 1152, 384, 12, 12)
LANE = 128
B_WIDE = B_HEADS * LANE
SEG = dict(qa=(0, 256), ka=(256, 256), va=(512, 256), qb=(768, B_WIDE), kb=(1536, 128), vb=(1664, 128),
           cq=(1792, 1152), gate=(2944, 384), a=(3328, 128), b=(3456, 128))
IN_WIDE = 3584
OUT_WIDE = A_W + B_WIDE + C_W
VMEM_LIMIT = 56 * 1024 * 1024


def _cparams(n_axes):
    return pltpu.CompilerParams(dimension_semantics=("arbitrary",) * n_axes, vmem_limit_bytes=VMEM_LIMIT)


def _dot(a, b):
    return jnp.dot(a.astype(bf16), b.astype(bf16), preferred_element_type=f32)


def _dot_nt(a, b):
    return lax.dot_general(a.astype(bf16), b.astype(bf16), (((1,), (1,)), ((), ())), preferred_element_type=f32)


def _dot_tn(a, b):
    return lax.dot_general(a.astype(bf16), b.astype(bf16), (((0,), (0,)), ((), ())), preferred_element_type=f32)


def _split(x, n):
    parts = []
    for _ in range(n - 1):
        p = x.astype(bf16)
        parts.append(p)
        x = x - p.astype(f32)
    parts.append(x.astype(bf16))
    return parts


def _dot_xl(x, m, n=3):
    return sum(jnp.dot(p, m, preferred_element_type=f32) for p in _split(x, n))


def _dot_lx(m, x, n=3):
    return sum(jnp.dot(m, p, preferred_element_type=f32) for p in _split(x, n))


def _group_sum(xx, bd):
    w = bd.shape[0]
    cols = [_dot_xl(xx[:, j:j + w], bd, 2) for j in range(0, xx.shape[1], w)]
    return cols[0] if len(cols) == 1 else jnp.concatenate(cols, axis=1)


def _silu(x):
    return x * jax.nn.sigmoid(x)


def _iota(shape, dim):
    return lax.broadcasted_iota(jnp.int32, shape, dim)


def _modulated_norm(h, shift, scale):
    hn = h * lax.rsqrt(jnp.mean(h * h, axis=-1, keepdims=True) + EPS)
    return hn * (1.0 + scale) + shift


def _mod_kernel(c_ref, w_ref, b_ref, o_ref):
    s = _silu(c_ref[...])
    w = w_ref[0]
    s_hi, s_lo = _split(s, 2)
    w_hi, w_lo = _split(w, 2)
    acc = jnp.dot(s_hi, w_hi, preferred_element_type=f32)
    acc += jnp.dot(s_hi, w_lo, preferred_element_type=f32)
    acc += jnp.dot(s_lo, w_hi, preferred_element_type=f32)
    o_ref[0] = acc + b_ref[0]


def _modulation(cvec, w_mod, b_mod):
    depth, d, n = w_mod.shape
    tn = 1024
    return pl.pallas_call(
        _mod_kernel,
        grid=(depth, n // tn),
        in_specs=[pl.BlockSpec((cvec.shape[0], d), lambda l, j: (0, 0)),
                  pl.BlockSpec((1, d, tn), lambda l, j: (l, 0, j)),
                  pl.BlockSpec((1, 1, tn), lambda l, j: (l, 0, j))],
        out_specs=pl.BlockSpec((1, cvec.shape[0], tn), lambda l, j: (l, 0, j)),
        out_shape=jax.ShapeDtypeStruct((depth, cvec.shape[0], n), f32),
        compiler_params=_cparams(2), name="modulation",
    )(cvec, w_mod, b_mod.reshape(depth, 1, n))


def _ffn_kernel(h_ref, mod_ref, w1_ref, w2_ref, o_ref, *, idx, n_chunks):
    h = h_ref[...]
    mod = mod_ref[0]
    hn = _modulated_norm(h, mod[idx:idx + 1], mod[idx + 1:idx + 2]).astype(bf16)
    ck = D_FF // n_chunks
    acc = None
    for c in range(n_chunks):
        g = jnp.dot(hn, w1_ref[:, c * ck:(c + 1) * ck], preferred_element_type=f32)
        u = jnp.dot(hn, w1_ref[:, D_FF + c * ck:D_FF + (c + 1) * ck], preferred_element_type=f32)
        a = (_silu(g) * u).astype(bf16)
        part = jnp.dot(a, w2_ref[c * ck:(c + 1) * ck, :], preferred_element_type=f32)
        acc = part if acc is None else acc + part
    o_ref[...] = h + (0.5 * mod[idx + 2:idx + 3]) * acc


def _ffn(h, mod, w1, w2, idx, lay, lat_only):
    tm, d = lay["tm"], D_MODEL
    off = lay["nc"] // tm if lat_only else 0
    n_tiles = h.shape[0] // tm - off
    group = functools.partial(_mod_group, lay=lay, off=off)
    return pl.pallas_call(
        functools.partial(_ffn_kernel, idx=idx, n_chunks=2),
        grid=(n_tiles,),
        in_specs=[pl.BlockSpec((tm, d), lambda i: (i + off, 0)),
                  pl.BlockSpec((1, 9, d), lambda i: (group(i), 0, 0)),
                  pl.BlockSpec((d, 2 * D_FF), lambda i: (0, 0), pipeline_mode=pl.Buffered(1)),
                  pl.BlockSpec((D_FF, d), lambda i: (0, 0), pipeline_mode=pl.Buffered(1))],
        out_specs=pl.BlockSpec((tm, d), lambda i: (i, 0)),
        out_shape=jax.ShapeDtypeStruct((n_tiles * tm, d), f32),
        compiler_params=_cparams(1), name=f"ffn_half_step_{idx}",
    )(h, mod, w1, w2)


def _mod_group(i, lay, off):
    r = (i + off) * lay["tm"]
    return jnp.where(r < lay["nc"], lay["b"], (r - lay["nc"]) // lay["seq"])


def _rope(x, cos, sin, quarter):
    w = x.shape[1]
    reps = w // LANE
    if reps > 1:
        cos = jnp.concatenate([cos] * reps, axis=1)
        sin = jnp.concatenate([sin] * reps, axis=1)
    first = (_iota((1, w), 1) % (2 * quarter)) < quarter
    swapped = jnp.where(first, pltpu.roll(x, w - quarter, 1), pltpu.roll(x, quarter, 1))
    return x * cos + swapped * sin


def _inproj_kernel(h_ref, mod_ref, w_ref, nw_ref, ca_ref, sa_ref, cb_ref, sb_ref, bd32_ref, bd64_ref,
                   qa_o, ka_o, va_o, qb_o, kb_o, vb_o, cq_o, gate_o, a_o, b_o):
    mod = mod_ref[0]
    hn = _modulated_norm(h_ref[...], mod[3:4], mod[4:5]).astype(bf16)

    def proj(name):
        off, width = SEG[name]
        return jnp.dot(hn, w_ref[:, off:off + width], preferred_element_type=f32)

    def normed(x, bd, group, wrow):
        s = _group_sum(x * x, bd)
        return x * lax.rsqrt(s * (1.0 / group) + EPS) * wrow

    nw = nw_ref[...]
    bd32, bd64 = bd32_ref[...], bd64_ref[...]
    ca, sa, cb, sb = ca_ref[...], sa_ref[...], cb_ref[...], sb_ref[...]
    qa_o[...] = _rope(normed(proj("qa"), bd32, A_DIM, nw[0:1, :256]), ca, sa, A_DIM // 4).astype(bf16)
    ka_o[...] = _rope(normed(proj("ka"), bd32, A_DIM, nw[1:2, :256]), ca, sa, A_DIM // 4).astype(bf16)
    va_o[...] = proj("va").astype(bf16)
    qb = proj("qb")
    wq = jnp.concatenate([nw[2:3, :256]] * (B_WIDE // 256), axis=1)
    qb_o[...] = _rope(normed(qb, bd64, B_DIM, wq), cb, sb, B_DIM // 4).astype(bf16)
    kb_o[...] = _rope(normed(proj("kb"), bd64[:LANE, :LANE], B_DIM, nw[3:4, :LANE]), cb, sb, B_DIM // 4).astype(bf16)
    vb_o[...] = proj("vb").astype(bf16)
    cq_o[...] = proj("cq")
    gate_o[...] = proj("gate")
    a_o[...] = proj("a")
    b_o[...] = proj("b")


def _inproj(h, mod, w_in, nw, tabs, bd32, bd64, lay):
    tm, d = lay["tm"], D_MODEL
    n = h.shape[0]
    n_tiles = n // tm
    nct = lay["nc"] // tm
    spt = lay["seq"] // tm
    group = functools.partial(_mod_group, lay=lay, off=0)

    def tab_idx(i):
        return jnp.where(i < nct, 0, 1 + (i - nct) % spt)

    row = lambda w: pl.BlockSpec((tm, w), lambda i: (i, 0))
    const = lambda shape: pl.BlockSpec(shape, lambda i: (0,) * len(shape))
    tab = pl.BlockSpec((tm, LANE), lambda i: (tab_idx(i), 0))
    names = ("qa", "ka", "va", "qb", "kb", "vb", "cq", "gate", "a", "b")
    dts = (bf16,) * 6 + (f32,) * 4
    return pl.pallas_call(
        _inproj_kernel,
        grid=(n_tiles,),
        in_specs=[row(d), pl.BlockSpec((1, 9, d), lambda i: (group(i), 0, 0)),
                  pl.BlockSpec((d, IN_WIDE), lambda i: (0, 0), pipeline_mode=pl.Buffered(1)),
                  const(nw.shape), tab, tab, tab, tab, const(bd32.shape), const(bd64.shape)],
        out_specs=[row(SEG[k][1]) for k in names],
        out_shape=[jax.ShapeDtypeStruct((n, SEG[k][1]), dt) for k, dt in zip(names, dts)],
        compiler_params=_cparams(1), name="mixer_in_proj",
    )(h, mod, w_in, nw, *tabs, bd32, bd64)


def _attn_a_kernel(lam_ref, q_ref, *refs, n_seg):
    k_refs, v_refs, o_ref = refs[:n_seg], refs[n_seg:2 * n_seg], refs[2 * n_seg]
    q = q_ref[...]
    lam = lam_ref[0]
    lane = _iota((1, A_W), 1)
    ks = [r[...] for r in k_refs]
    vs = [r[...] for r in v_refs]
    out = jnp.zeros(q.shape, f32)
    for h in range(A_HEADS):
        es, rs = [], []
        for m in range(2):
            qm = jnp.where(lane // A_DIM == 2 * h + m, q, jnp.zeros_like(q))
            s = [_dot_nt(qm, k) for k in ks]
            mx = functools.reduce(jnp.maximum, [jnp.max(x, axis=-1, keepdims=True) for x in s])
            e = [jnp.exp(x - mx) for x in s]
            den = functools.reduce(jnp.add, [jnp.sum(x, axis=-1, keepdims=True) for x in e])
            es.append(e)
            rs.append(1.0 / den)
        r1 = lam * rs[1]
        oh = None
        for e0, e1, v in zip(es[0], es[1], vs):
            w = (e0 * rs[0] - e1 * r1).astype(bf16)
            part = jnp.dot(w, v, preferred_element_type=f32)
            oh = part if oh is None else oh + part
        out = jnp.where(lane // A_VDIM == h, oh, out)
    o_ref[...] = out


def _attn_a(lam, qa, ka, va, lay, latent):
    b, ctx, seq, nc = lay["b"], lay["ctx"], lay["seq"], lay["nc"]
    smem = pl.BlockSpec(memory_space=pltpu.SMEM)
    if latent:
        tq = 256
        qpb = seq // tq
        grid = (b, qpb)
        qspec = pl.BlockSpec((tq, A_W), lambda i, j: (nc // tq + i * qpb + j, 0))
        kv = [pl.BlockSpec((ctx, A_W), lambda i, j: (i, 0)), pl.BlockSpec((seq, A_W), lambda i, j: (nc // seq + i, 0))]
        ospec = pl.BlockSpec((tq, A_W), lambda i, j: (i * qpb + j, 0))
        rows = b * seq
    else:
        grid = (b, 1)
        qspec = pl.BlockSpec((ctx, A_W), lambda i, j: (i, 0))
        kv = [pl.BlockSpec((ctx, A_W), lambda i, j: (i, 0))]
        ospec = qspec
        rows = nc
    n_seg = len(kv)
    return pl.pallas_call(
        functools.partial(_attn_a_kernel, n_seg=n_seg),
        grid=grid,
        in_specs=[smem, qspec] + kv + kv,
        out_specs=ospec,
        out_shape=jax.ShapeDtypeStruct((rows, A_W), f32),
        compiler_params=_cparams(2), name="diff_attention_lat" if latent else "diff_attention_ctx",
    )(lam, qa, *([ka] * n_seg), *([va] * n_seg))


def _attn_b_kernel(sink_ref, q_ref, kc_ref, vc_ref, *refs, latent):
    o_ref = refs[-1]
    q = q_ref[...]
    rows = q.shape[0]
    lane = _iota((1, LANE), 1)
    keys, vals, masks = [kc_ref[...]], [vc_ref[...]], [None]
    if latent:
        kl_ref, vl_ref = refs[0], refs[1]
        t = pl.program_id(1)
        nb = pl.num_programs(1)
        r = _iota((B_GROUP * rows, B_BLOCK), 0) % rows
        c = _iota((B_GROUP * rows, B_BLOCK), 1)
        for off in (-1, 0, 1):
            blk = jnp.clip(t + off, 0, nb - 1)
            start = pl.multiple_of(blk * B_BLOCK, B_BLOCK)
            keys.append(kl_ref[pl.ds(start, B_BLOCK), :])
            vals.append(vl_ref[pl.ds(start, B_BLOCK), :])
            inside = jnp.logical_and(t + off >= 0, t + off < nb)
            if off == -1:
                masks.append(jnp.logical_and(c >= r, inside))
            elif off == 1:
                masks.append(jnp.logical_and(c <= r, inside))
            else:
                masks.append(None)
    for g in range(B_KV_HEADS):
        heads = range(g * B_GROUP, (g + 1) * B_GROUP)
        qg = jnp.concatenate([q[:, h * LANE:(h + 1) * LANE] for h in heads], axis=0)
        sink = jnp.concatenate([jnp.full((rows, 1), sink_ref[h], f32) for h in heads], axis=0)
        s = []
        for k, msk in zip(keys, masks):
            x = _dot_nt(qg, k)
            s.append(x if msk is None else jnp.where(msk, x, NEG_INF))
        mx = functools.reduce(jnp.maximum, [jnp.max(x, axis=-1, keepdims=True) for x in s] + [sink])
        e = [jnp.exp(x - mx) for x in s]
        den = functools.reduce(jnp.add, [jnp.sum(x, axis=-1, keepdims=True) for x in e]) + jnp.exp(sink - mx)
        rden = 1.0 / den
        og = None
        for x, v in zip(e, vals):
            part = jnp.dot((x * rden).astype(bf16), v, preferred_element_type=f32)
            og = part if og is None else og + part
        og = jnp.where(lane // B_DIM == g, og, 0.0).astype(bf16)
        for n, h in enumerate(heads):
            o_ref[:, h * LANE:(h + 1) * LANE] = og[n * rows:(n + 1) * rows]


def _attn_b(sink, qb, kb, vb, lay, latent):
    b, ctx, seq, nc = lay["b"], lay["ctx"], lay["seq"], lay["nc"]
    smem = pl.BlockSpec(memory_space=pltpu.SMEM)
    cspec = pl.BlockSpec((ctx, LANE), lambda i, j: (i, 0))
    if latent:
        nb = seq // B_BLOCK
        grid = (b, nb)
        qspec = pl.BlockSpec((B_BLOCK, B_WIDE), lambda i, j: (nc // B_BLOCK + i * nb + j, 0))
        lspec = pl.BlockSpec((seq, LANE), lambda i, j: (nc // seq + i, 0))
        in_specs = [smem, qspec, cspec, cspec, lspec, lspec]
        args = (sink, qb, kb, vb, kb, vb)
        ospec = pl.BlockSpec((B_BLOCK, B_WIDE), lambda i, j: (i * nb + j, 0))
        rows = b * seq
    else:
        grid = (b, 1)
        qspec = pl.BlockSpec((ctx, B_WIDE), lambda i, j: (i, 0))
        in_specs = [smem, qspec, cspec, cspec]
        args = (sink, qb, kb, vb)
        ospec = qspec
        rows = nc
    return pl.pallas_call(
        functools.partial(_attn_b_kernel, latent=latent),
        grid=grid, in_specs=in_specs, out_specs=ospec,
        out_shape=jax.ShapeDtypeStruct((rows, B_WIDE), bf16),
        compiler_params=_cparams(2), name="window_attention_lat" if latent else "sink_attention_ctx",
    )(*args)


HALO = 8


def _gdn_prep_kernel(x_ref, prev_ref, next_ref, a_ref, b_ref, cw_ref, par_ref, bd64_ref,
                     q_o, k_o, v_o, g_o, beta_o, *, nct, cpt, spt):
    i = pl.program_id(0)
    tm = x_ref.shape[0]
    j = jnp.where(i < nct, i % cpt, (i - nct) % spt)
    per_seq = jnp.where(i < nct, cpt, spt)
    first = j == 0
    last = j == per_seq - 1
    prev = jnp.where(first, 0.0, prev_ref[...])
    nxt = jnp.where(last, 0.0, next_ref[...])
    xx = jnp.concatenate([prev, x_ref[...], nxt], axis=0)
    cw = cw_ref[...]
    rows = xx.shape[0]
    y = None
    for tap in range(C_CONV):
        shift = (C_CONV // 2 - tap) % rows
        sh = xx if shift == 0 else pltpu.roll(xx, shift, 0)
        term = sh[HALO:HALO + tm] * cw[tap:tap + 1]
        y = term if y is None else y + term
    y = _silu(y)
    bd64 = bd64_ref[...]

    def l2n(t):
        return t * lax.rsqrt(_group_sum(t * t, bd64) + EPS)

    q_o[...] = l2n(y[:, :C_W]) * (C_DK ** -0.5)
    k_o[...] = l2n(y[:, C_W:2 * C_W])
    v_o[...] = y[:, 2 * C_W:]
    par = par_ref[...]
    z = a_ref[...] + par[1:2]
    softplus = jnp.maximum(z, 0.0) + jnp.log1p(jnp.exp(-jnp.abs(z)))
    g_o[...] = -par[0:1] * softplus
    beta_o[...] = jax.nn.sigmoid(b_ref[...])


def _gdn_prep(cq, a, bb, cw, par, bd64, lay):
    tm = 256
    n = cq.shape[0]
    n_tiles = n // tm
    nct, spt = lay["nc"] // tm, lay["seq"] // tm
    hb = tm // HALO
    last_blk = n // HALO - 1
    row = lambda w: pl.BlockSpec((tm, w), lambda i: (i, 0))
    const = lambda shape: pl.BlockSpec(shape, lambda i: (0,) * len(shape))
    return pl.pallas_call(
        functools.partial(_gdn_prep_kernel, nct=nct, cpt=lay["ctx"] // tm, spt=spt),
        grid=(n_tiles,),
        in_specs=[row(3 * C_W),
                  pl.BlockSpec((HALO, 3 * C_W), lambda i: (jnp.maximum(i * hb - 1, 0), 0)),
                  pl.BlockSpec((HALO, 3 * C_W), lambda i: (jnp.minimum((i + 1) * hb, last_blk), 0)),
                  row(LANE), row(LANE), const(cw.shape), const(par.shape), const(bd64.shape)],
        out_specs=[row(C_W), row(C_W), row(C_W), row(LANE), row(LANE)],
        out_shape=[jax.ShapeDtypeStruct((n, C_W), f32)] * 3 + [jax.ShapeDtypeStruct((n, LANE), f32)] * 2,
        compiler_params=_cparams(1), name="gdn_inputs",
    )(cq, cq, cq, a, bb, cw, par, bd64)


GW = 4 * C_DK


def _bd(x):
    t = jnp.concatenate([x.astype(bf16)] * 4, axis=0)
    same = _iota((GW, GW), 0) // C_DK == _iota((GW, GW), 1) // C_DK
    return jnp.where(same, t, jnp.zeros_like(t))


def _dot_2(lhs, rhs):
    l_hi, l_lo = _split(lhs, 2)
    r_hi, r_lo = _split(rhs, 2)
    bd_hi = _bd(r_hi)
    out = jnp.dot(l_hi, bd_hi, preferred_element_type=f32)
    out += jnp.dot(l_lo, bd_hi, preferred_element_type=f32)
    out += jnp.dot(l_hi, _bd(r_lo), preferred_element_type=f32)
    return out


def _gdn_group_step(q, k, v, g, beta, s_ref, fwd_lane):
    L = C_CHUNK
    ri = _iota((L, GW), 0)
    ci = _iota((L, GW), 1) % C_DK
    ahead = jnp.where(fwd_lane, ri - ci, ci - ri)
    incl = ahead >= 0
    strict = ahead > 0
    eye = ci == ri
    r64 = _iota((L, L), 0)
    c64 = _iota((L, L), 1)
    tri_l = (c64 <= r64).astype(bf16)
    tri_u = (c64 >= r64).astype(bf16)
    gparts = _split(g, 3)
    gc_f = sum(jnp.dot(tri_l, p, preferred_element_type=f32) for p in gparts)
    gc_b = sum(jnp.dot(tri_u, p, preferred_element_type=f32) for p in gparts)
    gc = jnp.where(fwd_lane, gc_f, gc_b)
    gr = _dot_lx(jnp.ones((L, L), bf16), jnp.where(eye, gc, 0.0))
    dec = jnp.exp(jnp.where(incl, gc - gr, NEG_INF))
    kq = _dot_nt(jnp.concatenate([k, q], axis=0), _bd(k))
    a = jnp.where(strict, beta * kq[:L] * dec, 0.0)
    x = jnp.where(eye, 1.0, 0.0) - a
    p = _dot_2(a, a)
    for _ in range(4):
        xp = _dot_2(jnp.concatenate([x, p], axis=0), p)
        x = x + xp[:L]
        p = xp[L:]
    tm = x + _dot_2(x, p)
    eg = jnp.exp(gc)
    uw = _dot(tm, jnp.concatenate([_bd(beta * v), _bd(beta * eg * k)], axis=1))
    u, w = uw[:, :GW], uw[:, GW:]
    qk = jnp.where(incl, kq[L:] * dec, 0.0)
    g_last = jnp.where(fwd_lane, gc[L - 1:L], gc[0:1])
    kd = k * jnp.exp(g_last - gc)
    s = s_ref[...]
    ws = _dot(jnp.concatenate([w, q * eg], axis=0), s)
    v_new = u - ws[:L]
    o = ws[L:] + _dot(qk, _bd(v_new))
    same = _iota((GW, GW), 0) // C_DK == _iota((GW, GW), 1) // C_DK
    s_ref[...] = s * jnp.exp(g_last) + jnp.where(same, _dot_tn(kd, v_new), 0.0)
    return o


def _gdn_kernel(qc, kc, vc, gc, bc, ql, kl, vl, gl, bl, ef_ref, eb_ref, oc_ref, ol_ref, s0, s1, s2):
    for s in (s0, s1, s2):
        s[...] = jnp.zeros(s.shape, f32)
    lane = _iota((1, GW), 1)
    all_fwd = lane >= 0
    all_bwd = lane < 0
    half = lane < LANE
    ef, eb = ef_ref[...], eb_ref[...]

    def run(q_ref, k_ref, v_ref, g_ref, b_ref, o_ref):
        n_chunks = q_ref.shape[0] // C_CHUNK
        o_ref[...] = jnp.zeros(o_ref.shape, f32)

        def step(t, carry):
            rf = pl.multiple_of(t * C_CHUNK, C_CHUNK)
            rb = pl.multiple_of((n_chunks - 1 - t) * C_CHUNK, C_CHUNK)
            sl_f, sl_b = pl.ds(rf, C_CHUNK), pl.ds(rb, C_CHUNK)
            gb_f = _dot_xl(jnp.concatenate([g_ref[sl_f, :], b_ref[sl_f, :]], axis=0), ef)
            gb_b = _dot_xl(jnp.concatenate([g_ref[sl_b, :], b_ref[sl_b, :]], axis=0), eb)
            data_f = [r[sl_f, :] for r in (q_ref, k_ref, v_ref)] + [gb_f[:C_CHUNK], gb_f[C_CHUNK:]]
            data_b = [r[sl_b, :] for r in (q_ref, k_ref, v_ref)] + [gb_b[:C_CHUNK], gb_b[C_CHUNK:]]
            o0 = _gdn_group_step(*[x[:, :GW] for x in data_f], s0, all_fwd)
            o1 = _gdn_group_step(*[x[:, :GW] for x in data_b], s1, all_bwd)
            mixed = [jnp.concatenate([xf[:, GW:], xb[:, GW:]], axis=1) for xf, xb in zip(data_f, data_b)]
            o2 = _gdn_group_step(*mixed, s2, half)
            o_ref[sl_f, :] += jnp.concatenate([o0, o2[:, :LANE]], axis=1)
            o_ref[sl_b, :] += jnp.concatenate([o1, o2[:, LANE:]], axis=1)
            return carry

        lax.fori_loop(0, n_chunks, step, 0)

    run(qc, kc, vc, gc, bc, oc_ref)
    run(ql, kl, vl, gl, bl, ol_ref)


def _gdn(q, k, v, g, beta, ef, eb, lay):
    b, ctx, seq, nc = lay["b"], lay["ctx"], lay["seq"], lay["nc"]
    cs = lambda w: pl.BlockSpec((ctx, w), lambda i: (i, 0))
    ls = lambda w: pl.BlockSpec((seq, w), lambda i: (nc // seq + i, 0))
    const = lambda shape: pl.BlockSpec(shape, lambda i: (0,) * len(shape))
    widths = (C_W, C_W, C_W, LANE, LANE)
    return pl.pallas_call(
        _gdn_kernel,
        grid=(b,),
        in_specs=[cs(w) for w in widths] + [ls(w) for w in widths] + [const(ef.shape), const(eb.shape)],
        out_specs=[pl.BlockSpec((ctx, C_W), lambda i: (i, 0)), pl.BlockSpec((seq, C_W), lambda i: (i, 0))],
        out_shape=[jax.ShapeDtypeStruct((nc, C_W), f32), jax.ShapeDtypeStruct((b * seq, C_W), f32)],
        scratch_shapes=[pltpu.VMEM((GW, GW), f32)] * 3,
        compiler_params=_cparams(1), name="gated_deltanet",
    )(q, k, v, g, beta, q, k, v, g, beta, ef, eb)


def _outproj_kernel(h_ref, mod_ref, oa_ref, ob_ref, oc_ref, gate_ref, w_ref, nw_ref, bd64_ref, o_ref):
    bd64 = bd64_ref[...]
    nw = nw_ref[...]

    def normed(x, wrow):
        s = _group_sum(x * x, bd64)
        return x * lax.rsqrt(s * (1.0 / C_DK) + EPS) * wrow

    ya = normed(oa_ref[...], nw[0:1, :A_W]).astype(bf16)
    yc = (normed(oc_ref[...], nw[1:2, :C_W]) * _silu(gate_ref[...])).astype(bf16)
    y = jnp.dot(ya, w_ref[:A_W, :], preferred_element_type=f32)
    y += jnp.dot(ob_ref[...], w_ref[A_W:A_W + B_WIDE, :], preferred_element_type=f32)
    y += jnp.dot(yc, w_ref[A_W + B_WIDE:, :], preferred_element_type=f32)
    o_ref[...] = h_ref[...] + mod_ref[0][5:6] * y


def _outproj(h, mod, oa, ob, oc, gate, w_out, nw, bd128, lay, lat_only):
    tm, d = lay["tm"], D_MODEL
    off = lay["nc"] // tm if lat_only else 0
    n_tiles = h.shape[0] // tm - off
    group = functools.partial(_mod_group, lay=lay, off=off)
    rowo = lambda w: pl.BlockSpec((tm, w), lambda i: (i + off, 0))
    rowa = lambda w: pl.BlockSpec((tm, w), lambda i: (i, 0))
    const = lambda shape: pl.BlockSpec(shape, lambda i: (0,) * len(shape))
    return pl.pallas_call(
        _outproj_kernel,
        grid=(n_tiles,),
        in_specs=[rowo(d), pl.BlockSpec((1, 9, d), lambda i: (group(i), 0, 0)),
                  rowa(A_W), rowa(B_WIDE), rowa(C_W), rowo(C_W),
                  pl.BlockSpec((OUT_WIDE, d), lambda i: (0, 0), pipeline_mode=pl.Buffered(1)),
                  const(nw.shape), const(bd128.shape)],
        out_specs=pl.BlockSpec((tm, d), lambda i: (i, 0)),
        out_shape=jax.ShapeDtypeStruct((n_tiles * tm, d), f32),
        compiler_params=_cparams(1), name="mixer_out_proj",
    )(h, mod, oa, ob, oc, gate, w_out, nw, bd128)


def _block_ones(n, group):
    idx = np.arange(n) // group
    return jnp.asarray(idx[:, None] == idx[None, :], dtype=bf16)


def _rope_table(seq, d, pad_rows):
    half, quarter = d // 2, d // 4
    rows = seq // GRID_W
    row = jnp.repeat(jnp.arange(rows, dtype=f32), GRID_W)
    col = jnp.tile(jnp.arange(GRID_W, dtype=f32), rows)
    inv = ROPE_THETA ** (-jnp.arange(0, half, 2, dtype=f32) / half)
    ld = np.arange(LANE) % d
    pos = jnp.where(jnp.asarray(ld < half)[None, :], row[:, None], col[:, None])
    ang = pos * inv[np.asarray((ld % half) % quarter)][None, :]
    sign = jnp.asarray(np.where((ld % half) < quarter, -1.0, 1.0), dtype=f32)[None, :]
    cos = jnp.concatenate([jnp.ones((pad_rows, LANE), f32), jnp.cos(ang)], axis=0)
    sin = jnp.concatenate([jnp.zeros((pad_rows, LANE), f32), jnp.sin(ang) * sign], axis=0)
    return cos, sin


def _arrange_w_in(w):
    offs = np.concatenate([[0], np.cumsum(IN_SIZES)])
    part = lambda n: w[:, offs[n]:offs[n + 1]]
    out = jnp.zeros((w.shape[0], IN_WIDE), bf16)
    put = lambda o, name, x: o.at[:, SEG[name][0]:SEG[name][0] + x.shape[1]].set(x.astype(bf16))
    for n, name in ((0, "qa"), (1, "ka"), (2, "va"), (4, "kb"), (5, "vb"), (6, "cq"), (7, "gate"), (8, "a"), (9, "b")):
        out = put(out, name, part(n))
    qb = part(3)
    for h in range(B_HEADS):
        c0 = SEG["qb"][0] + h * LANE + (h // B_GROUP) * B_DIM
        out = out.at[:, c0:c0 + B_DIM].set(qb[:, h * B_DIM:(h + 1) * B_DIM].astype(bf16))
    return out


def _arrange_w_out(w):
    out = jnp.zeros((OUT_WIDE, w.shape[1]), bf16)
    out = out.at[:A_W].set(w[:A_W].astype(bf16))
    for h in range(B_HEADS):
        r0 = A_W + h * LANE + (h // B_GROUP) * B_DIM
        out = out.at[r0:r0 + B_DIM].set(w[A_W + h * B_DIM:A_W + (h + 1) * B_DIM].astype(bf16))
    return out.at[A_W + B_WIDE:].set(w[A_W + B_HEADS * B_DIM:].astype(bf16))


def _expander(direction):
    m = np.zeros((LANE, C_W), np.float32)
    for h in range(C_HEADS):
        m[direction * C_HEADS + h, h * C_DK:(h + 1) * C_DK] = 1.0
    return jnp.asarray(m, dtype=bf16)


def _pad_lanes(x, width):
    return jnp.pad(x, ((0, 0), (0, width - x.shape[1])))


def kernel(x, c, ctx, c_ctx, w_mod, b_mod, ffn1_w1, ffn1_w2, ffn2_w1, ffn2_w2, w_in, w_out,
           a_qnorm, a_knorm, a_lambda, a_subln, b_qnorm, b_knorm, b_sink,
           c_conv, c_A_log, c_dt_bias, c_onorm):
    b, seq, d = x.shape
    n_ctx = ctx.shape[1]
    nc = b * n_ctx
    tm = 512 if (nc % 512 == 0 and seq % 512 == 0) else 256
    lay = dict(b=b, ctx=n_ctx, seq=seq, nc=nc, tm=tm)
    assert d == D_MODEL and seq % 256 == 0 and n_ctx % 256 == 0 and nc % seq == 0 and seq % GRID_W == 0

    h = jnp.concatenate([ctx.reshape(nc, d), x.reshape(b * seq, d)], axis=0)
    cvec = jnp.zeros((16, d), f32).at[:b].set(c).at[b].set(c_ctx)
    mod_all = _modulation(cvec, w_mod, b_mod).reshape(DEPTH, 16, 9, d)

    bd32, bd64, bd64s = _block_ones(256, A_DIM), _block_ones(256, B_DIM), _block_ones(LANE, C_DK)
    tabs = _rope_table(seq, A_DIM, tm) + _rope_table(seq, B_DIM, tm)
    ef, eb = _expander(0), _expander(1)

    for l in range(DEPTH):
        last = l == DEPTH - 1
        lam_init = 0.8 - 0.6 * float(np.exp(-0.3 * l))
        mod = mod_all[l]
        lf = a_lambda[l].astype(f32)
        lam = (jnp.exp(jnp.sum(lf[0] * lf[1])) - jnp.exp(jnp.sum(lf[2] * lf[3])) + lam_init).reshape(1)
        nw_in = jnp.stack([_pad_lanes(jnp.tile(a_qnorm[l], 8)[None] * (A_DIM ** -0.5), 256)[0],
                           jnp.tile(a_knorm[l], 8),
                           jnp.tile(b_qnorm[l], 4) * (B_DIM ** -0.5),
                           jnp.tile(b_knorm[l], 4)] + [jnp.zeros((256,), f32)] * 4)
        nw_out = jnp.stack([_pad_lanes((jnp.tile(a_subln[l], 4) * (1.0 - lam_init))[None], C_W)[0],
                            jnp.tile(c_onorm[l], C_HEADS)] + [jnp.zeros((C_W,), f32)] * 6)
        gpar = jnp.stack([_pad_lanes(jnp.exp(c_A_log[l].astype(f32)).reshape(1, -1), LANE)[0],
                          _pad_lanes(c_dt_bias[l].astype(f32).reshape(1, -1), LANE)[0]] + [jnp.zeros((LANE,), f32)] * 6)
        cw = jnp.concatenate([c_conv[l], jnp.zeros((8 - C_CONV, 3 * C_W), f32)], axis=0)

        h = _ffn(h, mod, ffn1_w1[l].astype(bf16), ffn1_w2[l].astype(bf16), 0, lay, False)
        qa, ka, va, qb, kb, vb, cq, gate, pa, pb = _inproj(h, mod, _arrange_w_in(w_in[l]), nw_in, tabs, bd32, bd64, lay)
        oa = _attn_a(lam, qa, ka, va, lay, True)
        ob = _attn_b(b_sink[l], qb, kb, vb, lay, True)
        gq, gk, gv, gg, gbeta = _gdn_prep(cq, pa, pb, cw, gpar, bd64s, lay)
        oc_ctx, oc = _gdn(gq, gk, gv, gg, gbeta, ef, eb, lay)
        if not last:
            oa = jnp.concatenate([_attn_a(lam, qa, ka, va, lay, False), oa], axis=0)
            ob = jnp.concatenate([_attn_b(b_sink[l], qb, kb, vb, lay, False), ob], axis=0)
            oc = jnp.concatenate([oc_ctx, oc], axis=0)
        h = _outproj(h, mod, oa, ob, oc, gate, _arrange_w_out(w_out[l]), nw_out, bd64s, lay, last)
        if last:
            lay = dict(lay, nc=0)
        h = _ffn(h, mod, ffn2_w1[l].astype(bf16), ffn2_w2[l].astype(bf16), 6, lay, False)
    return h.reshape(b, seq, d)
```

```python
import functools

import numpy as np
import jax
import jax.numpy as jnp
from jax import lax
from jax.experimental import pallas as pl
from jax.experimental.pallas import tpu as pltpu

f32 = jnp.float32
bf16 = jnp.bfloat16

D_MODEL = 1024
DEPTH = 2
GRID_W = 64
EPS = 1e-6
NEG_INF = -1e30
ROPE_THETA = 10000.0
D_FF = 2816
A_HEADS, A_DIM, A_VDIM = 4, 32, 64
A_W = A_HEADS * A_VDIM
B_HEADS, B_KV_HEADS, B_DIM = 6, 2, 64
B_GROUP = B_HEADS // B_KV_HEADS
B_BLOCK = 128
C_HEADS, C_DK, C_CONV, C_CHUNK = 6, 64, 5, 64
C_W = C_HEADS * C_DK
IN_SIZES = (256, 256, 256, 384, 128, 128, 1152, 384, 12, 12)
LANE = 128
B_WIDE = B_HEADS * LANE
SEG = dict(qa=(0, 256), ka=(256, 256), va=(512, 256), qb=(768, B_WIDE), kb=(1536, 128), vb=(1664, 128),
           cq=(1792, 1152), gate=(2944, 384), a=(3328, 128), b=(3456, 128))
IN_WIDE = 3584
OUT_WIDE = A_W + B_WIDE + C_W
VMEM_LIMIT = 56 * 1024 * 1024


def _cparams(n_axes):
    return pltpu.CompilerParams(dimension_semantics=("arbitrary",) * n_axes, vmem_limit_bytes=VMEM_LIMIT)


def _dot(a, b):
    return jnp.dot(a.astype(bf16), b.astype(bf16), preferred_element_type=f32)


def _dot_nt(a, b):
    return lax.dot_general(a.astype(bf16), b.astype(bf16), (((1,), (1,)), ((), ())), preferred_element_type=f32)


def _dot_tn(a, b):
    return lax.dot_general(a.astype(bf16), b.astype(bf16), (((0,), (0,)), ((), ())), preferred_element_type=f32)


def _split(x, n):
    parts = []
    for _ in range(n - 1):
        p = x.astype(bf16)
        parts.append(p)
        x = x - p.astype(f32)
    parts.append(x.astype(bf16))
    return parts


def _dot_xl(x, m, n=3):
    return sum(jnp.dot(p, m, preferred_element_type=f32) for p in _split(x, n))


def _dot_lx(m, x, n=3):
    return sum(jnp.dot(m, p, preferred_element_type=f32) for p in _split(x, n))


def _group_sum(xx, bd):
    w = bd.shape[0]
    cols = [_dot_xl(xx[:, j:j + w], bd, 2) for j in range(0, xx.shape[1], w)]
    return cols[0] if len(cols) == 1 else jnp.concatenate(cols, axis=1)


def _silu(x):
    return x * jax.nn.sigmoid(x)


def _iota(shape, dim):
    return lax.broadcasted_iota(jnp.int32, shape, dim)


def _modulated_norm(h, shift, scale):
    hn = h * lax.rsqrt(jnp.mean(h * h, axis=-1, keepdims=True) + EPS)
    return hn * (1.0 + scale) + shift


def _mod_kernel(c_ref, w_ref, b_ref, o_ref):
    s = _silu(c_ref[...])
    w = w_ref[0]
    s_hi, s_lo = _split(s, 2)
    w_hi, w_lo = _split(w, 2)
    acc = jnp.dot(s_hi, w_hi, preferred_element_type=f32)
    acc += jnp.dot(s_hi, w_lo, preferred_element_type=f32)
    acc += jnp.dot(s_lo, w_hi, preferred_element_type=f32)
    o_ref[0] = acc + b_ref[0]


def _modulation(cvec, w_mod, b_mod):
    depth, d, n = w_mod.shape
    tn = 1024
    return pl.pallas_call(
        _mod_kernel,
        grid=(depth, n // tn),
        in_specs=[pl.BlockSpec((cvec.shape[0], d), lambda l, j: (0, 0)),
                  pl.BlockSpec((1, d, tn), lambda l, j: (l, 0, j)),
                  pl.BlockSpec((1, 1, tn), lambda l, j: (l, 0, j))],
        out_specs=pl.BlockSpec((1, cvec.shape[0], tn), lambda l, j: (l, 0, j)),
        out_shape=jax.ShapeDtypeStruct((depth, cvec.shape[0], n), f32),
        compiler_params=_cparams(2), name="modulation",
    )(cvec, w_mod, b_mod.reshape(depth, 1, n))


def _ffn_kernel(h_ref, mod_ref, w1_ref, w2_ref, o_ref, *, idx, n_chunks):
    h = h_ref[...]
    mod = mod_ref[0]
    hn = _modulated_norm(h, mod[idx:idx + 1], mod[idx + 1:idx + 2]).astype(bf16)
    ck = D_FF // n_chunks
    acc = None
    for c in range(n_chunks):
        g = jnp.dot(hn, w1_ref[:, c * ck:(c + 1) * ck], preferred_element_type=f32)
        u = jnp.dot(hn, w1_ref[:, D_FF + c * ck:D_FF + (c + 1) * ck], preferred_element_type=f32)
        a = (_silu(g) * u).astype(bf16)
        part = jnp.dot(a, w2_ref[c * ck:(c + 1) * ck, :], preferred_element_type=f32)
        acc = part if acc is None else acc + part
    o_ref[...] = h + (0.5 * mod[idx + 2:idx + 3]) * acc


def _ffn(h, mod, w1, w2, idx, lay, lat_only):
    tm, d = lay["tm"], D_MODEL
    off = lay["nc"] // tm if lat_only else 0
    n_tiles = h.shape[0] // tm - off
    group = functools.partial(_mod_group, lay=lay, off=off)
    return pl.pallas_call(
        functools.partial(_ffn_kernel, idx=idx, n_chunks=2),
        grid=(n_tiles,),
        in_specs=[pl.BlockSpec((tm, d), lambda i: (i + off, 0)),
                  pl.BlockSpec((1, 9, d), lambda i: (group(i), 0, 0)),
                  pl.BlockSpec((d, 2 * D_FF), lambda i: (0, 0), pipeline_mode=pl.Buffered(1)),
                  pl.BlockSpec((D_FF, d), lambda i: (0, 0), pipeline_mode=pl.Buffered(1))],
        out_specs=pl.BlockSpec((tm, d), lambda i: (i, 0)),
        out_shape=jax.ShapeDtypeStruct((n_tiles * tm, d), f32),
        compiler_params=_cparams(1), name=f"ffn_half_step_{idx}",
    )(h, mod, w1, w2)


def _mod_group(i, lay, off):
    r = (i + off) * lay["tm"]
    return jnp.where(r < lay["nc"], lay["b"], (r - lay["nc"]) // lay["seq"])


def _rope(x, cos, sin, quarter):
    w = x.shape[1]
    reps = w // LANE
    if reps > 1:
        cos = jnp.concatenate([cos] * reps, axis=1)
        sin = jnp.concatenate([sin] * reps, axis=1)
    first = (_iota((1, w), 1) % (2 * quarter)) < quarter
    swapped = jnp.where(first, pltpu.roll(x, w - quarter, 1), pltpu.roll(x, quarter, 1))
    return x * cos + swapped * sin


def _inproj_kernel(h_ref, mod_ref, w_ref, nw_ref, ca_ref, sa_ref, cb_ref, sb_ref, bd32_ref, bd64_ref,
                   qa_o, ka_o, va_o, qb_o, kb_o, vb_o, cq_o, gate_o, a_o, b_o):
    mod = mod_ref[0]
    hn = _modulated_norm(h_ref[...], mod[3:4], mod[4:5]).astype(bf16)

    def proj(name):
        off, width = SEG[name]
        return jnp.dot(hn, w_ref[:, off:off + width], preferred_element_type=f32)

    def normed(x, bd, group, wrow):
        s = _group_sum(x * x, bd)
        return x * lax.rsqrt(s * (1.0 / group) + EPS) * wrow

    nw = nw_ref[...]
    bd32, bd64 = bd32_ref[...], bd64_ref[...]
    ca, sa, cb, sb = ca_ref[...], sa_ref[...], cb_ref[...], sb_ref[...]
    qa_o[...] = _rope(normed(proj("qa"), bd32, A_DIM, nw[0:1, :256]), ca, sa, A_DIM // 4).astype(bf16)
    ka_o[...] = _rope(normed(proj("ka"), bd32, A_DIM, nw[1:2, :256]), ca, sa, A_DIM // 4).astype(bf16)
    va_o[...] = proj("va").astype(bf16)
    qb = proj("qb")
    wq = jnp.concatenate([nw[2:3, :256]] * (B_WIDE // 256), axis=1)
    qb_o[...] = _rope(normed(qb, bd64, B_DIM, wq), cb, sb, B_DIM // 4).astype(bf16)
    kb_o[...] = _rope(normed(proj("kb"), bd64[:LANE, :LANE], B_DIM, nw[3:4, :LANE]), cb, sb, B_DIM // 4).astype(bf16)
    vb_o[...] = proj("vb").astype(bf16)
    cq_o[...] = proj("cq")
    gate_o[...] = proj("gate")
    a_o[...] = proj("a")
    b_o[...] = proj("b")


def _inproj(h, mod, w_in, nw, tabs, bd32, bd64, lay):
    tm, d = lay["tm"], D_MODEL
    n = h.shape[0]
    n_tiles = n // tm
    nct = lay["nc"] // tm
    spt = lay["seq"] // tm
    group = functools.partial(_mod_group, lay=lay, off=0)

    def tab_idx(i):
        return jnp.where(i < nct, 0, 1 + (i - nct) % spt)

    row = lambda w: pl.BlockSpec((tm, w), lambda i: (i, 0))
    const = lambda shape: pl.BlockSpec(shape, lambda i: (0,) * len(shape))
    tab = pl.BlockSpec((tm, LANE), lambda i: (tab_idx(i), 0))
    names = ("qa", "ka", "va", "qb", "kb", "vb", "cq", "gate", "a", "b")
    dts = (bf16,) * 6 + (f32,) * 4
    return pl.pallas_call(
        _inproj_kernel,
        grid=(n_tiles,),
        in_specs=[row(d), pl.BlockSpec((1, 9, d), lambda i: (group(i), 0, 0)),
                  pl.BlockSpec((d, IN_WIDE), lambda i: (0, 0), pipeline_mode=pl.Buffered(1)),
                  const(nw.shape), tab, tab, tab, tab, const(bd32.shape), const(bd64.shape)],
        out_specs=[row(SEG[k][1]) for k in names],
        out_shape=[jax.ShapeDtypeStruct((n, SEG[k][1]), dt) for k, dt in zip(names, dts)],
        compiler_params=_cparams(1), name="mixer_in_proj",
    )(h, mod, w_in, nw, *tabs, bd32, bd64)


def _attn_a_kernel(lam_ref, q_ref, *refs, n_seg):
    k_refs, v_refs, o_ref = refs[:n_seg], refs[n_seg:2 * n_seg], refs[2 * n_seg]
    q = q_ref[...]
    lam = lam_ref[0]
    lane = _iota((1, A_W), 1)
    ks = [r[...] for r in k_refs]
    vs = [r[...] for r in v_refs]
    out = jnp.zeros(q.shape, f32)
    for h in range(A_HEADS):
        es, rs = [], []
        for m in range(2):
            qm = jnp.where(lane // A_DIM == 2 * h + m, q, jnp.zeros_like(q))
            s = [_dot_nt(qm, k) for k in ks]
            mx = functools.reduce(jnp.maximum, [jnp.max(x, axis=-1, keepdims=True) for x in s])
            e = [jnp.exp(x - mx) for x in s]
            den = functools.reduce(jnp.add, [jnp.sum(x, axis=-1, keepdims=True) for x in e])
            es.append(e)
            rs.append(1.0 / den)
        r1 = lam * rs[1]
        oh = None
        for e0, e1, v in zip(es[0], es[1], vs):
            w = (e0 * rs[0] - e1 * r1).astype(bf16)
            part = jnp.dot(w, v, preferred_element_type=f32)
            oh = part if oh is None else oh + part
        out = jnp.where(lane // A_VDIM == h, oh, out)
    o_ref[...] = out


def _attn_a(lam, qa, ka, va, lay, latent):
    b, ctx, seq, nc = lay["b"], lay["ctx"], lay["seq"], lay["nc"]
    smem = pl.BlockSpec(memory_space=pltpu.SMEM)
    if latent:
        tq = 256
        qpb = seq // tq
        grid = (b, qpb)
        qspec = pl.BlockSpec((tq, A_W), lambda i, j: (nc // tq + i * qpb + j, 0))
        kv = [pl.BlockSpec((ctx, A_W), lambda i, j: (i, 0)), pl.BlockSpec((seq, A_W), lambda i, j: (nc // seq + i, 0))]
        ospec = pl.BlockSpec((tq, A_W), lambda i, j: (i * qpb + j, 0))
        rows = b * seq
    else:
        grid = (b, 1)
        qspec = pl.BlockSpec((ctx, A_W), lambda i, j: (i, 0))
        kv = [pl.BlockSpec((ctx, A_W), lambda i, j: (i, 0))]
        ospec = qspec
        rows = nc
    n_seg = len(kv)
    return pl.pallas_call(
        functools.partial(_attn_a_kernel, n_seg=n_seg),
        grid=grid,
        in_specs=[smem, qspec] + kv + kv,
        out_specs=ospec,
        out_shape=jax.ShapeDtypeStruct((rows, A_W), f32),
        compiler_params=_cparams(2), name="diff_attention_lat" if latent else "diff_attention_ctx",
    )(lam, qa, *([ka] * n_seg), *([va] * n_seg))


def _attn_b_kernel(sink_ref, q_ref, kc_ref, vc_ref, *refs, latent):
    o_ref = refs[-1]
    q = q_ref[...]
    rows = q.shape[0]
    lane = _iota((1, LANE), 1)
    keys, vals, masks = [kc_ref[...]], [vc_ref[...]], [None]
    if latent:
        kl_ref, vl_ref = refs[0], refs[1]
        t = pl.program_id(1)
        nb = pl.num_programs(1)
        r = _iota((B_GROUP * rows, B_BLOCK), 0) % rows
        c = _iota((B_GROUP * rows, B_BLOCK), 1)
        for off in (-1, 0, 1):
            blk = jnp.clip(t + off, 0, nb - 1)
            start = pl.multiple_of(blk * B_BLOCK, B_BLOCK)
            keys.append(kl_ref[pl.ds(start, B_BLOCK), :])
            vals.append(vl_ref[pl.ds(start, B_BLOCK), :])
            inside = jnp.logical_and(t + off >= 0, t + off < nb)
            if off == -1:
                masks.append(jnp.logical_and(c >= r, inside))
            elif off == 1:
                masks.append(jnp.logical_and(c <= r, inside))
            else:
                masks.append(None)
    for g in range(B_KV_HEADS):
        heads = range(g * B_GROUP, (g + 1) * B_GROUP)
        qg = jnp.concatenate([q[:, h * LANE:(h + 1) * LANE] for h in heads], axis=0)
        sink = jnp.concatenate([jnp.full((rows, 1), sink_ref[h], f32) for h in heads], axis=0)
        s = []
        for k, msk in zip(keys, masks):
            x = _dot_nt(qg, k)
            s.append(x if msk is None else jnp.where(msk, x, NEG_INF))
        mx = functools.reduce(jnp.maximum, [jnp.max(x, axis=-1, keepdims=True) for x in s] + [sink])
        e = [jnp.exp(x - mx) for x in s]
        den = functools.reduce(jnp.add, [jnp.sum(x, axis=-1, keepdims=True) for x in e]) + jnp.exp(sink - mx)
        rden = 1.0 / den
        og = None
        for x, v in zip(e, vals):
            part = jnp.dot((x * rden).astype(bf16), v, preferred_element_type=f32)
            og = part if og is None else og + part
        og = jnp.where(lane // B_DIM == g, og, 0.0).astype(bf16)
        for n, h in enumerate(heads):
            o_ref[:, h * LANE:(h + 1) * LANE] = og[n * rows:(n + 1) * rows]


def _attn_b(sink, qb, kb, vb, lay, latent):
    b, ctx, seq, nc = lay["b"], lay["ctx"], lay["seq"], lay["nc"]
    smem = pl.BlockSpec(memory_space=pltpu.SMEM)
    cspec = pl.BlockSpec((ctx, LANE), lambda i, j: (i, 0))
    if latent:
        nb = seq // B_BLOCK
        grid = (b, nb)
        qspec = pl.BlockSpec((B_BLOCK, B_WIDE), lambda i, j: (nc // B_BLOCK + i * nb + j, 0))
        lspec = pl.BlockSpec((seq, LANE), lambda i, j: (nc // seq + i, 0))
        in_specs = [smem, qspec, cspec, cspec, lspec, lspec]
        args = (sink, qb, kb, vb, kb, vb)
        ospec = pl.BlockSpec((B_BLOCK, B_WIDE), lambda i, j: (i * nb + j, 0))
        rows = b * seq
    else:
        grid = (b, 1)
        qspec = pl.BlockSpec((ctx, B_WIDE), lambda i, j: (i, 0))
        in_specs = [smem, qspec, cspec, cspec]
        args = (sink, qb, kb, vb)
        ospec = qspec
        rows = nc
    return pl.pallas_call(
        functools.partial(_attn_b_kernel, latent=latent),
        grid=grid, in_specs=in_specs, out_specs=ospec,
        out_shape=jax.ShapeDtypeStruct((rows, B_WIDE), bf16),
        compiler_params=_cparams(2), name="window_attention_lat" if latent else "sink_attention_ctx",
    )(*args)


HALO = 8


def _gdn_prep_kernel(x_ref, prev_ref, next_ref, a_ref, b_ref, cw_ref, par_ref, bd64_ref,
                     q_o, k_o, v_o, g_o, beta_o, *, nct, cpt, spt):
    i = pl.program_id(0)
    tm = x_ref.shape[0]
    j = jnp.where(i < nct, i % cpt, (i - nct) % spt)
    per_seq = jnp.where(i < nct, cpt, spt)
    first = j == 0
    last = j == per_seq - 1
    prev = jnp.where(first, 0.0, prev_ref[...])
    nxt = jnp.where(last, 0.0, next_ref[...])
    xx = jnp.concatenate([prev, x_ref[...], nxt], axis=0)
    cw = cw_ref[...]
    rows = xx.shape[0]
    y = None
    for tap in range(C_CONV):
        shift = (C_CONV // 2 - tap) % rows
        sh = xx if shift == 0 else pltpu.roll(xx, shift, 0)
        term = sh[HALO:HALO + tm] * cw[tap:tap + 1]
        y = term if y is None else y + term
    y = _silu(y)
    bd64 = bd64_ref[...]

    def l2n(t):
        return t * lax.rsqrt(_group_sum(t * t, bd64) + EPS)

    q_o[...] = l2n(y[:, :C_W]) * (C_DK ** -0.5)
    k_o[...] = l2n(y[:, C_W:2 * C_W])
    v_o[...] = y[:, 2 * C_W:]
    par = par_ref[...]
    z = a_ref[...] + par[1:2]
    softplus = jnp.maximum(z, 0.0) + jnp.log1p(jnp.exp(-jnp.abs(z)))
    g_o[...] = -par[0:1] * softplus
    beta_o[...] = jax.nn.sigmoid(b_ref[...])


def _gdn_prep(cq, a, bb, cw, par, bd64, lay):
    tm = 256
    n = cq.shape[0]
    n_tiles = n // tm
    nct, spt = lay["nc"] // tm, lay["seq"] // tm
    hb = tm // HALO
    last_blk = n // HALO - 1
    row = lambda w: pl.BlockSpec((tm, w), lambda i: (i, 0))
    const = lambda shape: pl.BlockSpec(shape, lambda i: (0,) * len(shape))
    return pl.pallas_call(
        functools.partial(_gdn_prep_kernel, nct=nct, cpt=lay["ctx"] // tm, spt=spt),
        grid=(n_tiles,),
        in_specs=[row(3 * C_W),
                  pl.BlockSpec((HALO, 3 * C_W), lambda i: (jnp.maximum(i * hb - 1, 0), 0)),
                  pl.BlockSpec((HALO, 3 * C_W), lambda i: (jnp.minimum((i + 1) * hb, last_blk), 0)),
                  row(LANE), row(LANE), const(cw.shape), const(par.shape), const(bd64.shape)],
        out_specs=[row(C_W), row(C_W), row(C_W), row(LANE), row(LANE)],
        out_shape=[jax.ShapeDtypeStruct((n, C_W), f32)] * 3 + [jax.ShapeDtypeStruct((n, LANE), f32)] * 2,
        compiler_params=_cparams(1), name="gdn_inputs",
    )(cq, cq, cq, a, bb, cw, par, bd64)


GW = 4 * C_DK


def _bd(x):
    t = jnp.concatenate([x.astype(bf16)] * 4, axis=0)
    same = _iota((GW, GW), 0) // C_DK == _iota((GW, GW), 1) // C_DK
    return jnp.where(same, t, jnp.zeros_like(t))


N_GROUPS = 3
GDN_W = N_GROUPS * GW
PREP_UNROLL = 4


def _gdn_prepare(groups, fwd_lanes):
    L = C_CHUNK
    n = range(len(groups))
    ri = _iota((L, GW), 0)
    ci = _iota((L, GW), 1) % C_DK
    eye = ci == ri
    ahead = [jnp.where(f, ri - ci, ci - ri) for f in fwd_lanes]
    qs, ks, vs, gcs, betas = zip(*groups)
    ones = jnp.ones((L, L), bf16)
    gr = [_dot_lx(ones, jnp.where(eye, gcs[i], 0.0)) for i in n]
    kq = [_dot_nt(jnp.concatenate([ks[i], qs[i]], axis=0), _bd(ks[i])) for i in n]
    dec = [jnp.exp(jnp.where(ahead[i] >= 0, gcs[i] - gr[i], NEG_INF)) for i in n]
    a = [jnp.where(ahead[i] > 0, betas[i] * kq[i][:L] * dec[i], 0.0) for i in n]
    tm = [jnp.where(eye, 1.0, 0.0) - jnp.where(ri // 2 == ci // 2, a[i], 0.0) for i in n]
    s = 2
    while s < L:
        off = jnp.logical_and(ri // (2 * s) == ci // (2 * s), ri // s != ci // s)
        y = [_dot(jnp.where(off, a[i], 0.0), _bd(tm[i])) for i in n]
        tm = [tm[i] - _dot(tm[i], _bd(y[i])) for i in n]
        s *= 2
    eg = [jnp.exp(gcs[i]) for i in n]
    uw = [_dot(tm[i], jnp.concatenate([_bd(betas[i] * vs[i]), _bd(betas[i] * eg[i] * ks[i])], axis=1)) for i in n]
    out = []
    for i in n:
        g_last = jnp.where(fwd_lanes[i], gcs[i][L - 1:L], gcs[i][0:1])
        wq = jnp.concatenate([uw[i][:, GW:], qs[i] * eg[i]], axis=0).astype(bf16)
        qk = jnp.where(ahead[i] >= 0, kq[i][L:] * dec[i], 0.0).astype(bf16)
        kd = (ks[i] * jnp.exp(g_last - gcs[i])).astype(bf16)
        out.append((uw[i][:, :GW], wq, qk, kd, jnp.exp(g_last)))
    return out


def _gdn_scan(chunks, s_ref):
    L = C_CHUNK
    n = range(len(chunks))
    same = _iota((GW, GW), 0) // C_DK == _iota((GW, GW), 1) // C_DK
    s = [s_ref[i] for i in n]
    ws = [_dot(chunks[i][1], s[i]) for i in n]
    v_new = [chunks[i][0] - ws[i][:L] for i in n]
    o = [ws[i][L:] + _dot(chunks[i][2], _bd(v_new[i])) for i in n]
    upd = [_dot_tn(chunks[i][3], v_new[i]) for i in n]
    for i in n:
        s_ref[i] = s[i] * chunks[i][4] + jnp.where(same, upd[i], 0.0)
    return o


def _gdn_kernel(qc, kc, vc, gc, bc, ql, kl, vl, gl, bl, ef_ref, eb_ref, oc_ref, ol_ref,
                s_ref, u_s, wq_s, qk_s, kd_s, dl_s):
    L = C_CHUNK
    s_ref[...] = jnp.zeros(s_ref.shape, f32)
    lane = _iota((1, GW), 1)
    fwd_lanes = [lane >= 0, lane < 0, lane < LANE]
    ef, eb = ef_ref[...], eb_ref[...]
    r64 = _iota((L, L), 0)
    c64 = _iota((L, L), 1)
    tri_l = (c64 <= r64).astype(bf16)
    tri_u = (c64 >= r64).astype(bf16)

    def run(q_ref, k_ref, v_ref, g_ref, b_ref, o_ref, base):
        n_chunks = q_ref.shape[0] // L

        def prepare(tt, carry):
            groups = []
            for sub in range(PREP_UNROLL):
                t = tt * PREP_UNROLL + sub
                sl_f = pl.ds(pl.multiple_of(t * L, L), L)
                sl_b = pl.ds(pl.multiple_of((n_chunks - 1 - t) * L, L), L)
                gc_f = _dot_lx(tri_l, g_ref[sl_f, :])
                gc_b = _dot_lx(tri_u, g_ref[sl_b, :])
                gb_f = _dot_xl(jnp.concatenate([gc_f, b_ref[sl_f, :]], axis=0), ef)
                gb_b = _dot_xl(jnp.concatenate([gc_b, b_ref[sl_b, :]], axis=0), eb)
                data_f = [r[sl_f, :] for r in (q_ref, k_ref, v_ref)] + [gb_f[:L], gb_f[L:]]
                data_b = [r[sl_b, :] for r in (q_ref, k_ref, v_ref)] + [gb_b[:L], gb_b[L:]]
                groups += [tuple(x[:, :GW] for x in data_f), tuple(x[:, :GW] for x in data_b),
                           tuple(jnp.concatenate([xf[:, GW:], xb[:, GW:]], axis=1) for xf, xb in zip(data_f, data_b))]
            prepared = _gdn_prepare(groups, fwd_lanes * PREP_UNROLL)
            for n, (u, wq, qk, kd, dl) in enumerate(prepared):
                it = base + tt * PREP_UNROLL + n // N_GROUPS
                cols = slice((n % N_GROUPS) * GW, (n % N_GROUPS + 1) * GW)
                u_s[pl.ds(pl.multiple_of(it * L, L), L), cols] = u
                wq_s[pl.ds(pl.multiple_of(it * 2 * L, 2 * L), 2 * L), cols] = wq
                qk_s[pl.ds(pl.multiple_of(it * L, L), L), cols] = qk
                kd_s[pl.ds(pl.multiple_of(it * L, L), L), cols] = kd
                dl_s[pl.ds(pl.multiple_of(it * 8, 8), 8), cols] = jnp.broadcast_to(dl, (8, GW))
            return carry

        lax.fori_loop(0, n_chunks // PREP_UNROLL, prepare, 0)
        o_ref[...] = jnp.zeros(o_ref.shape, f32)

        def scan(t, carry):
            sl_f = pl.ds(pl.multiple_of(t * L, L), L)
            sl_b = pl.ds(pl.multiple_of((n_chunks - 1 - t) * L, L), L)
            it = base + t
            r1 = pl.ds(pl.multiple_of(it * L, L), L)
            r2 = pl.ds(pl.multiple_of(it * 2 * L, 2 * L), 2 * L)
            r8 = pl.ds(pl.multiple_of(it * 8, 8), 1)
            chunks = []
            for i in range(N_GROUPS):
                cols = slice(i * GW, (i + 1) * GW)
                chunks.append((u_s[r1, cols], wq_s[r2, cols], qk_s[r1, cols], kd_s[r1, cols], dl_s[r8, cols]))
            o0, o1, o2 = _gdn_scan(chunks, s_ref)
            o_ref[sl_f, :] += jnp.concatenate([o0, o2[:, :LANE]], axis=1)
            o_ref[sl_b, :] += jnp.concatenate([o1, o2[:, LANE:]], axis=1)
            return carry

        lax.fori_loop(0, n_chunks, scan, 0)

    run(qc, kc, vc, gc, bc, oc_ref, 0)
    run(ql, kl, vl, gl, bl, ol_ref, qc.shape[0] // L)


def _gdn(q, k, v, g, beta, ef, eb, lay):
    b, ctx, seq, nc = lay["b"], lay["ctx"], lay["seq"], lay["nc"]
    cs = lambda w: pl.BlockSpec((ctx, w), lambda i: (i, 0))
    ls = lambda w: pl.BlockSpec((seq, w), lambda i: (nc // seq + i, 0), pipeline_mode=pl.Buffered(1))
    const = lambda shape: pl.BlockSpec(shape, lambda i: (0,) * len(shape))
    widths = (C_W, C_W, C_W, LANE, LANE)
    n_it = (ctx + seq) // C_CHUNK
    return pl.pallas_call(
        _gdn_kernel,
        grid=(b,),
        in_specs=[cs(w) for w in widths] + [ls(w) for w in widths] + [const(ef.shape), const(eb.shape)],
        out_specs=[pl.BlockSpec((ctx, C_W), lambda i: (i, 0)), pl.BlockSpec((seq, C_W), lambda i: (i, 0))],
        out_shape=[jax.ShapeDtypeStruct((nc, C_W), f32), jax.ShapeDtypeStruct((b * seq, C_W), f32)],
        scratch_shapes=[pltpu.VMEM((N_GROUPS, GW, GW), f32),
                        pltpu.VMEM((n_it * C_CHUNK, GDN_W), f32),
                        pltpu.VMEM((n_it * 2 * C_CHUNK, GDN_W), bf16),
                        pltpu.VMEM((n_it * C_CHUNK, GDN_W), bf16),
                        pltpu.VMEM((n_it * C_CHUNK, GDN_W), bf16),
                        pltpu.VMEM((n_it * 8, GDN_W), f32)],
        compiler_params=_cparams(1), name="gated_deltanet",
    )(q, k, v, g, beta, q, k, v, g, beta, ef, eb)


def _outproj_kernel(h_ref, mod_ref, oa_ref, ob_ref, oc_ref, gate_ref, w_ref, nw_ref, bd64_ref, o_ref):
    bd64 = bd64_ref[...]
    nw = nw_ref[...]

    def normed(x, wrow):
        s = _group_sum(x * x, bd64)
        return x * lax.rsqrt(s * (1.0 / C_DK) + EPS) * wrow

    ya = normed(oa_ref[...], nw[0:1, :A_W]).astype(bf16)
    yc = (normed(oc_ref[...], nw[1:2, :C_W]) * _silu(gate_ref[...])).astype(bf16)
    y = jnp.dot(ya, w_ref[:A_W, :], preferred_element_type=f32)
    y += jnp.dot(ob_ref[...], w_ref[A_W:A_W + B_WIDE, :], preferred_element_type=f32)
    y += jnp.dot(yc, w_ref[A_W + B_WIDE:, :], preferred_element_type=f32)
    o_ref[...] = h_ref[...] + mod_ref[0][5:6] * y


def _outproj(h, mod, oa, ob, oc, gate, w_out, nw, bd128, lay, lat_only):
    tm, d = lay["tm"], D_MODEL
    off = lay["nc"] // tm if lat_only else 0
    n_tiles = h.shape[0] // tm - off
    group = functools.partial(_mod_group, lay=lay, off=off)
    rowo = lambda w: pl.BlockSpec((tm, w), lambda i: (i + off, 0))
    rowa = lambda w: pl.BlockSpec((tm, w), lambda i: (i, 0))
    const = lambda shape: pl.BlockSpec(shape, lambda i: (0,) * len(shape))
    return pl.pallas_call(
        _outproj_kernel,
        grid=(n_tiles,),
        in_specs=[rowo(d), pl.BlockSpec((1, 9, d), lambda i: (group(i), 0, 0)),
                  rowa(A_W), rowa(B_WIDE), rowa(C_W), rowo(C_W),
                  pl.BlockSpec((OUT_WIDE, d), lambda i: (0, 0), pipeline_mode=pl.Buffered(1)),
                  const(nw.shape), const(bd128.shape)],
        out_specs=pl.BlockSpec((tm, d), lambda i: (i, 0)),
        out_shape=jax.ShapeDtypeStruct((n_tiles * tm, d), f32),
        compiler_params=_cparams(1), name="mixer_out_proj",
    )(h, mod, oa, ob, oc, gate, w_out, nw, bd128)


def _block_ones(n, group):
    idx = np.arange(n) // group
    return jnp.asarray(idx[:, None] == idx[None, :], dtype=bf16)


def _rope_table(seq, d, pad_rows):
    half, quarter = d // 2, d // 4
    rows = seq // GRID_W
    row = jnp.repeat(jnp.arange(rows, dtype=f32), GRID_W)
    col = jnp.tile(jnp.arange(GRID_W, dtype=f32), rows)
    inv = ROPE_THETA ** (-jnp.arange(0, half, 2, dtype=f32) / half)
    ld = np.arange(LANE) % d
    pos = jnp.where(jnp.asarray(ld < half)[None, :], row[:, None], col[:, None])
    ang = pos * inv[np.asarray((ld % half) % quarter)][None, :]
    sign = jnp.asarray(np.where((ld % half) < quarter, -1.0, 1.0), dtype=f32)[None, :]
    cos = jnp.concatenate([jnp.ones((pad_rows, LANE), f32), jnp.cos(ang)], axis=0)
    sin = jnp.concatenate([jnp.zeros((pad_rows, LANE), f32), jnp.sin(ang) * sign], axis=0)
    return cos, sin


def _arrange_w_in(w):
    offs = np.concatenate([[0], np.cumsum(IN_SIZES)])
    part = lambda n: w[:, offs[n]:offs[n + 1]]
    out = jnp.zeros((w.shape[0], IN_WIDE), bf16)
    put = lambda o, name, x: o.at[:, SEG[name][0]:SEG[name][0] + x.shape[1]].set(x.astype(bf16))
    for n, name in ((0, "qa"), (1, "ka"), (2, "va"), (4, "kb"), (5, "vb"), (6, "cq"), (7, "gate"), (8, "a"), (9, "b")):
        out = put(out, name, part(n))
    qb = part(3)
    for h in range(B_HEADS):
        c0 = SEG["qb"][0] + h * LANE + (h // B_GROUP) * B_DIM
        out = out.at[:, c0:c0 + B_DIM].set(qb[:, h * B_DIM:(h + 1) * B_DIM].astype(bf16))
    return out


def _arrange_w_out(w):
    out = jnp.zeros((OUT_WIDE, w.shape[1]), bf16)
    out = out.at[:A_W].set(w[:A_W].astype(bf16))
    for h in range(B_HEADS):
        r0 = A_W + h * LANE + (h // B_GROUP) * B_DIM
        out = out.at[r0:r0 + B_DIM].set(w[A_W + h * B_DIM:A_W + (h + 1) * B_DIM].astype(bf16))
    return out.at[A_W + B_WIDE:].set(w[A_W + B_HEADS * B_DIM:].astype(bf16))


def _expander(direction):
    m = np.zeros((LANE, C_W), np.float32)
    for h in range(C_HEADS):
        m[direction * C_HEADS + h, h * C_DK:(h + 1) * C_DK] = 1.0
    return jnp.asarray(m, dtype=bf16)


def _pad_lanes(x, width):
    return jnp.pad(x, ((0, 0), (0, width - x.shape[1])))


def kernel(x, c, ctx, c_ctx, w_mod, b_mod, ffn1_w1, ffn1_w2, ffn2_w1, ffn2_w2, w_in, w_out,
           a_qnorm, a_knorm, a_lambda, a_subln, b_qnorm, b_knorm, b_sink,
           c_conv, c_A_log, c_dt_bias, c_onorm):
    b, seq, d = x.shape
    n_ctx = ctx.shape[1]
    nc = b * n_ctx
    tm = 512 if (nc % 512 == 0 and seq % 512 == 0) else 256
    lay = dict(b=b, ctx=n_ctx, seq=seq, nc=nc, tm=tm)
    assert d == D_MODEL and seq % 256 == 0 and n_ctx % 256 == 0 and nc % seq == 0 and seq % GRID_W == 0

    h = jnp.concatenate([ctx.reshape(nc, d), x.reshape(b * seq, d)], axis=0)
    cvec = jnp.zeros((16, d), f32).at[:b].set(c).at[b].set(c_ctx)
    mod_all = _modulation(cvec, w_mod, b_mod).reshape(DEPTH, 16, 9, d)

    bd32, bd64, bd64s = _block_ones(256, A_DIM), _block_ones(256, B_DIM), _block_ones(LANE, C_DK)
    tabs = _rope_table(seq, A_DIM, tm) + _rope_table(seq, B_DIM, tm)
    ef, eb = _expander(0), _expander(1)

    for l in range(DEPTH):
        last = l == DEPTH - 1
        lam_init = 0.8 - 0.6 * float(np.exp(-0.3 * l))
        mod = mod_all[l]
        lf = a_lambda[l].astype(f32)
        lam = (jnp.exp(jnp.sum(lf[0] * lf[1])) - jnp.exp(jnp.sum(lf[2] * lf[3])) + lam_init).reshape(1)
        nw_in = jnp.stack([_pad_lanes(jnp.tile(a_qnorm[l], 8)[None] * (A_DIM ** -0.5), 256)[0],
                           jnp.tile(a_knorm[l], 8),
                           jnp.tile(b_qnorm[l], 4) * (B_DIM ** -0.5),
                           jnp.tile(b_knorm[l], 4)] + [jnp.zeros((256,), f32)] * 4)
        nw_out = jnp.stack([_pad_lanes((jnp.tile(a_subln[l], 4) * (1.0 - lam_init))[None], C_W)[0],
                            jnp.tile(c_onorm[l], C_HEADS)] + [jnp.zeros((C_W,), f32)] * 6)
        gpar = jnp.stack([_pad_lanes(jnp.exp(c_A_log[l].astype(f32)).reshape(1, -1), LANE)[0],
                          _pad_lanes(c_dt_bias[l].astype(f32).reshape(1, -1), LANE)[0]] + [jnp.zeros((LANE,), f32)] * 6)
        cw = jnp.concatenate([c_conv[l], jnp.zeros((8 - C_CONV, 3 * C_W), f32)], axis=0)

        h = _ffn(h, mod, ffn1_w1[l].astype(bf16), ffn1_w2[l].astype(bf16), 0, lay, False)
        qa, ka, va, qb, kb, vb, cq, gate, pa, pb = _inproj(h, mod, _arrange_w_in(w_in[l]), nw_in, tabs, bd32, bd64, lay)
        oa = _attn_a(lam, qa, ka, va, lay, True)
        ob = _attn_b(b_sink[l], qb, kb, vb, lay, True)
        gq, gk, gv, gg, gbeta = _gdn_prep(cq, pa, pb, cw, gpar, bd64s, lay)
        oc_ctx, oc = _gdn(gq, gk, gv, gg, gbeta, ef, eb, lay)
        if not last:
            oa = jnp.concatenate([_attn_a(lam, qa, ka, va, lay, False), oa], axis=0)
            ob = jnp.concatenate([_attn_b(b_sink[l], qb, kb, vb, lay, False), ob], axis=0)
            oc = jnp.concatenate([oc_ctx, oc], axis=0)
        h = _outproj(h, mod, oa, ob, oc, gate, _arrange_w_out(w_out[l]), nw_out, bd64s, lay, last)
        if last:
            lay = dict(lay, nc=0)
        h = _ffn(h, mod, ffn2_w1[l].astype(bf16), ffn2_w2[l].astype(bf16), 6, lay, False)
    return h.reshape(b, seq, d)
```

```python
import functools

import numpy as np
import jax
import jax.numpy as jnp
from jax import lax
from jax.experimental import pallas as pl
from jax.experimental.pallas import tpu as pltpu

f32 = jnp.float32
bf16 = jnp.bfloat16

D_MODEL = 1024
DEPTH = 2
GRID_W = 64
EPS = 1e-6
NEG_INF = -1e30
LOG2E = 1.4426950408889634
ROPE_THETA = 10000.0
D_FF = 2816
A_HEADS, A_DIM, A_VDIM = 4, 32, 64
A_W = A_HEADS * A_VDIM
B_HEADS, B_KV_HEADS, B_DIM = 6, 2, 64
B_GROUP = B_HEADS // B_KV_HEADS
B_BLOCK = 128
C_HEADS, C_DK, C_CONV, C_CHUNK = 6, 64, 5, 64
C_W = C_HEADS * C_DK
IN_SIZES = (256, 256, 256, 384, 128, 128, 1152, 384, 12, 12)
LANE = 128
B_WIDE = B_HEADS * LANE
SEG = dict(qa=(0, 256), ka=(256, 256), va=(512, 256), qb=(768, B_WIDE), kb=(1536, 128), vb=(1664, 128),
           cq=(1792, 1152), gate=(2944, 384), a=(3328, 128), b=(3456, 128))
IN_WIDE = 3584
OUT_WIDE = A_W + B_WIDE + C_W
VMEM_LIMIT = 56 * 1024 * 1024


def _cparams(n_axes):
    return pltpu.CompilerParams(dimension_semantics=("arbitrary",) * n_axes, vmem_limit_bytes=VMEM_LIMIT)


def _dot(a, b):
    return jnp.dot(a.astype(bf16), b.astype(bf16), preferred_element_type=f32)


def _dot_nt(a, b):
    return lax.dot_general(a.astype(bf16), b.astype(bf16), (((1,), (1,)), ((), ())), preferred_element_type=f32)


def _dot_tn(a, b):
    return lax.dot_general(a.astype(bf16), b.astype(bf16), (((0,), (0,)), ((), ())), preferred_element_type=f32)


def _split(x, n):
    parts = []
    for _ in range(n - 1):
        p = x.astype(bf16)
        parts.append(p)
        x = x - p.astype(f32)
    parts.append(x.astype(bf16))
    return parts


def _dot_xl(x, m, n=3):
    return sum(jnp.dot(p, m, preferred_element_type=f32) for p in _split(x, n))


def _dot_lx(m, x, n=3):
    return sum(jnp.dot(m, p, preferred_element_type=f32) for p in _split(x, n))


def _group_sum(xx, bd, terms=2):
    w = bd.shape[0]
    cols = [_dot_xl(xx[:, j:j + w], bd, terms) for j in range(0, xx.shape[1], w)]
    return cols[0] if len(cols) == 1 else jnp.concatenate(cols, axis=1)


def _silu(x):
    return x * jax.nn.sigmoid(x)


def _iota(shape, dim):
    return lax.broadcasted_iota(jnp.int32, shape, dim)


def _modulated_norm(h, shift, scale):
    hn = h * lax.rsqrt(jnp.mean(h * h, axis=-1, keepdims=True) + EPS)
    return hn * (1.0 + scale) + shift


def _mod_kernel(c_ref, w_ref, b_ref, o_ref):
    s = _silu(c_ref[...])
    w = w_ref[0]
    s_hi, s_lo = _split(s, 2)
    w_hi, w_lo = _split(w, 2)
    acc = jnp.dot(s_hi, w_hi, preferred_element_type=f32)
    acc += jnp.dot(s_hi, w_lo, preferred_element_type=f32)
    acc += jnp.dot(s_lo, w_hi, preferred_element_type=f32)
    o_ref[0] = acc + b_ref[0]


def _modulation(cvec, w_mod, b_mod):
    depth, d, n = w_mod.shape
    tn = 1024
    return pl.pallas_call(
        _mod_kernel,
        grid=(depth, n // tn),
        in_specs=[pl.BlockSpec((cvec.shape[0], d), lambda l, j: (0, 0)),
                  pl.BlockSpec((1, d, tn), lambda l, j: (l, 0, j)),
                  pl.BlockSpec((1, 1, tn), lambda l, j: (l, 0, j))],
        out_specs=pl.BlockSpec((1, cvec.shape[0], tn), lambda l, j: (l, 0, j)),
        out_shape=jax.ShapeDtypeStruct((depth, cvec.shape[0], n), f32),
        compiler_params=_cparams(2), name="modulation",
    )(cvec, w_mod, b_mod.reshape(depth, 1, n))


def _ffn_kernel(h_ref, mod_ref, w1_ref, w2_ref, o_ref, *, idx, n_chunks):
    h = h_ref[...]
    mod = mod_ref[0]
    hn = _modulated_norm(h, mod[idx:idx + 1], mod[idx + 1:idx + 2]).astype(bf16)
    ck = D_FF // n_chunks
    acc = None
    for c in range(n_chunks):
        g = jnp.dot(hn, w1_ref[:, c * ck:(c + 1) * ck], preferred_element_type=f32)
        u = jnp.dot(hn, w1_ref[:, D_FF + c * ck:D_FF + (c + 1) * ck], preferred_element_type=f32)
        a = (_silu(g) * u).astype(bf16)
        part = jnp.dot(a, w2_ref[c * ck:(c + 1) * ck, :], preferred_element_type=f32)
        acc = part if acc is None else acc + part
    o_ref[...] = h + (0.5 * mod[idx + 2:idx + 3]) * acc


def _ffn(h, mod, w1, w2, idx, lay, lat_only):
    tm, d = lay["tm"], D_MODEL
    off = lay["nc"] // tm if lat_only else 0
    n_tiles = h.shape[0] // tm - off
    group = functools.partial(_mod_group, lay=lay, off=off)
    return pl.pallas_call(
        functools.partial(_ffn_kernel, idx=idx, n_chunks=2),
        grid=(n_tiles,),
        in_specs=[pl.BlockSpec((tm, d), lambda i: (i + off, 0)),
                  pl.BlockSpec((1, 9, d), lambda i: (group(i), 0, 0)),
                  pl.BlockSpec((d, 2 * D_FF), lambda i: (0, 0), pipeline_mode=pl.Buffered(1)),
                  pl.BlockSpec((D_FF, d), lambda i: (0, 0), pipeline_mode=pl.Buffered(1))],
        out_specs=pl.BlockSpec((tm, d), lambda i: (i, 0)),
        out_shape=jax.ShapeDtypeStruct((n_tiles * tm, d), f32),
        compiler_params=_cparams(1), name=f"ffn_half_step_{idx}",
    )(h, mod, w1, w2)


def _mod_group(i, lay, off):
    r = (i + off) * lay["tm"]
    return jnp.where(r < lay["nc"], lay["b"], (r - lay["nc"]) // lay["seq"])


def _rope(x, cos, sin, quarter):
    w = x.shape[1]
    reps = w // LANE
    if reps > 1:
        cos = jnp.concatenate([cos] * reps, axis=1)
        sin = jnp.concatenate([sin] * reps, axis=1)
    first = (_iota((1, w), 1) % (2 * quarter)) < quarter
    swapped = jnp.where(first, pltpu.roll(x, w - quarter, 1), pltpu.roll(x, quarter, 1))
    return x * cos + swapped * sin


def _inproj_kernel(h_ref, mod_ref, w_ref, nw_ref, ca_ref, sa_ref, cb_ref, sb_ref, bd32_ref, bd64_ref,
                   qa_o, ka_o, va_o, qb_o, kb_o, vb_o, cq_o, gate_o, a_o, b_o):
    mod = mod_ref[0]
    hn = _modulated_norm(h_ref[...], mod[3:4], mod[4:5]).astype(bf16)

    def proj(name):
        off, width = SEG[name]
        return jnp.dot(hn, w_ref[:, off:off + width], preferred_element_type=f32)

    def normed(x, bd, group, wrow):
        s = _group_sum(x * x, bd, terms=1)
        return x * lax.rsqrt(s * (1.0 / group) + EPS) * wrow

    nw = nw_ref[...]
    bd32, bd64 = bd32_ref[...], bd64_ref[...]
    ca, sa, cb, sb = ca_ref[...], sa_ref[...], cb_ref[...], sb_ref[...]
    qa_o[...] = _rope(normed(proj("qa"), bd32, A_DIM, nw[0:1, :256]), ca, sa, A_DIM // 4).astype(bf16)
    ka_o[...] = _rope(normed(proj("ka"), bd32, A_DIM, nw[1:2, :256]), ca, sa, A_DIM // 4).astype(bf16)
    va_o[...] = proj("va").astype(bf16)
    qb = proj("qb")
    wq = jnp.concatenate([nw[2:3, :256]] * (B_WIDE // 256), axis=1)
    qb_o[...] = _rope(normed(qb, bd64, B_DIM, wq), cb, sb, B_DIM // 4).astype(bf16)
    kb_o[...] = _rope(normed(proj("kb"), bd64[:LANE, :LANE], B_DIM, nw[3:4, :LANE]), cb, sb, B_DIM // 4).astype(bf16)
    vb_o[...] = proj("vb").astype(bf16)
    cq_o[...] = proj("cq")
    gate_o[...] = proj("gate")
    a_o[...] = proj("a")
    b_o[...] = proj("b")


def _inproj(h, mod, w_in, nw, tabs, bd32, bd64, lay):
    tm, d = lay["tm"], D_MODEL
    n = h.shape[0]
    n_tiles = n // tm
    nct = lay["nc"] // tm
    spt = lay["seq"] // tm
    group = functools.partial(_mod_group, lay=lay, off=0)

    def tab_idx(i):
        return jnp.where(i < nct, 0, 1 + (i - nct) % spt)

    row = lambda w: pl.BlockSpec((tm, w), lambda i: (i, 0))
    const = lambda shape: pl.BlockSpec(shape, lambda i: (0,) * len(shape))
    tab = pl.BlockSpec((tm, LANE), lambda i: (tab_idx(i), 0))
    names = ("qa", "ka", "va", "qb", "kb", "vb", "cq", "gate", "a", "b")
    dts = (bf16,) * 6 + (f32,) * 4
    return pl.pallas_call(
        _inproj_kernel,
        grid=(n_tiles,),
        in_specs=[row(d), pl.BlockSpec((1, 9, d), lambda i: (group(i), 0, 0)),
                  pl.BlockSpec((d, IN_WIDE), lambda i: (0, 0), pipeline_mode=pl.Buffered(1)),
                  const(nw.shape), tab, tab, tab, tab, const(bd32.shape), const(bd64.shape)],
        out_specs=[row(SEG[k][1]) for k in names],
        out_shape=[jax.ShapeDtypeStruct((n, SEG[k][1]), dt) for k, dt in zip(names, dts)],
        compiler_params=_cparams(1), name="mixer_in_proj",
    )(h, mod, w_in, nw, *tabs, bd32, bd64)


def _attn_a_kernel(lam_ref, q_ref, *refs, n_seg):
    k_refs, v_refs, o_ref = refs[:n_seg], refs[n_seg:2 * n_seg], refs[2 * n_seg]
    q = q_ref[...]
    lam = lam_ref[0]
    lane = _iota((1, A_W), 1)
    ks = [r[...] for r in k_refs]
    vs = [r[...] for r in v_refs]
    rows = q.shape[0]
    out = jnp.zeros(q.shape, f32)
    for h in range(A_HEADS):
        es, rs = [], []
        for m in range(2):
            qm = jnp.where(lane // A_DIM == 2 * h + m, q, jnp.zeros_like(q))
            s = [_dot_nt(qm, k) for k in ks]
            mx = functools.reduce(jnp.maximum, [jnp.max(x, axis=-1, keepdims=True) for x in s])
            e = [jnp.exp2(x - mx) for x in s]
            den = functools.reduce(jnp.add, [jnp.sum(x, axis=-1, keepdims=True) for x in e])
            es.append([x.astype(bf16) for x in e])
            rs.append(1.0 / den)
        ov = None
        for e0, e1, v in zip(es[0], es[1], vs):
            part = jnp.dot(jnp.concatenate([e0, e1], axis=0), v, preferred_element_type=f32)
            ov = part if ov is None else ov + part
        oh = ov[:rows] * rs[0] - ov[rows:] * (lam * rs[1])
        out = jnp.where(lane // A_VDIM == h, oh, out)
    o_ref[...] = out


def _attn_a(lam, qa, ka, va, lay, latent):
    b, ctx, seq, nc = lay["b"], lay["ctx"], lay["seq"], lay["nc"]
    smem = pl.BlockSpec(memory_space=pltpu.SMEM)
    if latent:
        tq = 256
        qpb = seq // tq
        grid = (b, qpb)
        qspec = pl.BlockSpec((tq, A_W), lambda i, j: (nc // tq + i * qpb + j, 0))
        kv = [pl.BlockSpec((ctx, A_W), lambda i, j: (i, 0)), pl.BlockSpec((seq, A_W), lambda i, j: (nc // seq + i, 0))]
        ospec = pl.BlockSpec((tq, A_W), lambda i, j: (i * qpb + j, 0))
        rows = b * seq
    else:
        grid = (b, 1)
        qspec = pl.BlockSpec((ctx, A_W), lambda i, j: (i, 0))
        kv = [pl.BlockSpec((ctx, A_W), lambda i, j: (i, 0))]
        ospec = qspec
        rows = nc
    n_seg = len(kv)
    return pl.pallas_call(
        functools.partial(_attn_a_kernel, n_seg=n_seg),
        grid=grid,
        in_specs=[smem, qspec] + kv + kv,
        out_specs=ospec,
        out_shape=jax.ShapeDtypeStruct((rows, A_W), f32),
        compiler_params=_cparams(2), name="diff_attention_lat" if latent else "diff_attention_ctx",
    )(lam, qa, *([ka] * n_seg), *([va] * n_seg))


def _attn_b_kernel(sink_ref, q_ref, kc_ref, vc_ref, *refs, latent):
    o_ref = refs[-1]
    q = q_ref[...]
    rows = q.shape[0]
    lane = _iota((1, LANE), 1)
    keys, vals, masks = [kc_ref[...]], [vc_ref[...]], [None]
    if latent:
        kl_ref, vl_ref = refs[0], refs[1]
        t = pl.program_id(1)
        nb = pl.num_programs(1)
        r = _iota((B_GROUP * rows, B_BLOCK), 0) % rows
        c = _iota((B_GROUP * rows, B_BLOCK), 1)
        for off in (-1, 0, 1):
            blk = jnp.clip(t + off, 0, nb - 1)
            start = pl.multiple_of(blk * B_BLOCK, B_BLOCK)
            keys.append(kl_ref[pl.ds(start, B_BLOCK), :])
            vals.append(vl_ref[pl.ds(start, B_BLOCK), :])
            inside = jnp.logical_and(t + off >= 0, t + off < nb)
            if off == -1:
                masks.append(jnp.logical_and(c >= r, inside))
            elif off == 1:
                masks.append(jnp.logical_and(c <= r, inside))
            else:
                masks.append(None)
    for g in range(B_KV_HEADS):
        heads = range(g * B_GROUP, (g + 1) * B_GROUP)
        qg = jnp.concatenate([q[:, h * LANE:(h + 1) * LANE] for h in heads], axis=0)
        sink = jnp.concatenate([jnp.full((rows, 1), sink_ref[h], f32) for h in heads], axis=0)
        s = []
        for k, msk in zip(keys, masks):
            x = _dot_nt(qg, k)
            s.append(x if msk is None else jnp.where(msk, x, NEG_INF))
        mx = functools.reduce(jnp.maximum, [jnp.max(x, axis=-1, keepdims=True) for x in s] + [sink])
        e = [jnp.exp2(x - mx) for x in s]
        den = functools.reduce(jnp.add, [jnp.sum(x, axis=-1, keepdims=True) for x in e]) + jnp.exp2(sink - mx)
        rden = 1.0 / den
        og = None
        for x, v in zip(e, vals):
            part = jnp.dot((x * rden).astype(bf16), v, preferred_element_type=f32)
            og = part if og is None else og + part
        og = jnp.where(lane // B_DIM == g, og, 0.0).astype(bf16)
        for n, h in enumerate(heads):
            o_ref[:, h * LANE:(h + 1) * LANE] = og[n * rows:(n + 1) * rows]


def _attn_b(sink, qb, kb, vb, lay, latent):
    b, ctx, seq, nc = lay["b"], lay["ctx"], lay["seq"], lay["nc"]
    smem = pl.BlockSpec(memory_space=pltpu.SMEM)
    cspec = pl.BlockSpec((ctx, LANE), lambda i, j: (i, 0))
    if latent:
        nb = seq // B_BLOCK
        grid = (b, nb)
        qspec = pl.BlockSpec((B_BLOCK, B_WIDE), lambda i, j: (nc // B_BLOCK + i * nb + j, 0))
        lspec = pl.BlockSpec((seq, LANE), lambda i, j: (nc // seq + i, 0))
        in_specs = [smem, qspec, cspec, cspec, lspec, lspec]
        args = (sink, qb, kb, vb, kb, vb)
        ospec = pl.BlockSpec((B_BLOCK, B_WIDE), lambda i, j: (i * nb + j, 0))
        rows = b * seq
    else:
        grid = (b, 1)
        qspec = pl.BlockSpec((ctx, B_WIDE), lambda i, j: (i, 0))
        in_specs = [smem, qspec, cspec, cspec]
        args = (sink, qb, kb, vb)
        ospec = qspec
        rows = nc
    return pl.pallas_call(
        functools.partial(_attn_b_kernel, latent=latent),
        grid=grid, in_specs=in_specs, out_specs=ospec,
        out_shape=jax.ShapeDtypeStruct((rows, B_WIDE), bf16),
        compiler_params=_cparams(2), name="window_attention_lat" if latent else "sink_attention_ctx",
    )(*args)


HALO = 8


def _gdn_prep_kernel(x_ref, prev_ref, next_ref, a_ref, b_ref, cw_ref, par_ref, bd64_ref,
                     q_o, k_o, v_o, g_o, beta_o, *, nct, cpt, spt):
    i = pl.program_id(0)
    tm = x_ref.shape[0]
    j = jnp.where(i < nct, i % cpt, (i - nct) % spt)
    per_seq = jnp.where(i < nct, cpt, spt)
    first = j == 0
    last = j == per_seq - 1
    prev = jnp.where(first, 0.0, prev_ref[...])
    nxt = jnp.where(last, 0.0, next_ref[...])
    xx = jnp.concatenate([prev, x_ref[...], nxt], axis=0)
    cw = cw_ref[...]
    rows = xx.shape[0]
    y = None
    for tap in range(C_CONV):
        shift = (C_CONV // 2 - tap) % rows
        sh = xx if shift == 0 else pltpu.roll(xx, shift, 0)
        term = sh[HALO:HALO + tm] * cw[tap:tap + 1]
        y = term if y is None else y + term
    y = _silu(y)
    bd64 = bd64_ref[...]

    def l2n(t):
        return t * lax.rsqrt(_group_sum(t * t, bd64) + EPS)

    q_o[...] = l2n(y[:, :C_W]) * (C_DK ** -0.5)
    k_o[...] = l2n(y[:, C_W:2 * C_W])
    v_o[...] = y[:, 2 * C_W:]
    par = par_ref[...]
    z = a_ref[...] + par[1:2]
    softplus = jnp.maximum(z, 0.0) + jnp.log1p(jnp.exp(-jnp.abs(z)))
    g_o[...] = -par[0:1] * softplus
    beta_o[...] = jax.nn.sigmoid(b_ref[...])


def _gdn_prep(cq, a, bb, cw, par, bd64, lay):
    tm = 256
    n = cq.shape[0]
    n_tiles = n // tm
    nct, spt = lay["nc"] // tm, lay["seq"] // tm
    hb = tm // HALO
    last_blk = n // HALO - 1
    row = lambda w: pl.BlockSpec((tm, w), lambda i: (i, 0))
    const = lambda shape: pl.BlockSpec(shape, lambda i: (0,) * len(shape))
    return pl.pallas_call(
        functools.partial(_gdn_prep_kernel, nct=nct, cpt=lay["ctx"] // tm, spt=spt),
        grid=(n_tiles,),
        in_specs=[row(3 * C_W),
                  pl.BlockSpec((HALO, 3 * C_W), lambda i: (jnp.maximum(i * hb - 1, 0), 0)),
                  pl.BlockSpec((HALO, 3 * C_W), lambda i: (jnp.minimum((i + 1) * hb, last_blk), 0)),
                  row(LANE), row(LANE), const(cw.shape), const(par.shape), const(bd64.shape)],
        out_specs=[row(C_W), row(C_W), row(C_W), row(LANE), row(LANE)],
        out_shape=[jax.ShapeDtypeStruct((n, C_W), f32)] * 3 + [jax.ShapeDtypeStruct((n, LANE), f32)] * 2,
        compiler_params=_cparams(1), name="gdn_inputs",
    )(cq, cq, cq, a, bb, cw, par, bd64)


GW = 4 * C_DK


def _bd(x):
    t = jnp.concatenate([x.astype(bf16)] * 4, axis=0)
    same = _iota((GW, GW), 0) // C_DK == _iota((GW, GW), 1) // C_DK
    return jnp.where(same, t, jnp.zeros_like(t))


N_GROUPS = 3
GDN_W = N_GROUPS * GW
PREP_UNROLL = 4


def _gdn_prepare(groups, fwd_lanes):
    L = C_CHUNK
    n = range(len(groups))
    ri = _iota((L, GW), 0)
    ci = _iota((L, GW), 1) % C_DK
    eye = ci == ri
    ahead = [jnp.where(f, ri - ci, ci - ri) for f in fwd_lanes]
    qs, ks, vs, gcs, betas = zip(*groups)
    ones = jnp.ones((L, L), bf16)
    gr = [_dot_lx(ones, jnp.where(eye, gcs[i], 0.0)) for i in n]
    kq = [_dot_nt(jnp.concatenate([ks[i], qs[i]], axis=0), _bd(ks[i])) for i in n]
    dec = [jnp.exp(jnp.where(ahead[i] >= 0, gcs[i] - gr[i], NEG_INF)) for i in n]
    a = [jnp.where(ahead[i] > 0, betas[i] * kq[i][:L] * dec[i], 0.0) for i in n]
    tm = [jnp.where(eye, 1.0, 0.0) - jnp.where(ri // 2 == ci // 2, a[i], 0.0) for i in n]
    s = 2
    while s < L:
        off = jnp.logical_and(ri // (2 * s) == ci // (2 * s), ri // s != ci // s)
        y = [_dot(jnp.where(off, a[i], 0.0), _bd(tm[i])) for i in n]
        tm = [tm[i] - _dot(tm[i], _bd(y[i])) for i in n]
        s *= 2
    eg = [jnp.exp(gcs[i]) for i in n]
    uw = [_dot(tm[i], jnp.concatenate([_bd(betas[i] * vs[i]), _bd(betas[i] * eg[i] * ks[i])], axis=1)) for i in n]
    out = []
    for i in n:
        g_last = jnp.where(fwd_lanes[i], gcs[i][L - 1:L], gcs[i][0:1])
        wq = jnp.concatenate([uw[i][:, GW:], qs[i] * eg[i]], axis=0).astype(bf16)
        qk = jnp.where(ahead[i] >= 0, kq[i][L:] * dec[i], 0.0).astype(bf16)
        kd = (ks[i] * jnp.exp(g_last - gcs[i])).astype(bf16)
        out.append((uw[i][:, :GW], wq, qk, kd, jnp.exp(g_last)))
    return out


def _gdn_scan(chunks, s_ref):
    L = C_CHUNK
    n = range(len(chunks))
    same = _iota((GW, GW), 0) // C_DK == _iota((GW, GW), 1) // C_DK
    s = [s_ref[i] for i in n]
    ws = [_dot(chunks[i][1], s[i]) for i in n]
    v_new = [chunks[i][0] - ws[i][:L] for i in n]
    o = [ws[i][L:] + _dot(chunks[i][2], _bd(v_new[i])) for i in n]
    upd = [_dot_tn(chunks[i][3], v_new[i]) for i in n]
    for i in n:
        s_ref[i] = s[i] * chunks[i][4] + jnp.where(same, upd[i], 0.0)
    return o


def _gdn_kernel(qc, kc, vc, gc, bc, ql, kl, vl, gl, bl, ef_ref, eb_ref, oc_ref, ol_ref,
                s_ref, u_s, wq_s, qk_s, kd_s, dl_s):
    L = C_CHUNK
    s_ref[...] = jnp.zeros(s_ref.shape, f32)
    lane = _iota((1, GW), 1)
    fwd_lanes = [lane >= 0, lane < 0, lane < LANE]
    ef, eb = ef_ref[...], eb_ref[...]
    r64 = _iota((L, L), 0)
    c64 = _iota((L, L), 1)
    tri_l = (c64 <= r64).astype(bf16)
    tri_u = (c64 >= r64).astype(bf16)

    def run(q_ref, k_ref, v_ref, g_ref, b_ref, o_ref, base):
        n_chunks = q_ref.shape[0] // L

        def prepare(tt, carry):
            groups = []
            for sub in range(PREP_UNROLL):
                t = tt * PREP_UNROLL + sub
                sl_f = pl.ds(pl.multiple_of(t * L, L), L)
                sl_b = pl.ds(pl.multiple_of((n_chunks - 1 - t) * L, L), L)
                gc_f = _dot_lx(tri_l, g_ref[sl_f, :])
                gc_b = _dot_lx(tri_u, g_ref[sl_b, :])
                gb_f = _dot_xl(jnp.concatenate([gc_f, b_ref[sl_f, :]], axis=0), ef)
                gb_b = _dot_xl(jnp.concatenate([gc_b, b_ref[sl_b, :]], axis=0), eb)
                data_f = [r[sl_f, :] for r in (q_ref, k_ref, v_ref)] + [gb_f[:L], gb_f[L:]]
                data_b = [r[sl_b, :] for r in (q_ref, k_ref, v_ref)] + [gb_b[:L], gb_b[L:]]
                groups += [tuple(x[:, :GW] for x in data_f), tuple(x[:, :GW] for x in data_b),
                           tuple(jnp.concatenate([xf[:, GW:], xb[:, GW:]], axis=1) for xf, xb in zip(data_f, data_b))]
            prepared = _gdn_prepare(groups, fwd_lanes * PREP_UNROLL)
            for n, (u, wq, qk, kd, dl) in enumerate(prepared):
                it = base + tt * PREP_UNROLL + n // N_GROUPS
                cols = slice((n % N_GROUPS) * GW, (n % N_GROUPS + 1) * GW)
                u_s[pl.ds(pl.multiple_of(it * L, L), L), cols] = u
                wq_s[pl.ds(pl.multiple_of(it * 2 * L, 2 * L), 2 * L), cols] = wq
                qk_s[pl.ds(pl.multiple_of(it * L, L), L), cols] = qk
                kd_s[pl.ds(pl.multiple_of(it * L, L), L), cols] = kd
                dl_s[pl.ds(pl.multiple_of(it * 8, 8), 8), cols] = jnp.broadcast_to(dl, (8, GW))
            return carry

        lax.fori_loop(0, n_chunks // PREP_UNROLL, prepare, 0)
        o_ref[...] = jnp.zeros(o_ref.shape, f32)

        def scan(t, carry):
            sl_f = pl.ds(pl.multiple_of(t * L, L), L)
            sl_b = pl.ds(pl.multiple_of((n_chunks - 1 - t) * L, L), L)
            it = base + t
            r1 = pl.ds(pl.multiple_of(it * L, L), L)
            r2 = pl.ds(pl.multiple_of(it * 2 * L, 2 * L), 2 * L)
            r8 = pl.ds(pl.multiple_of(it * 8, 8), 1)
            chunks = []
            for i in range(N_GROUPS):
                cols = slice(i * GW, (i + 1) * GW)
                chunks.append((u_s[r1, cols], wq_s[r2, cols], qk_s[r1, cols], kd_s[r1, cols], dl_s[r8, cols]))
            o0, o1, o2 = _gdn_scan(chunks, s_ref)
            o_ref[sl_f, :] += jnp.concatenate([o0, o2[:, :LANE]], axis=1)
            o_ref[sl_b, :] += jnp.concatenate([o1, o2[:, LANE:]], axis=1)
            return carry

        lax.fori_loop(0, n_chunks, scan, 0)

    run(qc, kc, vc, gc, bc, oc_ref, 0)
    run(ql, kl, vl, gl, bl, ol_ref, qc.shape[0] // L)


def _gdn(q, k, v, g, beta, ef, eb, lay):
    b, ctx, seq, nc = lay["b"], lay["ctx"], lay["seq"], lay["nc"]
    cs = lambda w: pl.BlockSpec((ctx, w), lambda i: (i, 0))
    ls = lambda w: pl.BlockSpec((seq, w), lambda i: (nc // seq + i, 0), pipeline_mode=pl.Buffered(1))
    const = lambda shape: pl.BlockSpec(shape, lambda i: (0,) * len(shape))
    widths = (C_W, C_W, C_W, LANE, LANE)
    n_it = (ctx + seq) // C_CHUNK
    return pl.pallas_call(
        _gdn_kernel,
        grid=(b,),
        in_specs=[cs(w) for w in widths] + [ls(w) for w in widths] + [const(ef.shape), const(eb.shape)],
        out_specs=[pl.BlockSpec((ctx, C_W), lambda i: (i, 0)), pl.BlockSpec((seq, C_W), lambda i: (i, 0))],
        out_shape=[jax.ShapeDtypeStruct((nc, C_W), f32), jax.ShapeDtypeStruct((b * seq, C_W), f32)],
        scratch_shapes=[pltpu.VMEM((N_GROUPS, GW, GW), f32),
                        pltpu.VMEM((n_it * C_CHUNK, GDN_W), f32),
                        pltpu.VMEM((n_it * 2 * C_CHUNK, GDN_W), bf16),
                        pltpu.VMEM((n_it * C_CHUNK, GDN_W), bf16),
                        pltpu.VMEM((n_it * C_CHUNK, GDN_W), bf16),
                        pltpu.VMEM((n_it * 8, GDN_W), f32)],
        compiler_params=_cparams(1), name="gated_deltanet",
    )(q, k, v, g, beta, q, k, v, g, beta, ef, eb)


def _outproj_kernel(h_ref, mod_ref, *refs, n_ctx_tiles):
    gate_ref, w_ref, nw_ref, bd64_ref, o_ref = refs[-5:]
    mixed = [r[...] for r in refs[-8:-5]]
    if n_ctx_tiles:
        is_ctx = pl.program_id(0) < n_ctx_tiles
        mixed = [jnp.where(is_ctx, c[...], x) for c, x in zip(refs[:3], mixed)]
    oa, ob, oc = mixed
    bd64 = bd64_ref[...]
    nw = nw_ref[...]

    def normed(x, wrow):
        s = _group_sum(x * x, bd64, terms=1)
        return x * lax.rsqrt(s * (1.0 / C_DK) + EPS) * wrow

    ya = normed(oa, nw[0:1, :A_W]).astype(bf16)
    yc = (normed(oc, nw[1:2, :C_W]) * _silu(gate_ref[...])).astype(bf16)
    y = jnp.dot(ya, w_ref[:A_W, :], preferred_element_type=f32)
    y += jnp.dot(ob, w_ref[A_W:A_W + B_WIDE, :], preferred_element_type=f32)
    y += jnp.dot(yc, w_ref[A_W + B_WIDE:, :], preferred_element_type=f32)
    o_ref[...] = h_ref[...] + mod_ref[0][5:6] * y


def _outproj(h, mod, ctx_outs, lat_outs, gate, w_out, nw, bd128, lay):
    tm, d = lay["tm"], D_MODEL
    nct = lay["nc"] // tm
    off = 0 if ctx_outs else nct
    n_tiles = h.shape[0] // tm - off
    group = functools.partial(_mod_group, lay=lay, off=off)
    rowo = lambda w: pl.BlockSpec((tm, w), lambda i: (i + off, 0))
    rowc = lambda w: pl.BlockSpec((tm, w), lambda i: (jnp.minimum(i, nct - 1), 0))
    rowl = lambda w: pl.BlockSpec((tm, w), lambda i: (jnp.maximum(i + off - nct, 0), 0))
    const = lambda shape: pl.BlockSpec(shape, lambda i: (0,) * len(shape))
    widths = (A_W, B_WIDE, C_W)
    ctx_specs = [rowc(w) for w in widths] if ctx_outs else []
    return pl.pallas_call(
        functools.partial(_outproj_kernel, n_ctx_tiles=nct if ctx_outs else 0),
        grid=(n_tiles,),
        in_specs=[rowo(d), pl.BlockSpec((1, 9, d), lambda i: (group(i), 0, 0))] + ctx_specs
                 + [rowl(w) for w in widths]
                 + [rowo(C_W), pl.BlockSpec((OUT_WIDE, d), lambda i: (0, 0), pipeline_mode=pl.Buffered(1)),
                    const(nw.shape), const(bd128.shape)],
        out_specs=pl.BlockSpec((tm, d), lambda i: (i, 0)),
        out_shape=jax.ShapeDtypeStruct((n_tiles * tm, d), f32),
        compiler_params=_cparams(1), name="mixer_out_proj",
    )(h, mod, *(ctx_outs or ()), *lat_outs, gate, w_out, nw, bd128)


def _block_ones(n, group):
    idx = np.arange(n) // group
    return jnp.asarray(idx[:, None] == idx[None, :], dtype=bf16)


def _rope_table(seq, d, pad_rows):
    half, quarter = d // 2, d // 4
    rows = seq // GRID_W
    row = jnp.repeat(jnp.arange(rows, dtype=f32), GRID_W)
    col = jnp.tile(jnp.arange(GRID_W, dtype=f32), rows)
    inv = ROPE_THETA ** (-jnp.arange(0, half, 2, dtype=f32) / half)
    ld = np.arange(LANE) % d
    pos = jnp.where(jnp.asarray(ld < half)[None, :], row[:, None], col[:, None])
    ang = pos * inv[np.asarray((ld % half) % quarter)][None, :]
    sign = jnp.asarray(np.where((ld % half) < quarter, -1.0, 1.0), dtype=f32)[None, :]
    cos = jnp.concatenate([jnp.ones((pad_rows, LANE), f32), jnp.cos(ang)], axis=0)
    sin = jnp.concatenate([jnp.zeros((pad_rows, LANE), f32), jnp.sin(ang) * sign], axis=0)
    return cos, sin


def _arrange_w_in(w):
    offs = np.concatenate([[0], np.cumsum(IN_SIZES)])
    part = lambda n: w[:, offs[n]:offs[n + 1]]
    zeros = lambda width: jnp.zeros((w.shape[0], width), w.dtype)
    qb = part(3)
    pieces = [part(0), part(1), part(2)]
    for h in range(B_HEADS):
        head = qb[:, h * B_DIM:(h + 1) * B_DIM]
        pieces += [head, zeros(B_DIM)] if h // B_GROUP == 0 else [zeros(B_DIM), head]
    pieces += [part(4), part(5), part(6), part(7), part(8), zeros(LANE - IN_SIZES[8]), part(9), zeros(LANE - IN_SIZES[9])]
    out = jnp.concatenate(pieces, axis=1).astype(bf16)
    assert out.shape[1] == IN_WIDE
    return out


def _arrange_w_out(w):
    zeros = jnp.zeros((B_DIM, w.shape[1]), w.dtype)
    pieces = [w[:A_W]]
    for h in range(B_HEADS):
        head = w[A_W + h * B_DIM:A_W + (h + 1) * B_DIM]
        pieces += [head, zeros] if h // B_GROUP == 0 else [zeros, head]
    pieces.append(w[A_W + B_HEADS * B_DIM:])
    out = jnp.concatenate(pieces, axis=0).astype(bf16)
    assert out.shape[0] == OUT_WIDE
    return out


def _expander(direction):
    m = np.zeros((LANE, C_W), np.float32)
    for h in range(C_HEADS):
        m[direction * C_HEADS + h, h * C_DK:(h + 1) * C_DK] = 1.0
    return jnp.asarray(m, dtype=bf16)


def _pad_lanes(x, width):
    return jnp.pad(x, ((0, 0), (0, width - x.shape[1])))


def kernel(x, c, ctx, c_ctx, w_mod, b_mod, ffn1_w1, ffn1_w2, ffn2_w1, ffn2_w2, w_in, w_out,
           a_qnorm, a_knorm, a_lambda, a_subln, b_qnorm, b_knorm, b_sink,
           c_conv, c_A_log, c_dt_bias, c_onorm):
    b, seq, d = x.shape
    n_ctx = ctx.shape[1]
    nc = b * n_ctx
    tm = 512 if (nc % 512 == 0 and seq % 512 == 0) else 256
    lay = dict(b=b, ctx=n_ctx, seq=seq, nc=nc, tm=tm)
    assert d == D_MODEL and seq % 256 == 0 and n_ctx % 256 == 0 and nc % seq == 0 and seq % GRID_W == 0

    h = jnp.concatenate([ctx.reshape(nc, d), x.reshape(b * seq, d)], axis=0)
    cvec = jnp.zeros((16, d), f32).at[:b].set(c).at[b].set(c_ctx)
    mod_all = _modulation(cvec, w_mod, b_mod).reshape(DEPTH, 16, 9, d)

    bd32, bd64, bd64s = _block_ones(256, A_DIM), _block_ones(256, B_DIM), _block_ones(LANE, C_DK)
    tabs = _rope_table(seq, A_DIM, tm) + _rope_table(seq, B_DIM, tm)
    ef, eb = _expander(0), _expander(1)

    for l in range(DEPTH):
        last = l == DEPTH - 1
        lam_init = 0.8 - 0.6 * float(np.exp(-0.3 * l))
        mod = mod_all[l]
        lf = a_lambda[l].astype(f32)
        lam = (jnp.exp(jnp.sum(lf[0] * lf[1])) - jnp.exp(jnp.sum(lf[2] * lf[3])) + lam_init).reshape(1)
        nw_in = jnp.stack([jnp.tile(a_qnorm[l], 8) * (A_DIM ** -0.5 * LOG2E),
                           jnp.tile(a_knorm[l], 8),
                           jnp.tile(b_qnorm[l], 4) * (B_DIM ** -0.5 * LOG2E),
                           jnp.tile(b_knorm[l], 4)] + [jnp.zeros((256,), f32)] * 4)
        nw_out = jnp.stack([_pad_lanes((jnp.tile(a_subln[l], 4) * (1.0 - lam_init))[None], C_W)[0],
                            jnp.tile(c_onorm[l], C_HEADS)] + [jnp.zeros((C_W,), f32)] * 6)
        gpar = jnp.stack([_pad_lanes(jnp.exp(c_A_log[l].astype(f32)).reshape(1, -1), LANE)[0],
                          _pad_lanes(c_dt_bias[l].astype(f32).reshape(1, -1), LANE)[0]] + [jnp.zeros((LANE,), f32)] * 6)
        cw = jnp.concatenate([c_conv[l], jnp.zeros((8 - C_CONV, 3 * C_W), f32)], axis=0)

        h = _ffn(h, mod, ffn1_w1[l].astype(bf16), ffn1_w2[l].astype(bf16), 0, lay, False)
        qa, ka, va, qb, kb, vb, cq, gate, pa, pb = _inproj(h, mod, _arrange_w_in(w_in[l]), nw_in, tabs, bd32, bd64, lay)
        sink = b_sink[l].astype(f32) * LOG2E
        oa = _attn_a(lam, qa, ka, va, lay, True)
        ob = _attn_b(sink, qb, kb, vb, lay, True)
        gq, gk, gv, gg, gbeta = _gdn_prep(cq, pa, pb, cw, gpar, bd64s, lay)
        oc_ctx, oc = _gdn(gq, gk, gv, gg, gbeta, ef, eb, lay)
        ctx_outs = None
        if not last:
            ctx_outs = (_attn_a(lam, qa, ka, va, lay, False), _attn_b(sink, qb, kb, vb, lay, False), oc_ctx)
        h = _outproj(h, mod, ctx_outs, (oa, ob, oc), gate, _arrange_w_out(w_out[l]), nw_out, bd64s, lay)
        if last:
            lay = dict(lay, nc=0)
        h = _ffn(h, mod, ffn2_w1[l].astype(bf16), ffn2_w2[l].astype(bf16), 6, lay, False)
    return h.reshape(b, seq, d)
```

```python
import functools

import numpy as np
import jax
import jax.numpy as jnp
from jax import lax
from jax.experimental import pallas as pl
from jax.experimental.pallas import tpu as pltpu

f32 = jnp.float32
bf16 = jnp.bfloat16

D_MODEL = 1024
DEPTH = 2
GRID_W = 64
EPS = 1e-6
NEG_INF = -1e30
LOG2E = 1.4426950408889634
ROPE_THETA = 10000.0
D_FF = 2816
A_HEADS, A_DIM, A_VDIM = 4, 32, 64
A_W = A_HEADS * A_VDIM
B_HEADS, B_KV_HEADS, B_DIM = 6, 2, 64
B_GROUP = B_HEADS // B_KV_HEADS
B_BLOCK = 128
C_HEADS, C_DK, C_CONV, C_CHUNK = 6, 64, 5, 64
C_W = C_HEADS * C_DK
IN_SIZES = (256, 256, 256, 384, 128, 128, 1152, 384, 12, 12)
LANE = 128
B_WIDE = B_HEADS * LANE
SEG = dict(qa=(0, 256), ka=(256, 256), va=(512, 256), qb=(768, B_WIDE), kb=(1536, 128), vb=(1664, 128),
           cq=(1792, 1152), gate=(2944, 384), a=(3328, 128), b=(3456, 128))
IN_WIDE = 3584
OUT_WIDE = A_W + B_WIDE + C_W
VMEM_LIMIT = 56 * 1024 * 1024


def _cparams(n_axes):
    return pltpu.CompilerParams(dimension_semantics=("arbitrary",) * n_axes, vmem_limit_bytes=VMEM_LIMIT)


def _dot(a, b):
    return jnp.dot(a.astype(bf16), b.astype(bf16), preferred_element_type=f32)


def _dot_nt(a, b):
    return lax.dot_general(a.astype(bf16), b.astype(bf16), (((1,), (1,)), ((), ())), preferred_element_type=f32)


def _dot_tn(a, b):
    return lax.dot_general(a.astype(bf16), b.astype(bf16), (((0,), (0,)), ((), ())), preferred_element_type=f32)


def _split(x, n):
    parts = []
    for _ in range(n - 1):
        p = x.astype(bf16)
        parts.append(p)
        x = x - p.astype(f32)
    parts.append(x.astype(bf16))
    return parts


def _dot_xl(x, m, n=3):
    return sum(jnp.dot(p, m, preferred_element_type=f32) for p in _split(x, n))


def _dot_lx(m, x, n=3):
    return sum(jnp.dot(m, p, preferred_element_type=f32) for p in _split(x, n))


def _group_sum(xx, bd, terms=2):
    w = bd.shape[0]
    cols = [_dot_xl(xx[:, j:j + w], bd, terms) for j in range(0, xx.shape[1], w)]
    return cols[0] if len(cols) == 1 else jnp.concatenate(cols, axis=1)


def _silu(x):
    return x * jax.nn.sigmoid(x)


def _iota(shape, dim):
    return lax.broadcasted_iota(jnp.int32, shape, dim)


def _modulated_norm(h, shift, scale):
    hn = h * lax.rsqrt(jnp.mean(h * h, axis=-1, keepdims=True) + EPS)
    return hn * (1.0 + scale) + shift


def _mod_kernel(c_ref, w_ref, b_ref, o_ref):
    s = _silu(c_ref[...])
    w = w_ref[0]
    s_hi, s_lo = _split(s, 2)
    w_hi, w_lo = _split(w, 2)
    acc = jnp.dot(s_hi, w_hi, preferred_element_type=f32)
    acc += jnp.dot(s_hi, w_lo, preferred_element_type=f32)
    acc += jnp.dot(s_lo, w_hi, preferred_element_type=f32)
    o_ref[0] = acc + b_ref[0]


def _modulation(cvec, w_mod, b_mod):
    depth, d, n = w_mod.shape
    tn = 1024
    return pl.pallas_call(
        _mod_kernel,
        grid=(depth, n // tn),
        in_specs=[pl.BlockSpec((cvec.shape[0], d), lambda l, j: (0, 0)),
                  pl.BlockSpec((1, d, tn), lambda l, j: (l, 0, j)),
                  pl.BlockSpec((1, 1, tn), lambda l, j: (l, 0, j))],
        out_specs=pl.BlockSpec((1, cvec.shape[0], tn), lambda l, j: (l, 0, j)),
        out_shape=jax.ShapeDtypeStruct((depth, cvec.shape[0], n), f32),
        compiler_params=_cparams(2), name="modulation",
    )(cvec, w_mod, b_mod.reshape(depth, 1, n))


def _ffn_kernel(*refs, idx, n_chunks, n_ctx_tiles):
    mod_ref, w1_ref, w2_ref, o_ref = refs[-4:]
    h = refs[-5][...]
    if n_ctx_tiles:
        h = jnp.where(pl.program_id(0) < n_ctx_tiles, refs[0][...], h)
    mod = mod_ref[0]
    hn = _modulated_norm(h, mod[idx:idx + 1], mod[idx + 1:idx + 2]).astype(bf16)
    ck = D_FF // n_chunks
    acc = None
    for c in range(n_chunks):
        g = jnp.dot(hn, w1_ref[:, c * ck:(c + 1) * ck], preferred_element_type=f32)
        u = jnp.dot(hn, w1_ref[:, D_FF + c * ck:D_FF + (c + 1) * ck], preferred_element_type=f32)
        a = (_silu(g) * u).astype(bf16)
        part = jnp.dot(a, w2_ref[c * ck:(c + 1) * ck, :], preferred_element_type=f32)
        acc = part if acc is None else acc + part
    o_ref[...] = h + (0.5 * mod[idx + 2:idx + 3]) * acc


def _ffn(h, mod, w1, w2, idx, lay):
    tm, d = lay["tm"], D_MODEL
    pair = isinstance(h, tuple)
    nct = lay["nc"] // tm
    n_tiles = (h[0].shape[0] + h[1].shape[0] if pair else h.shape[0]) // tm
    group = functools.partial(_mod_group, lay=lay, off=0)
    if pair:
        h_specs = [pl.BlockSpec((tm, d), lambda i: (jnp.minimum(i, nct - 1), 0)),
                   pl.BlockSpec((tm, d), lambda i: (jnp.maximum(i - nct, 0), 0))]
    else:
        h_specs = [pl.BlockSpec((tm, d), lambda i: (i, 0))]
    return pl.pallas_call(
        functools.partial(_ffn_kernel, idx=idx, n_chunks=2, n_ctx_tiles=nct if pair else 0),
        grid=(n_tiles,),
        in_specs=h_specs + [pl.BlockSpec((1, 9, d), lambda i: (group(i), 0, 0)),
                            pl.BlockSpec((d, 2 * D_FF), lambda i: (0, 0), pipeline_mode=pl.Buffered(1)),
                            pl.BlockSpec((D_FF, d), lambda i: (0, 0), pipeline_mode=pl.Buffered(1))],
        out_specs=pl.BlockSpec((tm, d), lambda i: (i, 0)),
        out_shape=jax.ShapeDtypeStruct((n_tiles * tm, d), f32),
        compiler_params=_cparams(1), name=f"ffn_half_step_{idx}",
    )(*(h if pair else (h,)), mod, w1, w2)


def _mod_group(i, lay, off):
    r = (i + off) * lay["tm"]
    return jnp.where(r < lay["nc"], lay["b"], (r - lay["nc"]) // lay["seq"])


def _rope(x, cos, sin, quarter):
    w = x.shape[1]
    reps = w // LANE
    if reps > 1:
        cos = jnp.concatenate([cos] * reps, axis=1)
        sin = jnp.concatenate([sin] * reps, axis=1)
    first = (_iota((1, w), 1) % (2 * quarter)) < quarter
    swapped = jnp.where(first, pltpu.roll(x, w - quarter, 1), pltpu.roll(x, quarter, 1))
    return x * cos + swapped * sin


def _inproj_kernel(h_ref, mod_ref, w_ref, nw_ref, ca_ref, sa_ref, cb_ref, sb_ref, bd32_ref, bd64_ref,
                   qa_o, ka_o, va_o, qb_o, kb_o, vb_o, cq_o, gate_o, a_o, b_o):
    mod = mod_ref[0]
    hn = _modulated_norm(h_ref[...], mod[3:4], mod[4:5]).astype(bf16)

    def proj(name):
        off, width = SEG[name]
        return jnp.dot(hn, w_ref[:, off:off + width], preferred_element_type=f32)

    def normed(x, bd, group, wrow):
        s = _group_sum(x * x, bd, terms=1)
        return x * lax.rsqrt(s * (1.0 / group) + EPS) * wrow

    nw = nw_ref[...]
    bd32, bd64 = bd32_ref[...], bd64_ref[...]
    ca, sa, cb, sb = ca_ref[...], sa_ref[...], cb_ref[...], sb_ref[...]
    qa_o[...] = _rope(normed(proj("qa"), bd32, A_DIM, nw[0:1, :256]), ca, sa, A_DIM // 4).astype(bf16)
    ka_o[...] = _rope(normed(proj("ka"), bd32, A_DIM, nw[1:2, :256]), ca, sa, A_DIM // 4).astype(bf16)
    va_o[...] = proj("va").astype(bf16)
    qb = proj("qb")
    wq = jnp.concatenate([nw[2:3, :256]] * (B_WIDE // 256), axis=1)
    qb_o[...] = _rope(normed(qb, bd64, B_DIM, wq), cb, sb, B_DIM // 4).astype(bf16)
    kb_o[...] = _rope(normed(proj("kb"), bd64[:LANE, :LANE], B_DIM, nw[3:4, :LANE]), cb, sb, B_DIM // 4).astype(bf16)
    vb_o[...] = proj("vb").astype(bf16)
    cq_o[...] = proj("cq")
    gate_o[...] = proj("gate")
    a_o[...] = proj("a")
    b_o[...] = proj("b")


def _inproj(h, mod, w_in, nw, tabs, bd32, bd64, lay):
    tm, d = lay["tm"], D_MODEL
    n = h.shape[0]
    n_tiles = n // tm
    nct = lay["nc"] // tm
    spt = lay["seq"] // tm
    group = functools.partial(_mod_group, lay=lay, off=0)

    def tab_idx(i):
        return jnp.where(i < nct, 0, 1 + (i - nct) % spt)

    row = lambda w: pl.BlockSpec((tm, w), lambda i: (i, 0))
    const = lambda shape: pl.BlockSpec(shape, lambda i: (0,) * len(shape))
    tab = pl.BlockSpec((tm, LANE), lambda i: (tab_idx(i), 0))
    names = ("qa", "ka", "va", "qb", "kb", "vb", "cq", "gate", "a", "b")
    dts = (bf16,) * 6 + (f32,) * 4
    return pl.pallas_call(
        _inproj_kernel,
        grid=(n_tiles,),
        in_specs=[row(d), pl.BlockSpec((1, 9, d), lambda i: (group(i), 0, 0)),
                  pl.BlockSpec((d, IN_WIDE), lambda i: (0, 0), pipeline_mode=pl.Buffered(1)),
                  const(nw.shape), tab, tab, tab, tab, const(bd32.shape), const(bd64.shape)],
        out_specs=[row(SEG[k][1]) for k in names],
        out_shape=[jax.ShapeDtypeStruct((n, SEG[k][1]), dt) for k, dt in zip(names, dts)],
        compiler_params=_cparams(1), name="mixer_in_proj",
    )(h, mod, w_in, nw, *tabs, bd32, bd64)


A_KBLOCK = 256
A_INTERLEAVE = 8


def _attn_a_kernel(lam_ref, q_ref, *refs, n_seg):
    k_refs, v_refs = refs[:n_seg], refs[n_seg:2 * n_seg]
    o_ref, vt_ref = refs[2 * n_seg], refs[2 * n_seg + 1]
    tq = q_ref.shape[0]
    n_maps = 2 * A_HEADS

    @pl.when(pl.program_id(1) == 0)
    def _():
        off = 0
        for v_ref in v_refs:
            for r in range(0, v_ref.shape[0], A_KBLOCK):
                vt_ref[:, off + r:off + r + A_KBLOCK] = v_ref[r:r + A_KBLOCK, :].astype(f32).T.astype(bf16)
            off += v_ref.shape[0]

    lam = lam_ref[0]
    qt = q_ref[...].astype(f32).T
    feat = _iota((A_W, tq), 0)
    qms = [jnp.where(feat // A_DIM == n, qt, 0.0).astype(bf16) for n in range(n_maps)]
    normed = [None] * n_maps
    for g0 in range(0, n_maps, A_INTERLEAVE):
        grp = range(g0, g0 + A_INTERLEAVE)
        m_run = {n: jnp.full((1, tq), NEG_INF, f32) for n in grp}
        l_run = {n: jnp.zeros((1, tq), f32) for n in grp}
        acc = {n: jnp.zeros((A_VDIM, tq), f32) for n in grp}
        off = 0
        for k_ref in k_refs:
            for r in range(0, k_ref.shape[0], A_KBLOCK):
                kb = k_ref[r:r + A_KBLOCK, :]
                st = {n: jnp.dot(kb, qms[n], preferred_element_type=f32) for n in grp}
                m_new = {n: jnp.maximum(m_run[n], jnp.max(st[n], axis=0, keepdims=True)) for n in grp}
                alpha = {n: jnp.exp2(m_run[n] - m_new[n]) for n in grp}
                e = {n: jnp.exp2(st[n] - m_new[n]) for n in grp}
                l_run = {n: alpha[n] * l_run[n] + jnp.sum(e[n], axis=0, keepdims=True) for n in grp}
                pv = {n: jnp.dot(vt_ref[(n // 2) * A_VDIM:(n // 2 + 1) * A_VDIM, off + r:off + r + A_KBLOCK],
                                 e[n].astype(bf16), preferred_element_type=f32) for n in grp}
                acc = {n: alpha[n] * acc[n] + pv[n] for n in grp}
                m_run = m_new
            off += k_ref.shape[0]
        for n in grp:
            normed[n] = acc[n] * ((lam if n % 2 else 1.0) / l_run[n])
    heads = [normed[2 * h] - normed[2 * h + 1] for h in range(A_HEADS)]
    o_ref[...] = jnp.concatenate(heads, axis=0).T


def _attn_a(lam, qa, ka, va, lay, latent):
    b, ctx, seq, nc = lay["b"], lay["ctx"], lay["seq"], lay["nc"]
    smem = pl.BlockSpec(memory_space=pltpu.SMEM)
    if latent:
        tq = 256
        qpb = seq // tq
        grid = (b, qpb)
        qspec = pl.BlockSpec((tq, A_W), lambda i, j: (nc // tq + i * qpb + j, 0))
        kv = [pl.BlockSpec((ctx, A_W), lambda i, j: (i, 0)), pl.BlockSpec((seq, A_W), lambda i, j: (nc // seq + i, 0))]
        ospec = pl.BlockSpec((tq, A_W), lambda i, j: (i * qpb + j, 0))
        rows = b * seq
    else:
        grid = (b, 1)
        qspec = pl.BlockSpec((ctx, A_W), lambda i, j: (i, 0))
        kv = [pl.BlockSpec((ctx, A_W), lambda i, j: (i, 0))]
        ospec = qspec
        rows = nc
    n_seg = len(kv)
    return pl.pallas_call(
        functools.partial(_attn_a_kernel, n_seg=n_seg),
        grid=grid,
        in_specs=[smem, qspec] + kv + kv,
        out_specs=ospec,
        out_shape=jax.ShapeDtypeStruct((rows, A_W), f32),
        scratch_shapes=[pltpu.VMEM((A_W, ctx + seq if latent else ctx), bf16)],
        compiler_params=_cparams(2), name="diff_attention_lat" if latent else "diff_attention_ctx",
    )(lam, qa, *([ka] * n_seg), *([va] * n_seg))


def _attn_b_kernel(sink_ref, q_ref, kc_ref, vc_ref, *refs, latent):
    o_ref = refs[-1]
    q = q_ref[...]
    rows = q.shape[0]
    lane = _iota((1, LANE), 1)
    keys, vals, masks = [kc_ref[...]], [vc_ref[...]], [None]
    if latent:
        kl_ref, vl_ref = refs[0], refs[1]
        t = pl.program_id(1)
        nb = pl.num_programs(1)
        r = _iota((B_GROUP * rows, B_BLOCK), 0) % rows
        c = _iota((B_GROUP * rows, B_BLOCK), 1)
        for off in (-1, 0, 1):
            blk = jnp.clip(t + off, 0, nb - 1)
            start = pl.multiple_of(blk * B_BLOCK, B_BLOCK)
            keys.append(kl_ref[pl.ds(start, B_BLOCK), :])
            vals.append(vl_ref[pl.ds(start, B_BLOCK), :])
            inside = jnp.logical_and(t + off >= 0, t + off < nb)
            if off == -1:
                masks.append(jnp.logical_and(c >= r, inside))
            elif off == 1:
                masks.append(jnp.logical_and(c <= r, inside))
            else:
                masks.append(None)
    for g in range(B_KV_HEADS):
        heads = range(g * B_GROUP, (g + 1) * B_GROUP)
        qg = jnp.concatenate([q[:, h * LANE:(h + 1) * LANE] for h in heads], axis=0)
        sink = jnp.concatenate([jnp.full((rows, 1), sink_ref[h], f32) for h in heads], axis=0)
        s = []
        for k, msk in zip(keys, masks):
            x = _dot_nt(qg, k)
            s.append(x if msk is None else jnp.where(msk, x, NEG_INF))
        mx = functools.reduce(jnp.maximum, [jnp.max(x, axis=-1, keepdims=True) for x in s] + [sink])
        e = [jnp.exp2(x - mx) for x in s]
        den = functools.reduce(jnp.add, [jnp.sum(x, axis=-1, keepdims=True) for x in e]) + jnp.exp2(sink - mx)
        rden = 1.0 / den
        og = None
        for x, v in zip(e, vals):
            part = jnp.dot((x * rden).astype(bf16), v, preferred_element_type=f32)
            og = part if og is None else og + part
        og = jnp.where(lane // B_DIM == g, og, 0.0).astype(bf16)
        for n, h in enumerate(heads):
            o_ref[:, h * LANE:(h + 1) * LANE] = og[n * rows:(n + 1) * rows]


def _attn_b(sink, qb, kb, vb, lay, latent):
    b, ctx, seq, nc = lay["b"], lay["ctx"], lay["seq"], lay["nc"]
    smem = pl.BlockSpec(memory_space=pltpu.SMEM)
    cspec = pl.BlockSpec((ctx, LANE), lambda i, j: (i, 0))
    if latent:
        nb = seq // B_BLOCK
        grid = (b, nb)
        qspec = pl.BlockSpec((B_BLOCK, B_WIDE), lambda i, j: (nc // B_BLOCK + i * nb + j, 0))
        lspec = pl.BlockSpec((seq, LANE), lambda i, j: (nc // seq + i, 0))
        in_specs = [smem, qspec, cspec, cspec, lspec, lspec]
        args = (sink, qb, kb, vb, kb, vb)
        ospec = pl.BlockSpec((B_BLOCK, B_WIDE), lambda i, j: (i * nb + j, 0))
        rows = b * seq
    else:
        grid = (b, 1)
        qspec = pl.BlockSpec((ctx, B_WIDE), lambda i, j: (i, 0))
        in_specs = [smem, qspec, cspec, cspec]
        args = (sink, qb, kb, vb)
        ospec = qspec
        rows = nc
    return pl.pallas_call(
        functools.partial(_attn_b_kernel, latent=latent),
        grid=grid, in_specs=in_specs, out_specs=ospec,
        out_shape=jax.ShapeDtypeStruct((rows, B_WIDE), bf16),
        compiler_params=_cparams(2), name="window_attention_lat" if latent else "sink_attention_ctx",
    )(*args)


HALO = 8


def _gdn_prep_kernel(x_ref, prev_ref, next_ref, a_ref, b_ref, cw_ref, par_ref, bd64_ref,
                     q_o, k_o, v_o, g_o, beta_o, *, nct, cpt, spt):
    i = pl.program_id(0)
    tm = x_ref.shape[0]
    j = jnp.where(i < nct, i % cpt, (i - nct) % spt)
    per_seq = jnp.where(i < nct, cpt, spt)
    first = j == 0
    last = j == per_seq - 1
    prev = jnp.where(first, 0.0, prev_ref[...])
    nxt = jnp.where(last, 0.0, next_ref[...])
    xx = jnp.concatenate([prev, x_ref[...], nxt], axis=0)
    cw = cw_ref[...]
    rows = xx.shape[0]
    y = None
    for tap in range(C_CONV):
        shift = (C_CONV // 2 - tap) % rows
        sh = xx if shift == 0 else pltpu.roll(xx, shift, 0)
        term = sh[HALO:HALO + tm] * cw[tap:tap + 1]
        y = term if y is None else y + term
    y = _silu(y)
    bd64 = bd64_ref[...]

    def l2n(t):
        return t * lax.rsqrt(_group_sum(t * t, bd64, terms=1) + EPS)

    q_o[...] = (l2n(y[:, :C_W]) * (C_DK ** -0.5)).astype(bf16)
    k_o[...] = l2n(y[:, C_W:2 * C_W]).astype(bf16)
    v_o[...] = y[:, 2 * C_W:].astype(bf16)
    par = par_ref[...]
    z = a_ref[...] + par[1:2]
    softplus = jnp.maximum(z, 0.0) + jnp.log1p(jnp.exp(-jnp.abs(z)))
    g_o[...] = -par[0:1] * softplus
    beta_o[...] = jax.nn.sigmoid(b_ref[...])


def _gdn_prep(cq, a, bb, cw, par, bd64, lay):
    tm = 256
    n = cq.shape[0]
    n_tiles = n // tm
    nct, spt = lay["nc"] // tm, lay["seq"] // tm
    hb = tm // HALO
    last_blk = n // HALO - 1
    row = lambda w: pl.BlockSpec((tm, w), lambda i: (i, 0))
    const = lambda shape: pl.BlockSpec(shape, lambda i: (0,) * len(shape))
    return pl.pallas_call(
        functools.partial(_gdn_prep_kernel, nct=nct, cpt=lay["ctx"] // tm, spt=spt),
        grid=(n_tiles,),
        in_specs=[row(3 * C_W),
                  pl.BlockSpec((HALO, 3 * C_W), lambda i: (jnp.maximum(i * hb - 1, 0), 0)),
                  pl.BlockSpec((HALO, 3 * C_W), lambda i: (jnp.minimum((i + 1) * hb, last_blk), 0)),
                  row(LANE), row(LANE), const(cw.shape), const(par.shape), const(bd64.shape)],
        out_specs=[row(C_W), row(C_W), row(C_W), row(LANE), row(LANE)],
        out_shape=[jax.ShapeDtypeStruct((n, C_W), bf16)] * 3 + [jax.ShapeDtypeStruct((n, LANE), f32)] * 2,
        compiler_params=_cparams(1), name="gdn_inputs",
    )(cq, cq, cq, a, bb, cw, par, bd64)


GW = 4 * C_DK


def _bd(x):
    t = jnp.concatenate([x.astype(bf16)] * 4, axis=0)
    same = _iota((GW, GW), 0) // C_DK == _iota((GW, GW), 1) // C_DK
    return jnp.where(same, t, jnp.zeros_like(t))


N_GROUPS = 3
GDN_W = N_GROUPS * GW
PREP_UNROLL = 4


def _gdn_prepare(groups, fwd_lanes):
    L = C_CHUNK
    n = range(len(groups))
    ri = _iota((L, GW), 0)
    ci = _iota((L, GW), 1) % C_DK
    eye = ci == ri
    ahead = [jnp.where(f, ri - ci, ci - ri) for f in fwd_lanes]
    qs, ks, vs, gcs, betas = zip(*groups)
    ones = jnp.ones((L, L), bf16)
    gr = [_dot_lx(ones, jnp.where(eye, gcs[i], 0.0)) for i in n]
    kq = [_dot_nt(jnp.concatenate([ks[i], qs[i]], axis=0), _bd(ks[i])) for i in n]
    dec = [jnp.exp(jnp.where(ahead[i] >= 0, gcs[i] - gr[i], NEG_INF)) for i in n]
    a = [jnp.where(ahead[i] > 0, betas[i] * kq[i][:L] * dec[i], 0.0) for i in n]
    tm = [jnp.where(eye, 1.0, 0.0) - jnp.where(ri // 2 == ci // 2, a[i], 0.0) for i in n]
    s = 2
    while s < L:
        off = jnp.logical_and(ri // (2 * s) == ci // (2 * s), ri // s != ci // s)
        y = [_dot(jnp.where(off, a[i], 0.0), _bd(tm[i])) for i in n]
        tm = [tm[i] - _dot(tm[i], _bd(y[i])) for i in n]
        s *= 2
    eg = [jnp.exp(gcs[i]) for i in n]
    uw = [_dot(tm[i], jnp.concatenate([_bd(betas[i] * vs[i]), _bd(betas[i] * eg[i] * ks[i])], axis=1)) for i in n]
    out = []
    for i in n:
        g_last = jnp.where(fwd_lanes[i], gcs[i][L - 1:L], gcs[i][0:1])
        wq = jnp.concatenate([uw[i][:, GW:], qs[i] * eg[i]], axis=0).astype(bf16)
        qk = jnp.where(ahead[i] >= 0, kq[i][L:] * dec[i], 0.0).astype(bf16)
        kd = (ks[i] * jnp.exp(g_last - gcs[i])).astype(bf16)
        out.append((uw[i][:, :GW], wq, qk, kd, jnp.exp(g_last)))
    return out


def _gdn_scan(chunks, s_ref):
    L = C_CHUNK
    n = range(len(chunks))
    same = _iota((GW, GW), 0) // C_DK == _iota((GW, GW), 1) // C_DK
    s = [s_ref[i] for i in n]
    ws = [_dot(chunks[i][1], s[i]) for i in n]
    v_new = [chunks[i][0] - ws[i][:L] for i in n]
    o = [ws[i][L:] + _dot(chunks[i][2], _bd(v_new[i])) for i in n]
    upd = [_dot_tn(chunks[i][3], v_new[i]) for i in n]
    for i in n:
        s_ref[i] = s[i] * chunks[i][4] + jnp.where(same, upd[i], 0.0)
    return o


def _gdn_kernel(qc, kc, vc, gc, bc, ql, kl, vl, gl, bl, ef_ref, eb_ref, oc_ref, ol_ref,
                s_ref, u_s, wq_s, qk_s, kd_s, dl_s):
    L = C_CHUNK
    s_ref[...] = jnp.zeros(s_ref.shape, f32)
    lane = _iota((1, GW), 1)
    fwd_lanes = [lane >= 0, lane < 0, lane < LANE]
    ef, eb = ef_ref[...], eb_ref[...]
    r64 = _iota((L, L), 0)
    c64 = _iota((L, L), 1)
    tri_l = (c64 <= r64).astype(bf16)
    tri_u = (c64 >= r64).astype(bf16)

    def run(q_ref, k_ref, v_ref, g_ref, b_ref, o_ref, base):
        n_chunks = q_ref.shape[0] // L

        def prepare(tt, carry):
            groups = []
            for sub in range(PREP_UNROLL):
                t = tt * PREP_UNROLL + sub
                sl_f = pl.ds(pl.multiple_of(t * L, L), L)
                sl_b = pl.ds(pl.multiple_of((n_chunks - 1 - t) * L, L), L)
                gc_f = _dot_lx(tri_l, g_ref[sl_f, :])
                gc_b = _dot_lx(tri_u, g_ref[sl_b, :])
                gb_f = _dot_xl(jnp.concatenate([gc_f, b_ref[sl_f, :]], axis=0), ef)
                gb_b = _dot_xl(jnp.concatenate([gc_b, b_ref[sl_b, :]], axis=0), eb)
                data_f = [r[sl_f, :].astype(f32) for r in (q_ref, k_ref, v_ref)] + [gb_f[:L], gb_f[L:]]
                data_b = [r[sl_b, :].astype(f32) for r in (q_ref, k_ref, v_ref)] + [gb_b[:L], gb_b[L:]]
                groups += [tuple(x[:, :GW] for x in data_f), tuple(x[:, :GW] for x in data_b),
                           tuple(jnp.concatenate([xf[:, GW:], xb[:, GW:]], axis=1) for xf, xb in zip(data_f, data_b))]
            prepared = _gdn_prepare(groups, fwd_lanes * PREP_UNROLL)
            for n, (u, wq, qk, kd, dl) in enumerate(prepared):
                it = base + tt * PREP_UNROLL + n // N_GROUPS
                cols = slice((n % N_GROUPS) * GW, (n % N_GROUPS + 1) * GW)
                u_s[pl.ds(pl.multiple_of(it * L, L), L), cols] = u
                wq_s[pl.ds(pl.multiple_of(it * 2 * L, 2 * L), 2 * L), cols] = wq
                qk_s[pl.ds(pl.multiple_of(it * L, L), L), cols] = qk
                kd_s[pl.ds(pl.multiple_of(it * L, L), L), cols] = kd
                dl_s[pl.ds(pl.multiple_of(it * 8, 8), 8), cols] = jnp.broadcast_to(dl, (8, GW))
            return carry

        lax.fori_loop(0, n_chunks // PREP_UNROLL, prepare, 0)
        o_ref[...] = jnp.zeros(o_ref.shape, f32)

        def scan(t, carry):
            sl_f = pl.ds(pl.multiple_of(t * L, L), L)
            sl_b = pl.ds(pl.multiple_of((n_chunks - 1 - t) * L, L), L)
            it = base + t
            r1 = pl.ds(pl.multiple_of(it * L, L), L)
            r2 = pl.ds(pl.multiple_of(it * 2 * L, 2 * L), 2 * L)
            r8 = pl.ds(pl.multiple_of(it * 8, 8), 1)
            chunks = []
            for i in range(N_GROUPS):
                cols = slice(i * GW, (i + 1) * GW)
                chunks.append((u_s[r1, cols], wq_s[r2, cols], qk_s[r1, cols], kd_s[r1, cols], dl_s[r8, cols]))
            o0, o1, o2 = _gdn_scan(chunks, s_ref)
            o_ref[sl_f, :] += jnp.concatenate([o0, o2[:, :LANE]], axis=1)
            o_ref[sl_b, :] += jnp.concatenate([o1, o2[:, LANE:]], axis=1)
            return carry

        lax.fori_loop(0, n_chunks, scan, 0)

    run(qc, kc, vc, gc, bc, oc_ref, 0)
    run(ql, kl, vl, gl, bl, ol_ref, qc.shape[0] // L)


def _gdn(q, k, v, g, beta, ef, eb, lay):
    b, ctx, seq, nc = lay["b"], lay["ctx"], lay["seq"], lay["nc"]
    cs = lambda w: pl.BlockSpec((ctx, w), lambda i: (i, 0))
    ls = lambda w: pl.BlockSpec((seq, w), lambda i: (nc // seq + i, 0))
    const = lambda shape: pl.BlockSpec(shape, lambda i: (0,) * len(shape))
    widths = (C_W, C_W, C_W, LANE, LANE)
    n_it = (ctx + seq) // C_CHUNK
    return pl.pallas_call(
        _gdn_kernel,
        grid=(b,),
        in_specs=[cs(w) for w in widths] + [ls(w) for w in widths] + [const(ef.shape), const(eb.shape)],
        out_specs=[pl.BlockSpec((ctx, C_W), lambda i: (i, 0)), pl.BlockSpec((seq, C_W), lambda i: (i, 0))],
        out_shape=[jax.ShapeDtypeStruct((nc, C_W), f32), jax.ShapeDtypeStruct((b * seq, C_W), f32)],
        scratch_shapes=[pltpu.VMEM((N_GROUPS, GW, GW), f32),
                        pltpu.VMEM((n_it * C_CHUNK, GDN_W), f32),
                        pltpu.VMEM((n_it * 2 * C_CHUNK, GDN_W), bf16),
                        pltpu.VMEM((n_it * C_CHUNK, GDN_W), bf16),
                        pltpu.VMEM((n_it * C_CHUNK, GDN_W), bf16),
                        pltpu.VMEM((n_it * 8, GDN_W), f32)],
        compiler_params=_cparams(1), name="gated_deltanet",
    )(q, k, v, g, beta, q, k, v, g, beta, ef, eb)


def _outproj_kernel(h_ref, mod_ref, *refs, n_ctx_tiles):
    gate_ref, w_ref, nw_ref, bd64_ref, o_ref = refs[-5:]
    mixed = [r[...] for r in refs[-8:-5]]
    if n_ctx_tiles:
        is_ctx = pl.program_id(0) < n_ctx_tiles
        mixed = [jnp.where(is_ctx, c[...], x) for c, x in zip(refs[:3], mixed)]
    oa, ob, oc = mixed
    bd64 = bd64_ref[...]
    nw = nw_ref[...]

    def normed(x, wrow):
        s = _group_sum(x * x, bd64, terms=1)
        return x * lax.rsqrt(s * (1.0 / C_DK) + EPS) * wrow

    ya = normed(oa, nw[0:1, :A_W]).astype(bf16)
    yc = (normed(oc, nw[1:2, :C_W]) * _silu(gate_ref[...])).astype(bf16)
    y = jnp.dot(ya, w_ref[:A_W, :], preferred_element_type=f32)
    y += jnp.dot(ob, w_ref[A_W:A_W + B_WIDE, :], preferred_element_type=f32)
    y += jnp.dot(yc, w_ref[A_W + B_WIDE:, :], preferred_element_type=f32)
    o_ref[...] = h_ref[...] + mod_ref[0][5:6] * y


def _outproj(h, mod, ctx_outs, lat_outs, gate, w_out, nw, bd128, lay):
    tm, d = lay["tm"], D_MODEL
    nct = lay["nc"] // tm
    off = 0 if ctx_outs else nct
    n_tiles = h.shape[0] // tm - off
    group = functools.partial(_mod_group, lay=lay, off=off)
    rowo = lambda w: pl.BlockSpec((tm, w), lambda i: (i + off, 0))
    rowc = lambda w: pl.BlockSpec((tm, w), lambda i: (jnp.minimum(i, nct - 1), 0))
    rowl = lambda w: pl.BlockSpec((tm, w), lambda i: (jnp.maximum(i + off - nct, 0), 0))
    const = lambda shape: pl.BlockSpec(shape, lambda i: (0,) * len(shape))
    widths = (A_W, B_WIDE, C_W)
    ctx_specs = [rowc(w) for w in widths] if ctx_outs else []
    return pl.pallas_call(
        functools.partial(_outproj_kernel, n_ctx_tiles=nct if ctx_outs else 0),
        grid=(n_tiles,),
        in_specs=[rowo(d), pl.BlockSpec((1, 9, d), lambda i: (group(i), 0, 0))] + ctx_specs
                 + [rowl(w) for w in widths]
                 + [rowo(C_W), pl.BlockSpec((OUT_WIDE, d), lambda i: (0, 0), pipeline_mode=pl.Buffered(1)),
                    const(nw.shape), const(bd128.shape)],
        out_specs=pl.BlockSpec((tm, d), lambda i: (i, 0)),
        out_shape=jax.ShapeDtypeStruct((n_tiles * tm, d), f32),
        compiler_params=_cparams(1), name="mixer_out_proj",
    )(h, mod, *(ctx_outs or ()), *lat_outs, gate, w_out, nw, bd128)


def _block_ones(n, group):
    idx = np.arange(n) // group
    return jnp.asarray(idx[:, None] == idx[None, :], dtype=bf16)


def _rope_table(seq, d, pad_rows):
    half, quarter = d // 2, d // 4
    rows = seq // GRID_W
    row = jnp.repeat(jnp.arange(rows, dtype=f32), GRID_W)
    col = jnp.tile(jnp.arange(GRID_W, dtype=f32), rows)
    inv = ROPE_THETA ** (-jnp.arange(0, half, 2, dtype=f32) / half)
    ld = np.arange(LANE) % d
    pos = jnp.where(jnp.asarray(ld < half)[None, :], row[:, None], col[:, None])
    ang = pos * inv[np.asarray((ld % half) % quarter)][None, :]
    sign = jnp.asarray(np.where((ld % half) < quarter, -1.0, 1.0), dtype=f32)[None, :]
    cos = jnp.concatenate([jnp.ones((pad_rows, LANE), f32), jnp.cos(ang)], axis=0)
    sin = jnp.concatenate([jnp.zeros((pad_rows, LANE), f32), jnp.sin(ang) * sign], axis=0)
    return cos, sin


def _arrange_w_in(w):
    offs = np.concatenate([[0], np.cumsum(IN_SIZES)])
    part = lambda n: w[:, offs[n]:offs[n + 1]]
    zeros = lambda width: jnp.zeros((w.shape[0], width), w.dtype)
    qb = part(3)
    pieces = [part(0), part(1), part(2)]
    for h in range(B_HEADS):
        head = qb[:, h * B_DIM:(h + 1) * B_DIM]
        pieces += [head, zeros(B_DIM)] if h // B_GROUP == 0 else [zeros(B_DIM), head]
    pieces += [part(4), part(5), part(6), part(7), part(8), zeros(LANE - IN_SIZES[8]), part(9), zeros(LANE - IN_SIZES[9])]
    out = jnp.concatenate(pieces, axis=1).astype(bf16)
    assert out.shape[1] == IN_WIDE
    return out


def _arrange_w_out(w):
    zeros = jnp.zeros((B_DIM, w.shape[1]), w.dtype)
    pieces = [w[:A_W]]
    for h in range(B_HEADS):
        head = w[A_W + h * B_DIM:A_W + (h + 1) * B_DIM]
        pieces += [head, zeros] if h // B_GROUP == 0 else [zeros, head]
    pieces.append(w[A_W + B_HEADS * B_DIM:])
    out = jnp.concatenate(pieces, axis=0).astype(bf16)
    assert out.shape[0] == OUT_WIDE
    return out


def _expander(direction):
    m = np.zeros((LANE, C_W), np.float32)
    for h in range(C_HEADS):
        m[direction * C_HEADS + h, h * C_DK:(h + 1) * C_DK] = 1.0
    return jnp.asarray(m, dtype=bf16)


def _pad_lanes(x, width):
    return jnp.pad(x, ((0, 0), (0, width - x.shape[1])))


def kernel(x, c, ctx, c_ctx, w_mod, b_mod, ffn1_w1, ffn1_w2, ffn2_w1, ffn2_w2, w_in, w_out,
           a_qnorm, a_knorm, a_lambda, a_subln, b_qnorm, b_knorm, b_sink,
           c_conv, c_A_log, c_dt_bias, c_onorm):
    b, seq, d = x.shape
    n_ctx = ctx.shape[1]
    nc = b * n_ctx
    tm = 512 if (nc % 512 == 0 and seq % 512 == 0) else 256
    lay = dict(b=b, ctx=n_ctx, seq=seq, nc=nc, tm=tm)
    assert d == D_MODEL and seq % 256 == 0 and n_ctx % 256 == 0 and nc % seq == 0 and seq % GRID_W == 0

    h = (ctx.reshape(nc, d), x.reshape(b * seq, d))
    cvec = jnp.zeros((16, d), f32).at[:b].set(c).at[b].set(c_ctx)
    mod_all = _modulation(cvec, w_mod, b_mod).reshape(DEPTH, 16, 9, d)

    bd32, bd64, bd64s = _block_ones(256, A_DIM), _block_ones(256, B_DIM), _block_ones(LANE, C_DK)
    tabs = _rope_table(seq, A_DIM, tm) + _rope_table(seq, B_DIM, tm)
    ef, eb = _expander(0), _expander(1)

    for l in range(DEPTH):
        last = l == DEPTH - 1
        lam_init = 0.8 - 0.6 * float(np.exp(-0.3 * l))
        mod = mod_all[l]
        lf = a_lambda[l].astype(f32)
        lam = (jnp.exp(jnp.sum(lf[0] * lf[1])) - jnp.exp(jnp.sum(lf[2] * lf[3])) + lam_init).reshape(1)
        nw_in = jnp.stack([jnp.tile(a_qnorm[l], 8) * (A_DIM ** -0.5 * LOG2E),
                           jnp.tile(a_knorm[l], 8),
                           jnp.tile(b_qnorm[l], 4) * (B_DIM ** -0.5 * LOG2E),
                           jnp.tile(b_knorm[l], 4)] + [jnp.zeros((256,), f32)] * 4)
        nw_out = jnp.stack([_pad_lanes((jnp.tile(a_subln[l], 4) * (1.0 - lam_init))[None], C_W)[0],
                            jnp.tile(c_onorm[l], C_HEADS)] + [jnp.zeros((C_W,), f32)] * 6)
        gpar = jnp.stack([_pad_lanes(jnp.exp(c_A_log[l].astype(f32)).reshape(1, -1), LANE)[0],
                          _pad_lanes(c_dt_bias[l].astype(f32).reshape(1, -1), LANE)[0]] + [jnp.zeros((LANE,), f32)] * 6)
        cw = jnp.concatenate([c_conv[l], jnp.zeros((8 - C_CONV, 3 * C_W), f32)], axis=0)

        h = _ffn(h, mod, ffn1_w1[l].astype(bf16), ffn1_w2[l].astype(bf16), 0, lay)
        qa, ka, va, qb, kb, vb, cq, gate, pa, pb = _inproj(h, mod, _arrange_w_in(w_in[l]), nw_in, tabs, bd32, bd64, lay)
        sink = b_sink[l].astype(f32) * LOG2E
        oa = _attn_a(lam, qa, ka, va, lay, True)
        ob = _attn_b(sink, qb, kb, vb, lay, True)
        gq, gk, gv, gg, gbeta = _gdn_prep(cq, pa, pb, cw, gpar, bd64s, lay)
        oc_ctx, oc = _gdn(gq, gk, gv, gg, gbeta, ef, eb, lay)
        ctx_outs = None
        if not last:
            ctx_outs = (_attn_a(lam, qa, ka, va, lay, False), _attn_b(sink, qb, kb, vb, lay, False), oc_ctx)
        h = _outproj(h, mod, ctx_outs, (oa, ob, oc), gate, _arrange_w_out(w_out[l]), nw_out, bd64s, lay)
        if last:
            lay = dict(lay, nc=0)
        h = _ffn(h, mod, ffn2_w1[l].astype(bf16), ffn2_w2[l].astype(bf16), 6, lay)
    return h.reshape(b, seq, d)
```

```python
import functools

import numpy as np
import jax
import jax.numpy as jnp
from jax import lax
from jax.experimental import pallas as pl
from jax.experimental.pallas import tpu as pltpu

f32 = jnp.float32
bf16 = jnp.bfloat16

D_MODEL = 1024
DEPTH = 2
GRID_W = 64
EPS = 1e-6
NEG_INF = -1e30
LOG2E = 1.4426950408889634
ROPE_THETA = 10000.0
D_FF = 2816
A_HEADS, A_DIM, A_VDIM = 4, 32, 64
A_W = A_HEADS * A_VDIM
B_HEADS, B_KV_HEADS, B_DIM = 6, 2, 64
B_GROUP = B_HEADS // B_KV_HEADS
B_BLOCK = 128
C_HEADS, C_DK, C_CONV, C_CHUNK = 6, 64, 5, 64
C_W = C_HEADS * C_DK
IN_SIZES = (256, 256, 256, 384, 128, 128, 1152, 384, 12, 12)
LANE = 128
B_WIDE = B_HEADS * LANE
SEG = dict(qa=(0, 256), ka=(256, 256), va=(512, 256), qb=(768, B_WIDE), kb=(1536, 128), vb=(1664, 128),
           cq=(1792, 1152), gate=(2944, 384), a=(3328, 128), b=(3456, 128))
IN_WIDE = 3584
OUT_WIDE = A_W + B_WIDE + C_W
VMEM_LIMIT = 56 * 1024 * 1024


def _cparams(n_axes):
    return pltpu.CompilerParams(dimension_semantics=("arbitrary",) * n_axes, vmem_limit_bytes=VMEM_LIMIT)


def _dot(a, b):
    return jnp.dot(a.astype(bf16), b.astype(bf16), preferred_element_type=f32)


def _dot_nt(a, b):
    return lax.dot_general(a.astype(bf16), b.astype(bf16), (((1,), (1,)), ((), ())), preferred_element_type=f32)


def _dot_tn(a, b):
    return lax.dot_general(a.astype(bf16), b.astype(bf16), (((0,), (0,)), ((), ())), preferred_element_type=f32)


def _split(x, n):
    parts = []
    for _ in range(n - 1):
        p = x.astype(bf16)
        parts.append(p)
        x = x - p.astype(f32)
    parts.append(x.astype(bf16))
    return parts


def _dot_xl(x, m, n=3):
    return sum(jnp.dot(p, m, preferred_element_type=f32) for p in _split(x, n))


def _dot_lx(m, x, n=3):
    return sum(jnp.dot(m, p, preferred_element_type=f32) for p in _split(x, n))


def _group_sum(xx, bd, terms=2):
    w = bd.shape[0]
    cols = [_dot_xl(xx[:, j:j + w], bd, terms) for j in range(0, xx.shape[1], w)]
    return cols[0] if len(cols) == 1 else jnp.concatenate(cols, axis=1)


def _silu(x):
    return x * jax.nn.sigmoid(x)


def _iota(shape, dim):
    return lax.broadcasted_iota(jnp.int32, shape, dim)


def _modulated_norm(h, shift, scale):
    hn = h * lax.rsqrt(jnp.mean(h * h, axis=-1, keepdims=True) + EPS)
    return hn * (1.0 + scale) + shift


def _mod_kernel(c_ref, w_ref, b_ref, o_ref):
    s = _silu(c_ref[...])
    w = w_ref[0]
    s_hi, s_lo = _split(s, 2)
    w_hi, w_lo = _split(w, 2)
    acc = jnp.dot(s_hi, w_hi, preferred_element_type=f32)
    acc += jnp.dot(s_hi, w_lo, preferred_element_type=f32)
    acc += jnp.dot(s_lo, w_hi, preferred_element_type=f32)
    o_ref[0] = acc + b_ref[0]


def _modulation(cvec, w_mod, b_mod):
    depth, d, n = w_mod.shape
    tn = 1024
    return pl.pallas_call(
        _mod_kernel,
        grid=(depth, n // tn),
        in_specs=[pl.BlockSpec((cvec.shape[0], d), lambda l, j: (0, 0)),
                  pl.BlockSpec((1, d, tn), lambda l, j: (l, 0, j)),
                  pl.BlockSpec((1, 1, tn), lambda l, j: (l, 0, j))],
        out_specs=pl.BlockSpec((1, cvec.shape[0], tn), lambda l, j: (l, 0, j)),
        out_shape=jax.ShapeDtypeStruct((depth, cvec.shape[0], n), f32),
        compiler_params=_cparams(2), name="modulation",
    )(cvec, w_mod, b_mod.reshape(depth, 1, n))


FFN_CHUNKS = 11


def _ffn_kernel(*refs, idx, n_chunks, n_ctx_tiles):
    mod_ref, w1_ref, w2_ref, o_ref = refs[-4:]
    h = refs[-5][...]
    if n_ctx_tiles:
        h = jnp.where(pl.program_id(0) < n_ctx_tiles, refs[0][...], h)
    mod = mod_ref[0]
    hn = _modulated_norm(h, mod[idx:idx + 1], mod[idx + 1:idx + 2]).astype(bf16)
    ck = D_FF // n_chunks
    acc = None
    for c in range(n_chunks):
        g = jnp.dot(hn, w1_ref[:, c * ck:(c + 1) * ck], preferred_element_type=f32)
        u = jnp.dot(hn, w1_ref[:, D_FF + c * ck:D_FF + (c + 1) * ck], preferred_element_type=f32)
        a = (_silu(g) * u).astype(bf16)
        part = jnp.dot(a, w2_ref[c * ck:(c + 1) * ck, :], preferred_element_type=f32)
        acc = part if acc is None else acc + part
    o_ref[...] = h + (0.5 * mod[idx + 2:idx + 3]) * acc


def _ffn(h, mod, w1, w2, idx, lay):
    tm, d = lay["tm"], D_MODEL
    pair = isinstance(h, tuple)
    nct = lay["nc"] // tm
    n_tiles = (h[0].shape[0] + h[1].shape[0] if pair else h.shape[0]) // tm
    group = functools.partial(_mod_group, lay=lay, off=0)
    if pair:
        h_specs = [pl.BlockSpec((tm, d), lambda i: (jnp.minimum(i, nct - 1), 0)),
                   pl.BlockSpec((tm, d), lambda i: (jnp.maximum(i - nct, 0), 0))]
    else:
        h_specs = [pl.BlockSpec((tm, d), lambda i: (i, 0))]
    return pl.pallas_call(
        functools.partial(_ffn_kernel, idx=idx, n_chunks=FFN_CHUNKS, n_ctx_tiles=nct if pair else 0),
        grid=(n_tiles,),
        in_specs=h_specs + [pl.BlockSpec((1, 9, d), lambda i: (group(i), 0, 0)),
                            pl.BlockSpec((d, 2 * D_FF), lambda i: (0, 0), pipeline_mode=pl.Buffered(1)),
                            pl.BlockSpec((D_FF, d), lambda i: (0, 0), pipeline_mode=pl.Buffered(1))],
        out_specs=pl.BlockSpec((tm, d), lambda i: (i, 0)),
        out_shape=jax.ShapeDtypeStruct((n_tiles * tm, d), f32),
        compiler_params=_cparams(1), name=f"ffn_half_step_{idx}",
    )(*(h if pair else (h,)), mod, w1, w2)


def _mod_group(i, lay, off):
    r = (i + off) * lay["tm"]
    return jnp.where(r < lay["nc"], lay["b"], (r - lay["nc"]) // lay["seq"])


def _rope(x, cos, sin, quarter):
    w = x.shape[1]
    reps = w // LANE
    if reps > 1:
        cos = jnp.concatenate([cos] * reps, axis=1)
        sin = jnp.concatenate([sin] * reps, axis=1)
    first = (_iota((1, w), 1) % (2 * quarter)) < quarter
    swapped = jnp.where(first, pltpu.roll(x, w - quarter, 1), pltpu.roll(x, quarter, 1))
    return x * cos + swapped * sin


def _inproj_kernel(h_ref, mod_ref, w_ref, nw_ref, ca_ref, sa_ref, cb_ref, sb_ref, bd32_ref, bd64_ref,
                   qa_o, ka_o, va_o, qb_o, kb_o, vb_o, cq_o, gate_o, a_o, b_o):
    mod = mod_ref[0]
    hn = _modulated_norm(h_ref[...], mod[3:4], mod[4:5]).astype(bf16)

    def proj(name):
        off, width = SEG[name]
        cols = [jnp.dot(hn, w_ref[:, c:min(c + 256, off + width)], preferred_element_type=f32)
                for c in range(off, off + width, 256)]
        return cols[0] if len(cols) == 1 else jnp.concatenate(cols, axis=1)

    def normed(x, bd, group, wrow):
        s = _group_sum(x * x, bd, terms=1)
        return x * lax.rsqrt(s * (1.0 / group) + EPS) * wrow

    nw = nw_ref[...]
    bd32, bd64 = bd32_ref[...], bd64_ref[...]
    ca, sa, cb, sb = ca_ref[...], sa_ref[...], cb_ref[...], sb_ref[...]
    qa_o[...] = _rope(normed(proj("qa"), bd32, A_DIM, nw[0:1, :256]), ca, sa, A_DIM // 4).astype(bf16)
    ka_o[...] = _rope(normed(proj("ka"), bd32, A_DIM, nw[1:2, :256]), ca, sa, A_DIM // 4).astype(bf16)
    va_o[...] = proj("va").astype(bf16)
    qb = proj("qb")
    wq = jnp.concatenate([nw[2:3, :256]] * (B_WIDE // 256), axis=1)
    qb_o[...] = _rope(normed(qb, bd64, B_DIM, wq), cb, sb, B_DIM // 4).astype(bf16)
    kb_o[...] = _rope(normed(proj("kb"), bd64[:LANE, :LANE], B_DIM, nw[3:4, :LANE]), cb, sb, B_DIM // 4).astype(bf16)
    vb_o[...] = proj("vb").astype(bf16)
    cq_o[...] = proj("cq")
    gate_o[...] = proj("gate")
    a_o[...] = proj("a")
    b_o[...] = proj("b")


def _inproj(h, mod, w_in, nw, tabs, bd32, bd64, lay):
    tm, d = lay["tm"], D_MODEL
    n = h.shape[0]
    n_tiles = n // tm
    nct = lay["nc"] // tm
    spt = lay["seq"] // tm
    group = functools.partial(_mod_group, lay=lay, off=0)

    def tab_idx(i):
        return jnp.where(i < nct, 0, 1 + (i - nct) % spt)

    row = lambda w: pl.BlockSpec((tm, w), lambda i: (i, 0))
    const = lambda shape: pl.BlockSpec(shape, lambda i: (0,) * len(shape))
    tab = pl.BlockSpec((tm, LANE), lambda i: (tab_idx(i), 0))
    names = ("qa", "ka", "va", "qb", "kb", "vb", "cq", "gate", "a", "b")
    dts = (bf16,) * 6 + (f32,) * 4
    return pl.pallas_call(
        _inproj_kernel,
        grid=(n_tiles,),
        in_specs=[row(d), pl.BlockSpec((1, 9, d), lambda i: (group(i), 0, 0)),
                  pl.BlockSpec((d, IN_WIDE), lambda i: (0, 0), pipeline_mode=pl.Buffered(1)),
                  const(nw.shape), tab, tab, tab, tab, const(bd32.shape), const(bd64.shape)],
        out_specs=[row(SEG[k][1]) for k in names],
        out_shape=[jax.ShapeDtypeStruct((n, SEG[k][1]), dt) for k, dt in zip(names, dts)],
        compiler_params=_cparams(1), name="mixer_in_proj",
    )(h, mod, w_in, nw, *tabs, bd32, bd64)


A_KBLOCK = 256
A_INTERLEAVE = 8


def _attn_a_kernel(lam_ref, q_ref, *refs, n_seg):
    k_refs, v_refs = refs[:n_seg], refs[n_seg:2 * n_seg]
    o_ref, vt_ref = refs[2 * n_seg], refs[2 * n_seg + 1]
    tq = q_ref.shape[0]
    n_maps = 2 * A_HEADS

    @pl.when(pl.program_id(1) == 0)
    def _():
        off = 0
        for v_ref in v_refs:
            for r in range(0, v_ref.shape[0], A_KBLOCK):
                vt_ref[:, off + r:off + r + A_KBLOCK] = v_ref[r:r + A_KBLOCK, :].astype(f32).T.astype(bf16)
            off += v_ref.shape[0]

    lam = lam_ref[0]
    qt = q_ref[...].astype(f32).T
    feat = _iota((A_W, tq), 0)
    qms = [jnp.where(feat // A_DIM == n, qt, 0.0).astype(bf16) for n in range(n_maps)]
    normed = [None] * n_maps
    for g0 in range(0, n_maps, A_INTERLEAVE):
        grp = range(g0, g0 + A_INTERLEAVE)
        m_run = {n: jnp.full((1, tq), NEG_INF, f32) for n in grp}
        l_run = {n: jnp.zeros((1, tq), f32) for n in grp}
        acc = {n: jnp.zeros((A_VDIM, tq), f32) for n in grp}
        off = 0
        for k_ref in k_refs:
            for r in range(0, k_ref.shape[0], A_KBLOCK):
                kb = k_ref[r:r + A_KBLOCK, :]
                st = {n: jnp.dot(kb, qms[n], preferred_element_type=f32) for n in grp}
                m_new = {n: jnp.maximum(m_run[n], jnp.max(st[n], axis=0, keepdims=True)) for n in grp}
                alpha = {n: jnp.exp2(m_run[n] - m_new[n]) for n in grp}
                e = {n: jnp.exp2(st[n] - m_new[n]) for n in grp}
                l_run = {n: alpha[n] * l_run[n] + jnp.sum(e[n], axis=0, keepdims=True) for n in grp}
                pv = {n: jnp.dot(vt_ref[(n // 2) * A_VDIM:(n // 2 + 1) * A_VDIM, off + r:off + r + A_KBLOCK],
                                 e[n].astype(bf16), preferred_element_type=f32) for n in grp}
                acc = {n: alpha[n] * acc[n] + pv[n] for n in grp}
                m_run = m_new
            off += k_ref.shape[0]
        for n in grp:
            normed[n] = acc[n] * ((lam if n % 2 else 1.0) / l_run[n])
    heads = [normed[2 * h] - normed[2 * h + 1] for h in range(A_HEADS)]
    o_ref[...] = jnp.concatenate(heads, axis=0).T


def _attn_a(lam, qa, ka, va, lay, latent):
    b, ctx, seq, nc = lay["b"], lay["ctx"], lay["seq"], lay["nc"]
    smem = pl.BlockSpec(memory_space=pltpu.SMEM)
    if latent:
        tq = 256
        qpb = seq // tq
        grid = (b, qpb)
        qspec = pl.BlockSpec((tq, A_W), lambda i, j: (nc // tq + i * qpb + j, 0))
        kv = [pl.BlockSpec((ctx, A_W), lambda i, j: (i, 0)), pl.BlockSpec((seq, A_W), lambda i, j: (nc // seq + i, 0))]
        ospec = pl.BlockSpec((tq, A_W), lambda i, j: (i * qpb + j, 0))
        rows = b * seq
    else:
        grid = (b, 1)
        qspec = pl.BlockSpec((ctx, A_W), lambda i, j: (i, 0))
        kv = [pl.BlockSpec((ctx, A_W), lambda i, j: (i, 0))]
        ospec = qspec
        rows = nc
    n_seg = len(kv)
    return pl.pallas_call(
        functools.partial(_attn_a_kernel, n_seg=n_seg),
        grid=grid,
        in_specs=[smem, qspec] + kv + kv,
        out_specs=ospec,
        out_shape=jax.ShapeDtypeStruct((rows, A_W), f32),
        scratch_shapes=[pltpu.VMEM((A_W, ctx + seq if latent else ctx), bf16)],
        compiler_params=_cparams(2), name="diff_attention_lat" if latent else "diff_attention_ctx",
    )(lam, qa, *([ka] * n_seg), *([va] * n_seg))


B_QSUB = 4


def _attn_b_kernel(sink_ref, q_ref, kc_ref, vc_ref, *refs, latent):
    o_ref = refs[-1]
    lane = _iota((1, LANE), 1)
    n_ctx = kc_ref.shape[0]
    kc, vc = kc_ref[...], vc_ref[...]
    rows = B_BLOCK if latent else q_ref.shape[0]
    n_sub = q_ref.shape[0] // rows
    for sub in range(n_sub):
        q = q_ref[sub * rows:(sub + 1) * rows, :]
        keys, vals, band_ok = kc, vc, None
        if latent:
            kl_ref, vl_ref = refs[0], refs[1]
            t = pl.program_id(1) * n_sub + sub
            nb = pl.num_programs(1) * n_sub
            starts = [pl.multiple_of(jnp.clip(t + off, 0, nb - 1) * B_BLOCK, B_BLOCK) for off in (-1, 0, 1)]
            keys = jnp.concatenate([kc] + [kl_ref[pl.ds(st, B_BLOCK), :] for st in starts], axis=0)
            vals = jnp.concatenate([vc] + [vl_ref[pl.ds(st, B_BLOCK), :] for st in starts], axis=0)
            r = _iota((B_GROUP * rows, 3 * B_BLOCK), 0) % rows
            c = _iota((B_GROUP * rows, 3 * B_BLOCK), 1)
            cc = c % B_BLOCK
            before = jnp.logical_and(jnp.logical_and(c < B_BLOCK, cc >= r), t >= 1)
            after = jnp.logical_and(jnp.logical_and(c >= 2 * B_BLOCK, cc <= r), t + 1 < nb)
            same = jnp.logical_and(c >= B_BLOCK, c < 2 * B_BLOCK)
            band_ok = jnp.logical_or(same, jnp.logical_or(before, after))
        for g in range(B_KV_HEADS):
            heads = range(g * B_GROUP, (g + 1) * B_GROUP)
            qg = jnp.concatenate([q[:, h * LANE:(h + 1) * LANE] for h in heads], axis=0)
            sink = jnp.concatenate([jnp.full((rows, 1), sink_ref[h], f32) for h in heads], axis=0)
            x = _dot_nt(qg, keys)
            s = [x] if band_ok is None else [x[:, :n_ctx], jnp.where(band_ok, x[:, n_ctx:], NEG_INF)]
            mx = functools.reduce(jnp.maximum, [jnp.max(x, axis=-1, keepdims=True) for x in s] + [sink])
            e = [jnp.exp2(x - mx) for x in s]
            den = functools.reduce(jnp.add, [jnp.sum(x, axis=-1, keepdims=True) for x in e]) + jnp.exp2(sink - mx)
            p = e[0] if len(e) == 1 else jnp.concatenate(e, axis=1)
            og = jnp.dot(p.astype(bf16), vals, preferred_element_type=f32) * (1.0 / den)
            og = jnp.where(lane // B_DIM == g, og, 0.0).astype(bf16)
            for n, h in enumerate(heads):
                o_ref[sub * rows:(sub + 1) * rows, h * LANE:(h + 1) * LANE] = og[n * rows:(n + 1) * rows]


def _attn_b(sink, qb, kb, vb, lay, latent):
    b, ctx, seq, nc = lay["b"], lay["ctx"], lay["seq"], lay["nc"]
    smem = pl.BlockSpec(memory_space=pltpu.SMEM)
    cspec = pl.BlockSpec((ctx, LANE), lambda i, j: (i, 0))
    if latent:
        tq = B_QSUB * B_BLOCK
        nb = seq // tq
        grid = (b, nb)
        qspec = pl.BlockSpec((tq, B_WIDE), lambda i, j: (nc // tq + i * nb + j, 0))
        lspec = pl.BlockSpec((seq, LANE), lambda i, j: (nc // seq + i, 0))
        in_specs = [smem, qspec, cspec, cspec, lspec, lspec]
        args = (sink, qb, kb, vb, kb, vb)
        ospec = pl.BlockSpec((tq, B_WIDE), lambda i, j: (i * nb + j, 0))
        rows = b * seq
    else:
        grid = (b, 1)
        qspec = pl.BlockSpec((ctx, B_WIDE), lambda i, j: (i, 0))
        in_specs = [smem, qspec, cspec, cspec]
        args = (sink, qb, kb, vb)
        ospec = qspec
        rows = nc
    return pl.pallas_call(
        functools.partial(_attn_b_kernel, latent=latent),
        grid=grid, in_specs=in_specs, out_specs=ospec,
        out_shape=jax.ShapeDtypeStruct((rows, B_WIDE), bf16),
        compiler_params=_cparams(2), name="window_attention_lat" if latent else "sink_attention_ctx",
    )(*args)


HALO = 8


def _gdn_prep_kernel(x_ref, prev_ref, next_ref, a_ref, b_ref, cw_ref, par_ref, bd64_ref,
                     q_o, k_o, v_o, g_o, beta_o, *, nct, cpt, spt):
    i = pl.program_id(0)
    tm = x_ref.shape[0]
    j = jnp.where(i < nct, i % cpt, (i - nct) % spt)
    per_seq = jnp.where(i < nct, cpt, spt)
    first = j == 0
    last = j == per_seq - 1
    prev = jnp.where(first, 0.0, prev_ref[...])
    nxt = jnp.where(last, 0.0, next_ref[...])
    xx = jnp.concatenate([prev, x_ref[...], nxt], axis=0)
    cw = cw_ref[...]
    rows = xx.shape[0]
    y = None
    for tap in range(C_CONV):
        shift = (C_CONV // 2 - tap) % rows
        sh = xx if shift == 0 else pltpu.roll(xx, shift, 0)
        term = sh[HALO:HALO + tm] * cw[tap:tap + 1]
        y = term if y is None else y + term
    y = _silu(y)
    bd64 = bd64_ref[...]

    def l2n(t):
        return t * lax.rsqrt(_group_sum(t * t, bd64, terms=1) + EPS)

    q_o[...] = (l2n(y[:, :C_W]) * (C_DK ** -0.5)).astype(bf16)
    k_o[...] = l2n(y[:, C_W:2 * C_W]).astype(bf16)
    v_o[...] = y[:, 2 * C_W:].astype(bf16)
    par = par_ref[...]
    z = a_ref[...] + par[1:2]
    softplus = jnp.maximum(z, 0.0) + jnp.log1p(jnp.exp(-jnp.abs(z)))
    g_o[...] = -par[0:1] * softplus
    beta_o[...] = jax.nn.sigmoid(b_ref[...])


def _gdn_prep(cq, a, bb, cw, par, bd64, lay):
    tm = 256
    n = cq.shape[0]
    n_tiles = n // tm
    nct, spt = lay["nc"] // tm, lay["seq"] // tm
    hb = tm // HALO
    last_blk = n // HALO - 1
    row = lambda w: pl.BlockSpec((tm, w), lambda i: (i, 0))
    const = lambda shape: pl.BlockSpec(shape, lambda i: (0,) * len(shape))
    return pl.pallas_call(
        functools.partial(_gdn_prep_kernel, nct=nct, cpt=lay["ctx"] // tm, spt=spt),
        grid=(n_tiles,),
        in_specs=[row(3 * C_W),
                  pl.BlockSpec((HALO, 3 * C_W), lambda i: (jnp.maximum(i * hb - 1, 0), 0)),
                  pl.BlockSpec((HALO, 3 * C_W), lambda i: (jnp.minimum((i + 1) * hb, last_blk), 0)),
                  row(LANE), row(LANE), const(cw.shape), const(par.shape), const(bd64.shape)],
        out_specs=[row(C_W), row(C_W), row(C_W), row(LANE), row(LANE)],
        out_shape=[jax.ShapeDtypeStruct((n, C_W), bf16)] * 3 + [jax.ShapeDtypeStruct((n, LANE), f32)] * 2,
        compiler_params=_cparams(1), name="gdn_inputs",
    )(cq, cq, cq, a, bb, cw, par, bd64)


GW = 4 * C_DK


def _bd(x):
    t = jnp.concatenate([x.astype(bf16)] * 4, axis=0)
    same = _iota((GW, GW), 0) // C_DK == _iota((GW, GW), 1) // C_DK
    return jnp.where(same, t, jnp.zeros_like(t))


N_GROUPS = 3
GDN_W = N_GROUPS * GW
PREP_UNROLL = 4


def _per_head_lanes(x, first):
    rows = x.shape[0]
    left = _iota((1, LANE), 1) < C_DK
    cols = []
    for pair in range(C_HEADS // 2):
        a = jnp.broadcast_to(x[:, first + 2 * pair:first + 2 * pair + 1], (rows, LANE))
        b = jnp.broadcast_to(x[:, first + 2 * pair + 1:first + 2 * pair + 2], (rows, LANE))
        cols.append(jnp.where(left, a, b))
    return jnp.concatenate(cols, axis=1)


def _per_head_rows(x, first):
    rows = x.shape[0]
    xt = jnp.concatenate([x, jnp.zeros_like(x)], axis=0).T
    xt = xt + pltpu.roll(xt, C_DK, 1)
    left = _iota((1, LANE), 1) < C_DK
    cols = []
    for pair in range(C_HEADS // 2):
        a = jnp.broadcast_to(xt[first + 2 * pair:first + 2 * pair + 1, :], (rows, LANE))
        b = jnp.broadcast_to(xt[first + 2 * pair + 1:first + 2 * pair + 2, :], (rows, LANE))
        cols.append(jnp.where(left, a, b))
    return jnp.concatenate(cols, axis=1)


def _gdn_prepare(groups, fwd_lanes):
    L = C_CHUNK
    n = range(len(groups))
    ri = _iota((L, GW), 0)
    ci = _iota((L, GW), 1) % C_DK
    eye = ci == ri
    ahead = [jnp.where(f, ri - ci, ci - ri) for f in fwd_lanes]
    qs, ks, vs, gcs, betas, gr = zip(*groups)
    kq =[_dot_nt(jnp.concatenate([ks[i], qs[i]], axis=0), _bd(ks[i])) for i in n]
    dec = [jnp.exp(jnp.where(ahead[i] >= 0, gcs[i] - gr[i], NEG_INF)) for i in n]
    a = [jnp.where(ahead[i] > 0, betas[i] * kq[i][:L] * dec[i], 0.0) for i in n]
    tm = [jnp.where(eye, 1.0, 0.0) - jnp.where(ri // 2 == ci // 2, a[i], 0.0) for i in n]
    s = 2
    while s < L:
        off = jnp.logical_and(ri // (2 * s) == ci // (2 * s), ri // s != ci // s)
        y = [_dot(jnp.where(off, a[i], 0.0), _bd(tm[i])) for i in n]
        tm = [tm[i] - _dot(tm[i], _bd(y[i])) for i in n]
        s *= 2
    eg = [jnp.exp(gcs[i]) for i in n]
    uw = [_dot(tm[i], jnp.concatenate([_bd(betas[i] * vs[i]), _bd(betas[i] * eg[i] * ks[i])], axis=1)) for i in n]
    out = []
    for i in n:
        g_last = jnp.where(fwd_lanes[i], gcs[i][L - 1:L], gcs[i][0:1])
        wq = jnp.concatenate([uw[i][:, GW:], qs[i] * eg[i]], axis=0).astype(bf16)
        qk = jnp.where(ahead[i] >= 0, kq[i][L:] * dec[i], 0.0).astype(bf16)
        kd = (ks[i] * jnp.exp(g_last - gcs[i])).astype(bf16)
        out.append((uw[i][:, :GW], wq, qk, kd, jnp.exp(g_last)))
    return out


def _gdn_scan(chunks, s_ref):
    L = C_CHUNK
    n = range(len(chunks))
    same = _iota((GW, GW), 0) // C_DK == _iota((GW, GW), 1) // C_DK
    s = [s_ref[i] for i in n]
    ws = [_dot(chunks[i][1], s[i]) for i in n]
    v_new = [chunks[i][0] - ws[i][:L] for i in n]
    o = [ws[i][L:] + _dot(chunks[i][2], _bd(v_new[i])) for i in n]
    upd = [_dot_tn(chunks[i][3], v_new[i]) for i in n]
    for i in n:
        s_ref[i] = s[i] * chunks[i][4] + jnp.where(same, upd[i], 0.0)
    return o


def _gdn_kernel(qc, kc, vc, gc, bc, ql, kl, vl, gl, bl, oc_ref, ol_ref,
                s_ref, u_s, wq_s, qk_s, kd_s, dl_s):
    L = C_CHUNK
    s_ref[...] = jnp.zeros(s_ref.shape, f32)
    lane = _iota((1, GW), 1)
    fwd_lanes = [lane >= 0, lane < 0, lane < LANE]
    r64 = _iota((L, L), 0)
    c64 = _iota((L, L), 1)
    tri_l = (c64 <= r64).astype(bf16)
    tri_u = (c64 >= r64).astype(bf16)

    def run(q_ref, k_ref, v_ref, g_ref, b_ref, o_ref, base):
        n_chunks = q_ref.shape[0] // L
        unroll = min(PREP_UNROLL, n_chunks)

        def prepare(tt, carry):
            groups = []
            for sub in range(unroll):
                t = tt * unroll + sub
                sl_f = pl.ds(pl.multiple_of(t * L, L), L)
                sl_b = pl.ds(pl.multiple_of((n_chunks - 1 - t) * L, L), L)
                gc_f = _dot_lx(tri_l, g_ref[sl_f, :])
                gc_b = _dot_lx(tri_u, g_ref[sl_b, :])
                data_f = ([r[sl_f, :].astype(f32) for r in (q_ref, k_ref, v_ref)]
                          + [_per_head_lanes(gc_f, 0), _per_head_lanes(b_ref[sl_f, :], 0), _per_head_rows(gc_f, 0)])
                data_b = ([r[sl_b, :].astype(f32) for r in (q_ref, k_ref, v_ref)]
                          + [_per_head_lanes(gc_b, C_HEADS), _per_head_lanes(b_ref[sl_b, :], C_HEADS),
                             _per_head_rows(gc_b, C_HEADS)])
                groups += [tuple(x[:, :GW] for x in data_f), tuple(x[:, :GW] for x in data_b),
                           tuple(jnp.concatenate([xf[:, GW:], xb[:, GW:]], axis=1) for xf, xb in zip(data_f, data_b))]
            prepared = _gdn_prepare(groups, fwd_lanes * unroll)
            for n, (u, wq, qk, kd, dl) in enumerate(prepared):
                it = base + tt * unroll + n // N_GROUPS
                cols = slice((n % N_GROUPS) * GW, (n % N_GROUPS + 1) * GW)
                u_s[pl.ds(pl.multiple_of(it * L, L), L), cols] = u
                wq_s[pl.ds(pl.multiple_of(it * 2 * L, 2 * L), 2 * L), cols] = wq
                qk_s[pl.ds(pl.multiple_of(it * L, L), L), cols] = qk
                kd_s[pl.ds(pl.multiple_of(it * L, L), L), cols] = kd
                dl_s[pl.ds(pl.multiple_of(it * 8, 8), 8), cols] = jnp.broadcast_to(dl, (8, GW))
            return carry

        lax.fori_loop(0, n_chunks // unroll, prepare, 0)
        o_ref[...] = jnp.zeros(o_ref.shape, f32)

        def scan(t, carry):
            sl_f = pl.ds(pl.multiple_of(t * L, L), L)
            sl_b = pl.ds(pl.multiple_of((n_chunks - 1 - t) * L, L), L)
            it = base + t
            r1 = pl.ds(pl.multiple_of(it * L, L), L)
            r2 = pl.ds(pl.multiple_of(it * 2 * L, 2 * L), 2 * L)
            r8 = pl.ds(pl.multiple_of(it * 8, 8), 1)
            chunks = []
            for i in range(N_GROUPS):
                cols = slice(i * GW, (i + 1) * GW)
                chunks.append((u_s[r1, cols], wq_s[r2, cols], qk_s[r1, cols], kd_s[r1, cols], dl_s[r8, cols]))
            o0, o1, o2 = _gdn_scan(chunks, s_ref)
            o_ref[sl_f, :] += jnp.concatenate([o0, o2[:, :LANE]], axis=1)
            o_ref[sl_b, :] += jnp.concatenate([o1, o2[:, LANE:]], axis=1)
            return carry

        lax.fori_loop(0, n_chunks, scan, 0)

    run(qc, kc, vc, gc, bc, oc_ref, 0)
    run(ql, kl, vl, gl, bl, ol_ref, qc.shape[0] // L)


def _gdn(q, k, v, g, beta, lay):
    b, ctx, seq, nc = lay["b"], lay["ctx"], lay["seq"], lay["nc"]
    cs = lambda w: pl.BlockSpec((ctx, w), lambda i: (i, 0))
    ls = lambda w: pl.BlockSpec((seq, w), lambda i: (nc // seq + i, 0))
    widths = (C_W, C_W, C_W, LANE, LANE)
    n_it = (ctx + seq) // C_CHUNK
    return pl.pallas_call(
        _gdn_kernel,
        grid=(b,),
        in_specs=[cs(w) for w in widths] + [ls(w) for w in widths],
        out_specs=[pl.BlockSpec((ctx, C_W), lambda i: (i, 0)), pl.BlockSpec((seq, C_W), lambda i: (i, 0))],
        out_shape=[jax.ShapeDtypeStruct((nc, C_W), f32), jax.ShapeDtypeStruct((b * seq, C_W), f32)],
        scratch_shapes=[pltpu.VMEM((N_GROUPS, GW, GW), f32),
                        pltpu.VMEM((n_it * C_CHUNK, GDN_W), f32),
                        pltpu.VMEM((n_it * 2 * C_CHUNK, GDN_W), bf16),
                        pltpu.VMEM((n_it * C_CHUNK, GDN_W), bf16),
                        pltpu.VMEM((n_it * C_CHUNK, GDN_W), bf16),
                        pltpu.VMEM((n_it * 8, GDN_W), f32)],
        compiler_params=_cparams(1), name="gated_deltanet",
    )(q, k, v, g, beta, q, k, v, g, beta)


def _outproj_kernel(h_ref, mod_ref, *refs, n_ctx_tiles):
    gate_ref, w_ref, nw_ref, bd64_ref, o_ref = refs[-5:]
    mixed = [r[...] for r in refs[-8:-5]]
    if n_ctx_tiles:
        is_ctx = pl.program_id(0) < n_ctx_tiles
        mixed = [jnp.where(is_ctx, c[...], x) for c, x in zip(refs[:3], mixed)]
    oa, ob, oc = mixed
    bd64 = bd64_ref[...]
    nw = nw_ref[...]

    def normed(x, wrow):
        s = _group_sum(x * x, bd64, terms=1)
        return x * lax.rsqrt(s * (1.0 / C_DK) + EPS) * wrow

    ya = normed(oa, nw[0:1, :A_W]).astype(bf16)
    yc = (normed(oc, nw[1:2, :C_W]) * _silu(gate_ref[...])).astype(bf16)
    y = jnp.dot(ya, w_ref[:A_W, :], preferred_element_type=f32)
    y += jnp.dot(ob, w_ref[A_W:A_W + B_WIDE, :], preferred_element_type=f32)
    y += jnp.dot(yc, w_ref[A_W + B_WIDE:, :], preferred_element_type=f32)
    o_ref[...] = h_ref[...] + mod_ref[0][5:6] * y


def _outproj(h, mod, ctx_outs, lat_outs, gate, w_out, nw, bd128, lay):
    tm, d = lay["tm"], D_MODEL
    nct = lay["nc"] // tm
    off = 0 if ctx_outs else nct
    n_tiles = h.shape[0] // tm - off
    group = functools.partial(_mod_group, lay=lay, off=off)
    rowo = lambda w: pl.BlockSpec((tm, w), lambda i: (i + off, 0))
    rowc = lambda w: pl.BlockSpec((tm, w), lambda i: (jnp.minimum(i, nct - 1), 0))
    rowl = lambda w: pl.BlockSpec((tm, w), lambda i: (jnp.maximum(i + off - nct, 0), 0))
    const = lambda shape: pl.BlockSpec(shape, lambda i: (0,) * len(shape))
    widths = (A_W, B_WIDE, C_W)
    ctx_specs = [rowc(w) for w in widths] if ctx_outs else []
    return pl.pallas_call(
        functools.partial(_outproj_kernel, n_ctx_tiles=nct if ctx_outs else 0),
        grid=(n_tiles,),
        in_specs=[rowo(d), pl.BlockSpec((1, 9, d), lambda i: (group(i), 0, 0))] + ctx_specs
                 + [rowl(w) for w in widths]
                 + [rowo(C_W), pl.BlockSpec((OUT_WIDE, d), lambda i: (0, 0), pipeline_mode=pl.Buffered(1)),
                    const(nw.shape), const(bd128.shape)],
        out_specs=pl.BlockSpec((tm, d), lambda i: (i, 0)),
        out_shape=jax.ShapeDtypeStruct((n_tiles * tm, d), f32),
        compiler_params=_cparams(1), name="mixer_out_proj",
    )(h, mod, *(ctx_outs or ()), *lat_outs, gate, w_out, nw, bd128)


def _block_ones(n, group):
    idx = np.arange(n) // group
    return jnp.asarray(idx[:, None] == idx[None, :], dtype=bf16)


def _rope_table(seq, d, pad_rows):
    half, quarter = d // 2, d // 4
    rows = seq // GRID_W
    row = jnp.repeat(jnp.arange(rows, dtype=f32), GRID_W)
    col = jnp.tile(jnp.arange(GRID_W, dtype=f32), rows)
    inv = ROPE_THETA ** (-jnp.arange(0, half, 2, dtype=f32) / half)
    ld = np.arange(LANE) % d
    pos = jnp.where(jnp.asarray(ld < half)[None, :], row[:, None], col[:, None])
    ang = pos * inv[np.asarray((ld % half) % quarter)][None, :]
    sign = jnp.asarray(np.where((ld % half) < quarter, -1.0, 1.0), dtype=f32)[None, :]
    cos = jnp.concatenate([jnp.ones((pad_rows, LANE), f32), jnp.cos(ang)], axis=0)
    sin = jnp.concatenate([jnp.zeros((pad_rows, LANE), f32), jnp.sin(ang) * sign], axis=0)
    return cos, sin


def _arrange_w_in(w):
    offs = np.concatenate([[0], np.cumsum(IN_SIZES)])
    part = lambda n: w[:, offs[n]:offs[n + 1]]
    zeros = lambda width: jnp.zeros((w.shape[0], width), w.dtype)
    qb = part(3)
    pieces = [part(0), part(1), part(2)]
    for h in range(B_HEADS):
        head = qb[:, h * B_DIM:(h + 1) * B_DIM]
        pieces += [head, zeros(B_DIM)] if h // B_GROUP == 0 else [zeros(B_DIM), head]
    pieces += [part(4), part(5), part(6), part(7), part(8), zeros(LANE - IN_SIZES[8]), part(9), zeros(LANE - IN_SIZES[9])]
    out = jnp.concatenate(pieces, axis=1).astype(bf16)
    assert out.shape[1] == IN_WIDE
    return out


def _arrange_w_out(w):
    zeros = jnp.zeros((B_DIM, w.shape[1]), w.dtype)
    pieces = [w[:A_W]]
    for h in range(B_HEADS):
        head = w[A_W + h * B_DIM:A_W + (h + 1) * B_DIM]
        pieces += [head, zeros] if h // B_GROUP == 0 else [zeros, head]
    pieces.append(w[A_W + B_HEADS * B_DIM:])
    out = jnp.concatenate(pieces, axis=0).astype(bf16)
    assert out.shape[0] == OUT_WIDE
    return out


def _pad_lanes(x, width):
    return jnp.pad(x, ((0, 0), (0, width - x.shape[1])))


def kernel(x, c, ctx, c_ctx, w_mod, b_mod, ffn1_w1, ffn1_w2, ffn2_w1, ffn2_w2, w_in, w_out,
           a_qnorm, a_knorm, a_lambda, a_subln, b_qnorm, b_knorm, b_sink,
           c_conv, c_A_log, c_dt_bias, c_onorm):
    b, seq, d = x.shape
    n_ctx = ctx.shape[1]
    nc = b * n_ctx
    tm = 512 if (nc % 512 == 0 and seq % 512 == 0) else 256
    lay = dict(b=b, ctx=n_ctx, seq=seq, nc=nc, tm=tm)
    assert d == D_MODEL and seq % 256 == 0 and n_ctx % 256 == 0 and nc % seq == 0 and seq % GRID_W == 0

    h = (ctx.reshape(nc, d), x.reshape(b * seq, d))
    cvec = jnp.zeros((16, d), f32).at[:b].set(c).at[b].set(c_ctx)
    mod_all = _modulation(cvec, w_mod, b_mod).reshape(DEPTH, 16, 9, d)

    bd32, bd64, bd64s = _block_ones(256, A_DIM), _block_ones(256, B_DIM), _block_ones(LANE, C_DK)
    tabs = _rope_table(seq, A_DIM, tm) + _rope_table(seq, B_DIM, tm)

    for l in range(DEPTH):
        last = l == DEPTH - 1
        lam_init = 0.8 - 0.6 * float(np.exp(-0.3 * l))
        mod = mod_all[l]
        lf = a_lambda[l].astype(f32)
        lam = (jnp.exp(jnp.sum(lf[0] * lf[1])) - jnp.exp(jnp.sum(lf[2] * lf[3])) + lam_init).reshape(1)
        nw_in = jnp.stack([jnp.tile(a_qnorm[l], 8) * (A_DIM ** -0.5 * LOG2E),
                           jnp.tile(a_knorm[l], 8),
                           jnp.tile(b_qnorm[l], 4) * (B_DIM ** -0.5 * LOG2E),
                           jnp.tile(b_knorm[l], 4)] + [jnp.zeros((256,), f32)] * 4)
        nw_out = jnp.stack([_pad_lanes((jnp.tile(a_subln[l], 4) * (1.0 - lam_init))[None], C_W)[0],
                            jnp.tile(c_onorm[l], C_HEADS)] + [jnp.zeros((C_W,), f32)] * 6)
        gpar = jnp.stack([_pad_lanes(jnp.exp(c_A_log[l].astype(f32)).reshape(1, -1), LANE)[0],
                          _pad_lanes(c_dt_bias[l].astype(f32).reshape(1, -1), LANE)[0]] + [jnp.zeros((LANE,), f32)] * 6)
        cw = jnp.concatenate([c_conv[l], jnp.zeros((8 - C_CONV, 3 * C_W), f32)], axis=0)

        h = _ffn(h, mod, ffn1_w1[l].astype(bf16), ffn1_w2[l].astype(bf16), 0, lay)
        qa, ka, va, qb, kb, vb, cq, gate, pa, pb = _inproj(h, mod, _arrange_w_in(w_in[l]), nw_in, tabs, bd32, bd64, lay)
        sink = b_sink[l].astype(f32) * LOG2E
        oa = _attn_a(lam, qa, ka, va, lay, True)
        ob = _attn_b(sink, qb, kb, vb, lay, True)
        gq, gk, gv, gg, gbeta = _gdn_prep(cq, pa, pb, cw, gpar, bd64s, lay)
        oc_ctx, oc = _gdn(gq, gk, gv, gg, gbeta, lay)
        ctx_outs = None
        if not last:
            ctx_outs = (_attn_a(lam, qa, ka, va, lay, False), _attn_b(sink, qb, kb, vb, lay, False), oc_ctx)
        h = _outproj(h, mod, ctx_outs, (oa, ob, oc), gate, _arrange_w_out(w_out[l]), nw_out, bd64s, lay)
        if last:
            lay = dict(lay, nc=0)
        h = _ffn(h, mod, ffn2_w1[l].astype(bf16), ffn2_w2[l].astype(bf16), 6, lay)
    return h.reshape(b, seq, d)
```

```python
import functools

import numpy as np
import jax
import jax.numpy as jnp
from jax import lax
from jax.experimental import pallas as pl
from jax.experimental.pallas import tpu as pltpu

f32 = jnp.float32
bf16 = jnp.bfloat16

D_MODEL = 1024
DEPTH = 2
GRID_W = 64
EPS = 1e-6
NEG_INF = -1e30
LOG2E = 1.4426950408889634
ROPE_THETA = 10000.0
D_FF = 2816
A_HEADS, A_DIM, A_VDIM = 4, 32, 64
A_W = A_HEADS * A_VDIM
B_HEADS, B_KV_HEADS, B_DIM = 6, 2, 64
B_GROUP = B_HEADS // B_KV_HEADS
B_BLOCK = 128
C_HEADS, C_DK, C_CONV, C_CHUNK = 6, 64, 5, 64
C_W = C_HEADS * C_DK
IN_SIZES = (256, 256, 256, 384, 128, 128, 1152, 384, 12, 12)
LANE = 128
B_WIDE = B_HEADS * LANE
SEG = dict(qa=(0, 256), ka=(256, 256), va=(512, 256), qb=(768, B_WIDE), kb=(1536, 128), vb=(1664, 128),
           cq=(1792, 1152), gate=(2944, 384), a=(3328, 128), b=(3456, 128))
IN_WIDE = 3584
OUT_WIDE = A_W + B_WIDE + C_W
VMEM_LIMIT = 56 * 1024 * 1024


def _cparams(n_axes):
    return pltpu.CompilerParams(dimension_semantics=("arbitrary",) * n_axes, vmem_limit_bytes=VMEM_LIMIT)


def _dot(a, b):
    return jnp.dot(a.astype(bf16), b.astype(bf16), preferred_element_type=f32)


def _dot_nt(a, b):
    return lax.dot_general(a.astype(bf16), b.astype(bf16), (((1,), (1,)), ((), ())), preferred_element_type=f32)


def _dot_tn(a, b):
    return lax.dot_general(a.astype(bf16), b.astype(bf16), (((0,), (0,)), ((), ())), preferred_element_type=f32)


def _split(x, n):
    parts = []
    for _ in range(n - 1):
        p = x.astype(bf16)
        parts.append(p)
        x = x - p.astype(f32)
    parts.append(x.astype(bf16))
    return parts


def _dot_xl(x, m, n=3):
    return sum(jnp.dot(p, m, preferred_element_type=f32) for p in _split(x, n))


def _dot_lx(m, x, n=3):
    return sum(jnp.dot(m, p, preferred_element_type=f32) for p in _split(x, n))


def _group_sum(xx, bd, terms=2):
    w = bd.shape[0]
    cols = [_dot_xl(xx[:, j:j + w], bd, terms) for j in range(0, xx.shape[1], w)]
    return cols[0] if len(cols) == 1 else jnp.concatenate(cols, axis=1)


def _silu(x):
    return x * jax.nn.sigmoid(x)


def _iota(shape, dim):
    return lax.broadcasted_iota(jnp.int32, shape, dim)


def _modulated_norm(h, shift, scale):
    hn = h * lax.rsqrt(jnp.mean(h * h, axis=-1, keepdims=True) + EPS)
    return hn * (1.0 + scale) + shift


def _mod_kernel(c_ref, w_ref, b_ref, o_ref):
    s = _silu(c_ref[...])
    w = w_ref[0]
    s_hi, s_lo = _split(s, 2)
    w_hi, w_lo = _split(w, 2)
    acc = jnp.dot(s_hi, w_hi, preferred_element_type=f32)
    acc += jnp.dot(s_hi, w_lo, preferred_element_type=f32)
    acc += jnp.dot(s_lo, w_hi, preferred_element_type=f32)
    o_ref[0] = acc + b_ref[0]


def _modulation(cvec, w_mod, b_mod):
    depth, d, n = w_mod.shape
    tn = 1024
    return pl.pallas_call(
        _mod_kernel,
        grid=(depth, n // tn),
        in_specs=[pl.BlockSpec((cvec.shape[0], d), lambda l, j: (0, 0)),
                  pl.BlockSpec((1, d, tn), lambda l, j: (l, 0, j)),
                  pl.BlockSpec((1, 1, tn), lambda l, j: (l, 0, j))],
        out_specs=pl.BlockSpec((1, cvec.shape[0], tn), lambda l, j: (l, 0, j)),
        out_shape=jax.ShapeDtypeStruct((depth, cvec.shape[0], n), f32),
        compiler_params=_cparams(2), name="modulation",
    )(cvec, w_mod, b_mod.reshape(depth, 1, n))


FFN_CHUNKS = 11


def _ffn_kernel(*refs, idx, n_chunks, n_ctx_tiles):
    mod_ref, w1_ref, w2_ref, o_ref = refs[-4:]
    h = refs[-5][...]
    if n_ctx_tiles:
        h = jnp.where(pl.program_id(0) < n_ctx_tiles, refs[0][...], h)
    mod = mod_ref[0]
    hn = _modulated_norm(h, mod[idx:idx + 1], mod[idx + 1:idx + 2]).astype(bf16)
    ck = D_FF // n_chunks
    acc = None
    for c in range(n_chunks):
        g = jnp.dot(hn, w1_ref[:, c * ck:(c + 1) * ck], preferred_element_type=f32)
        u = jnp.dot(hn, w1_ref[:, D_FF + c * ck:D_FF + (c + 1) * ck], preferred_element_type=f32)
        a = (_silu(g) * u).astype(bf16)
        part = jnp.dot(a, w2_ref[c * ck:(c + 1) * ck, :], preferred_element_type=f32)
        acc = part if acc is None else acc + part
    o_ref[...] = h + (0.5 * mod[idx + 2:idx + 3]) * acc


def _ffn(h, mod, w1, w2, idx, lay):
    tm, d = lay["tm"], D_MODEL
    pair = isinstance(h, tuple)
    nct = lay["nc"] // tm
    n_tiles = (h[0].shape[0] + h[1].shape[0] if pair else h.shape[0]) // tm
    group = functools.partial(_mod_group, lay=lay, off=0)
    if pair:
        h_specs = [pl.BlockSpec((tm, d), lambda i: (jnp.minimum(i, nct - 1), 0)),
                   pl.BlockSpec((tm, d), lambda i: (jnp.maximum(i - nct, 0), 0))]
    else:
        h_specs = [pl.BlockSpec((tm, d), lambda i: (i, 0))]
    return pl.pallas_call(
        functools.partial(_ffn_kernel, idx=idx, n_chunks=FFN_CHUNKS, n_ctx_tiles=nct if pair else 0),
        grid=(n_tiles,),
        in_specs=h_specs + [pl.BlockSpec((1, 9, d), lambda i: (group(i), 0, 0)),
                            pl.BlockSpec((d, 2 * D_FF), lambda i: (0, 0), pipeline_mode=pl.Buffered(1)),
                            pl.BlockSpec((D_FF, d), lambda i: (0, 0), pipeline_mode=pl.Buffered(1))],
        out_specs=pl.BlockSpec((tm, d), lambda i: (i, 0)),
        out_shape=jax.ShapeDtypeStruct((n_tiles * tm, d), f32),
        compiler_params=_cparams(1), name=f"ffn_half_step_{idx}",
    )(*(h if pair else (h,)), mod, w1, w2)


def _mod_group(i, lay, off):
    r = (i + off) * lay["tm"]
    return jnp.where(r < lay["nc"], lay["b"], (r - lay["nc"]) // lay["seq"])


def _rope(x, cos, sin, quarter):
    w = x.shape[1]
    reps = w // LANE
    if reps > 1:
        cos = jnp.concatenate([cos] * reps, axis=1)
        sin = jnp.concatenate([sin] * reps, axis=1)
    first = (_iota((1, w), 1) % (2 * quarter)) < quarter
    swapped = jnp.where(first, pltpu.roll(x, w - quarter, 1), pltpu.roll(x, quarter, 1))
    return x * cos + swapped * sin


def _inproj_kernel(h_ref, mod_ref, w_ref, nw_ref, ca_ref, sa_ref, cb_ref, sb_ref, bd32_ref, bd64_ref,
                   qa_o, ka_o, va_o, qb_o, kb_o, vb_o, cq_o, gate_o, a_o, b_o):
    mod = mod_ref[0]
    hn = _modulated_norm(h_ref[...], mod[3:4], mod[4:5]).astype(bf16)

    def proj(name):
        off, width = SEG[name]
        cols = [jnp.dot(hn, w_ref[:, c:min(c + 256, off + width)], preferred_element_type=f32)
                for c in range(off, off + width, 256)]
        return cols[0] if len(cols) == 1 else jnp.concatenate(cols, axis=1)

    def normed(x, bd, group, wrow):
        s = _group_sum(x * x, bd, terms=1)
        return x * lax.rsqrt(s * (1.0 / group) + EPS) * wrow

    nw = nw_ref[...]
    bd32, bd64 = bd32_ref[...], bd64_ref[...]
    ca, sa, cb, sb = ca_ref[...], sa_ref[...], cb_ref[...], sb_ref[...]
    qa_o[...] = _rope(normed(proj("qa"), bd32, A_DIM, nw[0:1, :256]), ca, sa, A_DIM // 4).astype(bf16)
    ka_o[...] = _rope(normed(proj("ka"), bd32, A_DIM, nw[1:2, :256]), ca, sa, A_DIM // 4).astype(bf16)
    va_o[...] = proj("va").astype(bf16)
    qb = proj("qb")
    wq = jnp.concatenate([nw[2:3, :256]] * (B_WIDE // 256), axis=1)
    qb_o[...] = _rope(normed(qb, bd64, B_DIM, wq), cb, sb, B_DIM // 4).astype(bf16)
    kb_o[...] = _rope(normed(proj("kb"), bd64[:LANE, :LANE], B_DIM, nw[3:4, :LANE]), cb, sb, B_DIM // 4).astype(bf16)
    vb_o[...] = proj("vb").astype(bf16)
    cq_o[...] = proj("cq")
    gate_o[...] = proj("gate")
    a_o[...] = proj("a")
    b_o[...] = proj("b")


def _inproj(h, mod, w_in, nw, tabs, bd32, bd64, lay):
    tm, d = lay["tm"], D_MODEL
    n = h.shape[0]
    n_tiles = n // tm
    nct = lay["nc"] // tm
    spt = lay["seq"] // tm
    group = functools.partial(_mod_group, lay=lay, off=0)

    def tab_idx(i):
        return jnp.where(i < nct, 0, 1 + (i - nct) % spt)

    row = lambda w: pl.BlockSpec((tm, w), lambda i: (i, 0))
    const = lambda shape: pl.BlockSpec(shape, lambda i: (0,) * len(shape))
    tab = pl.BlockSpec((tm, LANE), lambda i: (tab_idx(i), 0))
    names = ("qa", "ka", "va", "qb", "kb", "vb", "cq", "gate", "a", "b")
    dts = (bf16,) * 6 + (f32,) * 4
    return pl.pallas_call(
        _inproj_kernel,
        grid=(n_tiles,),
        in_specs=[row(d), pl.BlockSpec((1, 9, d), lambda i: (group(i), 0, 0)),
                  pl.BlockSpec((d, IN_WIDE), lambda i: (0, 0), pipeline_mode=pl.Buffered(1)),
                  const(nw.shape), tab, tab, tab, tab, const(bd32.shape), const(bd64.shape)],
        out_specs=[row(SEG[k][1]) for k in names],
        out_shape=[jax.ShapeDtypeStruct((n, SEG[k][1]), dt) for k, dt in zip(names, dts)],
        compiler_params=_cparams(1), name="mixer_in_proj",
    )(h, mod, w_in, nw, *tabs, bd32, bd64)


A_TQ = 512
A_KBLOCK = 256
A_INTERLEAVE = 8


def _attn_a_kernel(lam_ref, q_ref, *refs, n_seg):
    k_refs, v_refs = refs[:n_seg], refs[n_seg:2 * n_seg]
    o_ref, vt_ref = refs[2 * n_seg], refs[2 * n_seg + 1]
    tq = q_ref.shape[0]
    n_maps = 2 * A_HEADS

    @pl.when(pl.program_id(1) == 0)
    def _():
        off = 0
        for v_ref in v_refs:
            for r in range(0, v_ref.shape[0], A_KBLOCK):
                vt_ref[:, off + r:off + r + A_KBLOCK] = v_ref[r:r + A_KBLOCK, :].astype(f32).T.astype(bf16)
            off += v_ref.shape[0]

    lam = lam_ref[0]
    qt = q_ref[...].astype(f32).T
    feat = _iota((A_W, tq), 0)
    qms = [jnp.where(feat // A_DIM == n, qt, 0.0).astype(bf16) for n in range(n_maps)]
    normed = [None] * n_maps
    for g0 in range(0, n_maps, A_INTERLEAVE):
        grp = range(g0, g0 + A_INTERLEAVE)
        m_run = {n: jnp.full((1, tq), NEG_INF, f32) for n in grp}
        l_run = {n: jnp.zeros((1, tq), f32) for n in grp}
        acc = {n: jnp.zeros((A_VDIM, tq), f32) for n in grp}
        off = 0
        for k_ref in k_refs:
            for r in range(0, k_ref.shape[0], A_KBLOCK):
                kb = k_ref[r:r + A_KBLOCK, :]
                st = {n: jnp.dot(kb, qms[n], preferred_element_type=f32) for n in grp}
                m_new = {n: jnp.maximum(m_run[n], jnp.max(st[n], axis=0, keepdims=True)) for n in grp}
                alpha = {n: jnp.exp2(m_run[n] - m_new[n]) for n in grp}
                e = {n: jnp.exp2(st[n] - m_new[n]) for n in grp}
                l_run = {n: alpha[n] * l_run[n] + jnp.sum(e[n], axis=0, keepdims=True) for n in grp}
                pv = {n: jnp.dot(vt_ref[(n // 2) * A_VDIM:(n // 2 + 1) * A_VDIM, off + r:off + r + A_KBLOCK],
                                 e[n].astype(bf16), preferred_element_type=f32) for n in grp}
                acc = {n: alpha[n] * acc[n] + pv[n] for n in grp}
                m_run = m_new
            off += k_ref.shape[0]
        for n in grp:
            normed[n] = acc[n] * ((lam if n % 2 else 1.0) / l_run[n])
    heads = [normed[2 * h] - normed[2 * h + 1] for h in range(A_HEADS)]
    o_ref[...] = jnp.concatenate(heads, axis=0).T


def _attn_a(lam, qa, ka, va, lay, latent):
    b, ctx, seq, nc = lay["b"], lay["ctx"], lay["seq"], lay["nc"]
    smem = pl.BlockSpec(memory_space=pltpu.SMEM)
    if latent:
        tq = A_TQ
        qpb = seq // tq
        grid = (b, qpb)
        qspec = pl.BlockSpec((tq, A_W), lambda i, j: (nc // tq + i * qpb + j, 0))
        kv = [pl.BlockSpec((ctx, A_W), lambda i, j: (i, 0)), pl.BlockSpec((seq, A_W), lambda i, j: (nc // seq + i, 0))]
        ospec = pl.BlockSpec((tq, A_W), lambda i, j: (i * qpb + j, 0))
        rows = b * seq
    else:
        grid = (b, 1)
        qspec = pl.BlockSpec((ctx, A_W), lambda i, j: (i, 0))
        kv = [pl.BlockSpec((ctx, A_W), lambda i, j: (i, 0))]
        ospec = qspec
        rows = nc
    n_seg = len(kv)
    return pl.pallas_call(
        functools.partial(_attn_a_kernel, n_seg=n_seg),
        grid=grid,
        in_specs=[smem, qspec] + kv + kv,
        out_specs=ospec,
        out_shape=jax.ShapeDtypeStruct((rows, A_W), f32),
        scratch_shapes=[pltpu.VMEM((A_W, ctx + seq if latent else ctx), bf16)],
        compiler_params=_cparams(2), name="diff_attention_lat" if latent else "diff_attention_ctx",
    )(lam, qa, *([ka] * n_seg), *([va] * n_seg))


B_QSUB = 4


def _attn_b_kernel(sink_ref, q_ref, kc_ref, vc_ref, *refs, latent):
    o_ref = refs[-1]
    lane = _iota((1, LANE), 1)
    n_ctx = kc_ref.shape[0]
    kc, vc = kc_ref[...], vc_ref[...]
    rows = B_BLOCK if latent else q_ref.shape[0]
    n_sub = q_ref.shape[0] // rows
    for sub in range(n_sub):
        q = q_ref[sub * rows:(sub + 1) * rows, :]
        keys, vals, band_ok = kc, vc, None
        if latent:
            kl_ref, vl_ref = refs[0], refs[1]
            t = pl.program_id(1) * n_sub + sub
            nb = pl.num_programs(1) * n_sub
            starts = [pl.multiple_of(jnp.clip(t + off, 0, nb - 1) * B_BLOCK, B_BLOCK) for off in (-1, 0, 1)]
            keys = jnp.concatenate([kc] + [kl_ref[pl.ds(st, B_BLOCK), :] for st in starts], axis=0)
            vals = jnp.concatenate([vc] + [vl_ref[pl.ds(st, B_BLOCK), :] for st in starts], axis=0)
            r = _iota((B_GROUP * rows, 3 * B_BLOCK), 0) % rows
            c = _iota((B_GROUP * rows, 3 * B_BLOCK), 1)
            cc = c % B_BLOCK
            before = jnp.logical_and(jnp.logical_and(c < B_BLOCK, cc >= r), t >= 1)
            after = jnp.logical_and(jnp.logical_and(c >= 2 * B_BLOCK, cc <= r), t + 1 < nb)
            same = jnp.logical_and(c >= B_BLOCK, c < 2 * B_BLOCK)
            band_ok = jnp.logical_or(same, jnp.logical_or(before, after))
        for g in range(B_KV_HEADS):
            heads = range(g * B_GROUP, (g + 1) * B_GROUP)
            qg = jnp.concatenate([q[:, h * LANE:(h + 1) * LANE] for h in heads], axis=0)
            sink = jnp.concatenate([jnp.full((rows, 1), sink_ref[h], f32) for h in heads], axis=0)
            x = _dot_nt(qg, keys)
            s = [x] if band_ok is None else [x[:, :n_ctx], jnp.where(band_ok, x[:, n_ctx:], NEG_INF)]
            mx = functools.reduce(jnp.maximum, [jnp.max(x, axis=-1, keepdims=True) for x in s] + [sink])
            e = [jnp.exp2(x - mx) for x in s]
            den = functools.reduce(jnp.add, [jnp.sum(x, axis=-1, keepdims=True) for x in e]) + jnp.exp2(sink - mx)
            p = e[0] if len(e) == 1 else jnp.concatenate(e, axis=1)
            og = jnp.dot(p.astype(bf16), vals, preferred_element_type=f32) * (1.0 / den)
            og = jnp.where(lane // B_DIM == g, og, 0.0).astype(bf16)
            for n, h in enumerate(heads):
                o_ref[sub * rows:(sub + 1) * rows, h * LANE:(h + 1) * LANE] = og[n * rows:(n + 1) * rows]


def _attn_b(sink, qb, kb, vb, lay, latent):
    b, ctx, seq, nc = lay["b"], lay["ctx"], lay["seq"], lay["nc"]
    smem = pl.BlockSpec(memory_space=pltpu.SMEM)
    cspec = pl.BlockSpec((ctx, LANE), lambda i, j: (i, 0))
    if latent:
        tq = B_QSUB * B_BLOCK
        nb = seq // tq
        grid = (b, nb)
        qspec = pl.BlockSpec((tq, B_WIDE), lambda i, j: (nc // tq + i * nb + j, 0))
        lspec = pl.BlockSpec((seq, LANE), lambda i, j: (nc // seq + i, 0))
        in_specs = [smem, qspec, cspec, cspec, lspec, lspec]
        args = (sink, qb, kb, vb, kb, vb)
        ospec = pl.BlockSpec((tq, B_WIDE), lambda i, j: (i * nb + j, 0))
        rows = b * seq
    else:
        grid = (b, 1)
        qspec = pl.BlockSpec((ctx, B_WIDE), lambda i, j: (i, 0))
        in_specs = [smem, qspec, cspec, cspec]
        args = (sink, qb, kb, vb)
        ospec = qspec
        rows = nc
    return pl.pallas_call(
        functools.partial(_attn_b_kernel, latent=latent),
        grid=grid, in_specs=in_specs, out_specs=ospec,
        out_shape=jax.ShapeDtypeStruct((rows, B_WIDE), bf16),
        compiler_params=_cparams(2), name="window_attention_lat" if latent else "sink_attention_ctx",
    )(*args)


HALO = 8


def _gdn_prep_kernel(x_ref, prev_ref, next_ref, a_ref, b_ref, cw_ref, par_ref, bd64_ref,
                     q_o, k_o, v_o, g_o, beta_o, *, nct, cpt, spt):
    i = pl.program_id(0)
    tm = x_ref.shape[0]
    j = jnp.where(i < nct, i % cpt, (i - nct) % spt)
    per_seq = jnp.where(i < nct, cpt, spt)
    first = j == 0
    last = j == per_seq - 1
    prev = jnp.where(first, 0.0, prev_ref[...])
    nxt = jnp.where(last, 0.0, next_ref[...])
    xx = jnp.concatenate([prev, x_ref[...], nxt], axis=0)
    cw = cw_ref[...]
    rows = xx.shape[0]
    y = None
    for tap in range(C_CONV):
        shift = (C_CONV // 2 - tap) % rows
        sh = xx if shift == 0 else pltpu.roll(xx, shift, 0)
        term = sh[HALO:HALO + tm] * cw[tap:tap + 1]
        y = term if y is None else y + term
    y = _silu(y)
    bd64 = bd64_ref[...]

    def l2n(t):
        return t * lax.rsqrt(_group_sum(t * t, bd64, terms=1) + EPS)

    q_o[...] = (l2n(y[:, :C_W]) * (C_DK ** -0.5)).astype(bf16)
    k_o[...] = l2n(y[:, C_W:2 * C_W]).astype(bf16)
    v_o[...] = y[:, 2 * C_W:].astype(bf16)
    par = par_ref[...]
    z = a_ref[...] + par[1:2]
    softplus = jnp.maximum(z, 0.0) + jnp.log1p(jnp.exp(-jnp.abs(z)))
    g_o[...] = -par[0:1] * softplus
    beta_o[...] = jax.nn.sigmoid(b_ref[...])


def _gdn_prep(cq, a, bb, cw, par, bd64, lay):
    tm = 256
    n = cq.shape[0]
    n_tiles = n // tm
    nct, spt = lay["nc"] // tm, lay["seq"] // tm
    hb = tm // HALO
    last_blk = n // HALO - 1
    row = lambda w: pl.BlockSpec((tm, w), lambda i: (i, 0))
    const = lambda shape: pl.BlockSpec(shape, lambda i: (0,) * len(shape))
    return pl.pallas_call(
        functools.partial(_gdn_prep_kernel, nct=nct, cpt=lay["ctx"] // tm, spt=spt),
        grid=(n_tiles,),
        in_specs=[row(3 * C_W),
                  pl.BlockSpec((HALO, 3 * C_W), lambda i: (jnp.maximum(i * hb - 1, 0), 0)),
                  pl.BlockSpec((HALO, 3 * C_W), lambda i: (jnp.minimum((i + 1) * hb, last_blk), 0)),
                  row(LANE), row(LANE), const(cw.shape), const(par.shape), const(bd64.shape)],
        out_specs=[row(C_W), row(C_W), row(C_W), row(LANE), row(LANE)],
        out_shape=[jax.ShapeDtypeStruct((n, C_W), bf16)] * 3 + [jax.ShapeDtypeStruct((n, LANE), f32)] * 2,
        compiler_params=_cparams(1), name="gdn_inputs",
    )(cq, cq, cq, a, bb, cw, par, bd64)


GW = 4 * C_DK


def _bd(x):
    t = jnp.concatenate([x.astype(bf16)] * 4, axis=0)
    same = _iota((GW, GW), 0) // C_DK == _iota((GW, GW), 1) // C_DK
    return jnp.where(same, t, jnp.zeros_like(t))


N_GROUPS = 3
GDN_W = N_GROUPS * GW
PREP_UNROLL = 4


def _per_head_lanes(x, first):
    rows = x.shape[0]
    left = _iota((1, LANE), 1) < C_DK
    cols = []
    for pair in range(C_HEADS // 2):
        a = jnp.broadcast_to(x[:, first + 2 * pair:first + 2 * pair + 1], (rows, LANE))
        b = jnp.broadcast_to(x[:, first + 2 * pair + 1:first + 2 * pair + 2], (rows, LANE))
        cols.append(jnp.where(left, a, b))
    return jnp.concatenate(cols, axis=1)


def _per_head_rows(x, first):
    rows = x.shape[0]
    xt = jnp.concatenate([x, jnp.zeros_like(x)], axis=0).T
    xt = xt + pltpu.roll(xt, C_DK, 1)
    left = _iota((1, LANE), 1) < C_DK
    cols = []
    for pair in range(C_HEADS // 2):
        a = jnp.broadcast_to(xt[first + 2 * pair:first + 2 * pair + 1, :], (rows, LANE))
        b = jnp.broadcast_to(xt[first + 2 * pair + 1:first + 2 * pair + 2, :], (rows, LANE))
        cols.append(jnp.where(left, a, b))
    return jnp.concatenate(cols, axis=1)


def _gdn_prepare(groups, fwd_lanes):
    L = C_CHUNK
    n = range(len(groups))
    ri = _iota((L, GW), 0)
    ci = _iota((L, GW), 1) % C_DK
    eye = ci == ri
    ahead = [jnp.where(f, ri - ci, ci - ri) for f in fwd_lanes]
    qs, ks, vs, gcs, betas, gr = zip(*groups)
    kq =[_dot_nt(jnp.concatenate([ks[i], qs[i]], axis=0), _bd(ks[i])) for i in n]
    dec = [jnp.exp(jnp.where(ahead[i] >= 0, gcs[i] - gr[i], NEG_INF)) for i in n]
    a = [jnp.where(ahead[i] > 0, betas[i] * kq[i][:L] * dec[i], 0.0) for i in n]
    tm = [jnp.where(eye, 1.0, 0.0) - jnp.where(ri // 2 == ci // 2, a[i], 0.0) for i in n]
    s = 2
    while s < L:
        off = jnp.logical_and(ri // (2 * s) == ci // (2 * s), ri // s != ci // s)
        y = [_dot(jnp.where(off, a[i], 0.0), _bd(tm[i])) for i in n]
        tm = [tm[i] - _dot(tm[i], _bd(y[i])) for i in n]
        s *= 2
    eg = [jnp.exp(gcs[i]) for i in n]
    uw = [_dot(tm[i], jnp.concatenate([_bd(betas[i] * vs[i]), _bd(betas[i] * eg[i] * ks[i])], axis=1)) for i in n]
    out = []
    for i in n:
        g_last = jnp.where(fwd_lanes[i], gcs[i][L - 1:L], gcs[i][0:1])
        wq = jnp.concatenate([uw[i][:, GW:], qs[i] * eg[i]], axis=0).astype(bf16)
        qk = jnp.where(ahead[i] >= 0, kq[i][L:] * dec[i], 0.0).astype(bf16)
        kd = (ks[i] * jnp.exp(g_last - gcs[i])).astype(bf16)
        out.append((uw[i][:, :GW], wq, qk, kd, jnp.exp(g_last)))
    return out


def _gdn_scan(chunks, s_ref):
    L = C_CHUNK
    n = range(len(chunks))
    same = _iota((GW, GW), 0) // C_DK == _iota((GW, GW), 1) // C_DK
    s = [s_ref[i] for i in n]
    ws = [_dot(chunks[i][1], s[i]) for i in n]
    v_new = [chunks[i][0] - ws[i][:L] for i in n]
    o = [ws[i][L:] + _dot(chunks[i][2], _bd(v_new[i])) for i in n]
    upd = [_dot_tn(chunks[i][3], v_new[i]) for i in n]
    for i in n:
        s_ref[i] = s[i] * chunks[i][4] + jnp.where(same, upd[i], 0.0)
    return o


def _gdn_kernel(qc, kc, vc, gc, bc, ql, kl, vl, gl, bl, oc_ref, ol_ref, s_ref, *slot_refs):
    L = C_CHUNK
    slots = (slot_refs[:5], slot_refs[5:])
    s_ref[...] = jnp.zeros(s_ref.shape, f32)
    lane = _iota((1, GW), 1)
    fwd_lanes = [lane >= 0, lane < 0, lane < LANE]
    r64 = _iota((L, L), 0)
    c64 = _iota((L, L), 1)
    tri_l = (c64 <= r64).astype(bf16)
    tri_u = (c64 >= r64).astype(bf16)

    nb = PREP_UNROLL

    def prepare_block(refs, blk, slot):
        q_ref, k_ref, v_ref, g_ref, b_ref = refs
        n_chunks = q_ref.shape[0] // L
        groups = []
        for sub in range(nb):
            t = blk * nb + sub
            sl_f = pl.ds(pl.multiple_of(t * L, L), L)
            sl_b = pl.ds(pl.multiple_of((n_chunks - 1 - t) * L, L), L)
            gc_f = _dot_lx(tri_l, g_ref[sl_f, :])
            gc_b = _dot_lx(tri_u, g_ref[sl_b, :])
            data_f = ([r[sl_f, :].astype(f32) for r in (q_ref, k_ref, v_ref)]
                      + [_per_head_lanes(gc_f, 0), _per_head_lanes(b_ref[sl_f, :], 0), _per_head_rows(gc_f, 0)])
            data_b = ([r[sl_b, :].astype(f32) for r in (q_ref, k_ref, v_ref)]
                      + [_per_head_lanes(gc_b, C_HEADS), _per_head_lanes(b_ref[sl_b, :], C_HEADS),
                         _per_head_rows(gc_b, C_HEADS)])
            groups += [tuple(x[:, :GW] for x in data_f), tuple(x[:, :GW] for x in data_b),
                       tuple(jnp.concatenate([xf[:, GW:], xb[:, GW:]], axis=1) for xf, xb in zip(data_f, data_b))]
        u_s, wq_s, qk_s, kd_s, dl_s = slots[slot]
        for n, (u, wq, qk, kd, dl) in enumerate(_gdn_prepare(groups, fwd_lanes * nb)):
            it = n // N_GROUPS
            cols = slice((n % N_GROUPS) * GW, (n % N_GROUPS + 1) * GW)
            u_s[it * L:(it + 1) * L, cols] = u
            wq_s[it * 2 * L:(it + 1) * 2 * L, cols] = wq
            qk_s[it * L:(it + 1) * L, cols] = qk
            kd_s[it * L:(it + 1) * L, cols] = kd
            dl_s[it * 8:(it + 1) * 8, cols] = jnp.broadcast_to(dl, (8, GW))

    def scan_block(o_ref, blk, slot):
        n_chunks = o_ref.shape[0] // L
        u_s, wq_s, qk_s, kd_s, dl_s = slots[slot]
        for sub in range(nb):
            t = blk * nb + sub
            sl_f = pl.ds(pl.multiple_of(t * L, L), L)
            sl_b = pl.ds(pl.multiple_of((n_chunks - 1 - t) * L, L), L)
            r1 = slice(sub * L, (sub + 1) * L)
            r2 = slice(sub * 2 * L, (sub + 1) * 2 * L)
            r8 = slice(sub * 8, sub * 8 + 1)
            chunks = []
            for i in range(N_GROUPS):
                cols = slice(i * GW, (i + 1) * GW)
                chunks.append((u_s[r1, cols], wq_s[r2, cols], qk_s[r1, cols], kd_s[r1, cols], dl_s[r8, cols]))
            o0, o1, o2 = _gdn_scan(chunks, s_ref)
            o_ref[sl_f, :] += jnp.concatenate([o0, o2[:, :LANE]], axis=1)
            o_ref[sl_b, :] += jnp.concatenate([o1, o2[:, LANE:]], axis=1)

    ctx_refs, lat_refs = (qc, kc, vc, gc, bc), (ql, kl, vl, gl, bl)
    cb, lb = qc.shape[0] // (L * nb), ql.shape[0] // (L * nb)
    oc_ref[...] = jnp.zeros(oc_ref.shape, f32)
    ol_ref[...] = jnp.zeros(ol_ref.shape, f32)
    prepare_block(ctx_refs, 0, 0)
    for j in range(cb):
        if j + 1 < cb:
            prepare_block(ctx_refs, j + 1, (j + 1) % 2)
        else:
            prepare_block(lat_refs, 0, (j + 1) % 2)
        scan_block(oc_ref, j, j % 2)

    def body(pair, carry):
        for half in range(2):
            tt = 2 * pair + half
            scan_block(ol_ref, tt, (cb + half) % 2)
            prepare_block(lat_refs, tt + 1, (cb + half + 1) % 2)
        return carry

    lax.fori_loop(0, (lb - 1) // 2, body, 0)
    for tt in range(2 * ((lb - 1) // 2), lb - 1):
        prepare_block(lat_refs, tt + 1, (cb + tt + 1) % 2)
        scan_block(ol_ref, tt, (cb + tt) % 2)
    scan_block(ol_ref, lb - 1, (cb + lb - 1) % 2)


def _gdn(q, k, v, g, beta, lay):
    b, ctx, seq, nc = lay["b"], lay["ctx"], lay["seq"], lay["nc"]
    cs = lambda w: pl.BlockSpec((ctx, w), lambda i: (i, 0))
    ls = lambda w: pl.BlockSpec((seq, w), lambda i: (nc // seq + i, 0))
    widths = (C_W, C_W, C_W, LANE, LANE)
    n_it = PREP_UNROLL
    assert ctx % (C_CHUNK * PREP_UNROLL) == 0 and seq % (C_CHUNK * PREP_UNROLL) == 0
    return pl.pallas_call(
        _gdn_kernel,
        grid=(b,),
        in_specs=[cs(w) for w in widths] + [ls(w) for w in widths],
        out_specs=[pl.BlockSpec((ctx, C_W), lambda i: (i, 0)), pl.BlockSpec((seq, C_W), lambda i: (i, 0))],
        out_shape=[jax.ShapeDtypeStruct((nc, C_W), f32), jax.ShapeDtypeStruct((b * seq, C_W), f32)],
        scratch_shapes=[pltpu.VMEM((N_GROUPS, GW, GW), f32)] + 2 * [
            pltpu.VMEM((n_it * C_CHUNK, GDN_W), f32),
            pltpu.VMEM((n_it * 2 * C_CHUNK, GDN_W), bf16),
            pltpu.VMEM((n_it * C_CHUNK, GDN_W), bf16),
            pltpu.VMEM((n_it * C_CHUNK, GDN_W), bf16),
            pltpu.VMEM((n_it * 8, GDN_W), f32)],
        compiler_params=_cparams(1), name="gated_deltanet",
    )(q, k, v, g, beta, q, k, v, g, beta)


def _outproj_kernel(h_ref, mod_ref, *refs, n_ctx_tiles):
    gate_ref, w_ref, nw_ref, bd64_ref, o_ref = refs[-5:]
    mixed = [r[...] for r in refs[-8:-5]]
    if n_ctx_tiles:
        is_ctx = pl.program_id(0) < n_ctx_tiles
        mixed = [jnp.where(is_ctx, c[...], x) for c, x in zip(refs[:3], mixed)]
    oa, ob, oc = mixed
    bd64 = bd64_ref[...]
    nw = nw_ref[...]

    def normed(x, wrow):
        s = _group_sum(x * x, bd64, terms=1)
        return x * lax.rsqrt(s * (1.0 / C_DK) + EPS) * wrow

    ya = normed(oa, nw[0:1, :A_W]).astype(bf16)
    yc = (normed(oc, nw[1:2, :C_W]) * _silu(gate_ref[...])).astype(bf16)
    y = jnp.dot(ya, w_ref[:A_W, :], preferred_element_type=f32)
    y += jnp.dot(ob, w_ref[A_W:A_W + B_WIDE, :], preferred_element_type=f32)
    y += jnp.dot(yc, w_ref[A_W + B_WIDE:, :], preferred_element_type=f32)
    o_ref[...] = h_ref[...] + mod_ref[0][5:6] * y


def _outproj(h, mod, ctx_outs, lat_outs, gate, w_out, nw, bd128, lay):
    tm, d = lay["tm"], D_MODEL
    nct = lay["nc"] // tm
    off = 0 if ctx_outs else nct
    n_tiles = h.shape[0] // tm - off
    group = functools.partial(_mod_group, lay=lay, off=off)
    rowo = lambda w: pl.BlockSpec((tm, w), lambda i: (i + off, 0))
    rowc = lambda w: pl.BlockSpec((tm, w), lambda i: (jnp.minimum(i, nct - 1), 0))
    rowl = lambda w: pl.BlockSpec((tm, w), lambda i: (jnp.maximum(i + off - nct, 0), 0))
    const = lambda shape: pl.BlockSpec(shape, lambda i: (0,) * len(shape))
    widths = (A_W, B_WIDE, C_W)
    ctx_specs = [rowc(w) for w in widths] if ctx_outs else []
    return pl.pallas_call(
        functools.partial(_outproj_kernel, n_ctx_tiles=nct if ctx_outs else 0),
        grid=(n_tiles,),
        in_specs=[rowo(d), pl.BlockSpec((1, 9, d), lambda i: (group(i), 0, 0))] + ctx_specs
                 + [rowl(w) for w in widths]
                 + [rowo(C_W), pl.BlockSpec((OUT_WIDE, d), lambda i: (0, 0), pipeline_mode=pl.Buffered(1)),
                    const(nw.shape), const(bd128.shape)],
        out_specs=pl.BlockSpec((tm, d), lambda i: (i, 0)),
        out_shape=jax.ShapeDtypeStruct((n_tiles * tm, d), f32),
        compiler_params=_cparams(1), name="mixer_out_proj",
    )(h, mod, *(ctx_outs or ()), *lat_outs, gate, w_out, nw, bd128)


def _block_ones(n, group):
    idx = np.arange(n) // group
    return jnp.asarray(idx[:, None] == idx[None, :], dtype=bf16)


def _rope_table(seq, d, pad_rows):
    half, quarter = d // 2, d // 4
    rows = seq // GRID_W
    row = jnp.repeat(jnp.arange(rows, dtype=f32), GRID_W)
    col = jnp.tile(jnp.arange(GRID_W, dtype=f32), rows)
    inv = ROPE_THETA ** (-jnp.arange(0, half, 2, dtype=f32) / half)
    ld = np.arange(LANE) % d
    pos = jnp.where(jnp.asarray(ld < half)[None, :], row[:, None], col[:, None])
    ang = pos * inv[np.asarray((ld % half) % quarter)][None, :]
    sign = jnp.asarray(np.where((ld % half) < quarter, -1.0, 1.0), dtype=f32)[None, :]
    cos = jnp.concatenate([jnp.ones((pad_rows, LANE), f32), jnp.cos(ang)], axis=0)
    sin = jnp.concatenate([jnp.zeros((pad_rows, LANE), f32), jnp.sin(ang) * sign], axis=0)
    return cos, sin


def _arrange_w_in(w):
    offs = np.concatenate([[0], np.cumsum(IN_SIZES)])
    part = lambda n: w[:, offs[n]:offs[n + 1]]
    zeros = lambda width: jnp.zeros((w.shape[0], width), w.dtype)
    qb = part(3)
    pieces = [part(0), part(1), part(2)]
    for h in range(B_HEADS):
        head = qb[:, h * B_DIM:(h + 1) * B_DIM]
        pieces += [head, zeros(B_DIM)] if h // B_GROUP == 0 else [zeros(B_DIM), head]
    pieces += [part(4), part(5), part(6), part(7), part(8), zeros(LANE - IN_SIZES[8]), part(9), zeros(LANE - IN_SIZES[9])]
    out = jnp.concatenate(pieces, axis=1).astype(bf16)
    assert out.shape[1] == IN_WIDE
    return out


def _arrange_w_out(w):
    zeros = jnp.zeros((B_DIM, w.shape[1]), w.dtype)
    pieces = [w[:A_W]]
    for h in range(B_HEADS):
        head = w[A_W + h * B_DIM:A_W + (h + 1) * B_DIM]
        pieces += [head, zeros] if h // B_GROUP == 0 else [zeros, head]
    pieces.append(w[A_W + B_HEADS * B_DIM:])
    out = jnp.concatenate(pieces, axis=0).astype(bf16)
    assert out.shape[0] == OUT_WIDE
    return out


def _pad_lanes(x, width):
    return jnp.pad(x, ((0, 0), (0, width - x.shape[1])))


def kernel(x, c, ctx, c_ctx, w_mod, b_mod, ffn1_w1, ffn1_w2, ffn2_w1, ffn2_w2, w_in, w_out,
           a_qnorm, a_knorm, a_lambda, a_subln, b_qnorm, b_knorm, b_sink,
           c_conv, c_A_log, c_dt_bias, c_onorm):
    b, seq, d = x.shape
    n_ctx = ctx.shape[1]
    nc = b * n_ctx
    tm = 512 if (nc % 512 == 0 and seq % 512 == 0) else 256
    lay = dict(b=b, ctx=n_ctx, seq=seq, nc=nc, tm=tm)
    assert d == D_MODEL and seq % 256 == 0 and n_ctx % 256 == 0 and nc % seq == 0 and seq % GRID_W == 0

    h = (ctx.reshape(nc, d), x.reshape(b * seq, d))
    cvec = jnp.zeros((16, d), f32).at[:b].set(c).at[b].set(c_ctx)
    mod_all = _modulation(cvec, w_mod, b_mod).reshape(DEPTH, 16, 9, d)

    bd32, bd64, bd64s = _block_ones(256, A_DIM), _block_ones(256, B_DIM), _block_ones(LANE, C_DK)
    tabs = _rope_table(seq, A_DIM, tm) + _rope_table(seq, B_DIM, tm)

    for l in range(DEPTH):
        last = l == DEPTH - 1
        lam_init = 0.8 - 0.6 * float(np.exp(-0.3 * l))
        mod = mod_all[l]
        lf = a_lambda[l].astype(f32)
        lam = (jnp.exp(jnp.sum(lf[0] * lf[1])) - jnp.exp(jnp.sum(lf[2] * lf[3])) + lam_init).reshape(1)
        nw_in = jnp.stack([jnp.tile(a_qnorm[l], 8) * (A_DIM ** -0.5 * LOG2E),
                           jnp.tile(a_knorm[l], 8),
                           jnp.tile(b_qnorm[l], 4) * (B_DIM ** -0.5 * LOG2E),
                           jnp.tile(b_knorm[l], 4)] + [jnp.zeros((256,), f32)] * 4)
        nw_out = jnp.stack([_pad_lanes((jnp.tile(a_subln[l], 4) * (1.0 - lam_init))[None], C_W)[0],
                            jnp.tile(c_onorm[l], C_HEADS)] + [jnp.zeros((C_W,), f32)] * 6)
        gpar = jnp.stack([_pad_lanes(jnp.exp(c_A_log[l].astype(f32)).reshape(1, -1), LANE)[0],
                          _pad_lanes(c_dt_bias[l].astype(f32).reshape(1, -1), LANE)[0]] + [jnp.zeros((LANE,), f32)] * 6)
        cw = jnp.concatenate([c_conv[l], jnp.zeros((8 - C_CONV, 3 * C_W), f32)], axis=0)

        h = _ffn(h, mod, ffn1_w1[l].astype(bf16), ffn1_w2[l].astype(bf16), 0, lay)
        qa, ka, va, qb, kb, vb, cq, gate, pa, pb = _inproj(h, mod, _arrange_w_in(w_in[l]), nw_in, tabs, bd32, bd64, lay)
        sink = b_sink[l].astype(f32) * LOG2E
        oa = _attn_a(lam, qa, ka, va, lay, True)
        ob = _attn_b(sink, qb, kb, vb, lay, True)
        gq, gk, gv, gg, gbeta = _gdn_prep(cq, pa, pb, cw, gpar, bd64s, lay)
        oc_ctx, oc = _gdn(gq, gk, gv, gg, gbeta, lay)
        ctx_outs = None
        if not last:
            ctx_outs = (_attn_a(lam, qa, ka, va, lay, False), _attn_b(sink, qb, kb, vb, lay, False), oc_ctx)
        h = _outproj(h, mod, ctx_outs, (oa, ob, oc), gate, _arrange_w_out(w_out[l]), nw_out, bd64s, lay)
        if last:
            lay = dict(lay, nc=0)
        h = _ffn(h, mod, ffn2_w1[l].astype(bf16), ffn2_w2[l].astype(bf16), 6, lay)
    return h.reshape(b, seq, d)
```

```python
import functools

import numpy as np
import jax
import jax.numpy as jnp
from jax import lax
from jax.experimental import pallas as pl
from jax.experimental.pallas import tpu as pltpu

f32 = jnp.float32
bf16 = jnp.bfloat16

D_MODEL = 1024
DEPTH = 2
GRID_W = 64
EPS = 1e-6
NEG_INF = -1e30
LOG2E = 1.4426950408889634
ROPE_THETA = 10000.0
D_FF = 2816
A_HEADS, A_DIM, A_VDIM = 4, 32, 64
A_W = A_HEADS * A_VDIM
B_HEADS, B_KV_HEADS, B_DIM = 6, 2, 64
B_GROUP = B_HEADS // B_KV_HEADS
B_BLOCK = 128
C_HEADS, C_DK, C_CONV, C_CHUNK = 6, 64, 5, 64
C_W = C_HEADS * C_DK
IN_SIZES = (256, 256, 256, 384, 128, 128, 1152, 384, 12, 12)
LANE = 128
B_WIDE = B_HEADS * LANE
SEG = dict(qa=(0, 256), ka=(256, 256), va=(512, 256), qb=(768, B_WIDE), kb=(1536, 128), vb=(1664, 128),
           cq=(1792, 1152), gate=(2944, 384), a=(3328, 128), b=(3456, 128))
IN_WIDE = 3584
OUT_WIDE = A_W + B_WIDE + C_W
VMEM_LIMIT = 56 * 1024 * 1024


def _cparams(n_axes):
    return pltpu.CompilerParams(dimension_semantics=("arbitrary",) * n_axes, vmem_limit_bytes=VMEM_LIMIT)


def _dot(a, b):
    return jnp.dot(a.astype(bf16), b.astype(bf16), preferred_element_type=f32)


def _dot_nt(a, b):
    return lax.dot_general(a.astype(bf16), b.astype(bf16), (((1,), (1,)), ((), ())), preferred_element_type=f32)


def _dot_tn(a, b):
    return lax.dot_general(a.astype(bf16), b.astype(bf16), (((0,), (0,)), ((), ())), preferred_element_type=f32)


def _split(x, n):
    parts = []
    for _ in range(n - 1):
        p = x.astype(bf16)
        parts.append(p)
        x = x - p.astype(f32)
    parts.append(x.astype(bf16))
    return parts


def _dot_xl(x, m, n=3):
    return sum(jnp.dot(p, m, preferred_element_type=f32) for p in _split(x, n))


def _dot_lx(m, x, n=3):
    return sum(jnp.dot(m, p, preferred_element_type=f32) for p in _split(x, n))


def _group_sum(xx, bd, terms=2):
    w = bd.shape[0]
    cols = [_dot_xl(xx[:, j:j + w], bd, terms) for j in range(0, xx.shape[1], w)]
    return cols[0] if len(cols) == 1 else jnp.concatenate(cols, axis=1)


def _silu(x):
    return x * jax.nn.sigmoid(x)


def _iota(shape, dim):
    return lax.broadcasted_iota(jnp.int32, shape, dim)


def _modulated_norm(h, shift, scale):
    hn = h * lax.rsqrt(jnp.mean(h * h, axis=-1, keepdims=True) + EPS)
    return hn * (1.0 + scale) + shift


def _mod_kernel(c_ref, w_ref, b_ref, o_ref):
    s = _silu(c_ref[...])
    w = w_ref[0]
    s_hi, s_lo = _split(s, 2)
    w_hi, w_lo = _split(w, 2)
    acc = jnp.dot(s_hi, w_hi, preferred_element_type=f32)
    acc += jnp.dot(s_hi, w_lo, preferred_element_type=f32)
    acc += jnp.dot(s_lo, w_hi, preferred_element_type=f32)
    o_ref[0] = acc + b_ref[0]


def _modulation(cvec, w_mod, b_mod):
    depth, d, n = w_mod.shape
    tn = 1024
    return pl.pallas_call(
        _mod_kernel,
        grid=(depth, n // tn),
        in_specs=[pl.BlockSpec((cvec.shape[0], d), lambda l, j: (0, 0)),
                  pl.BlockSpec((1, d, tn), lambda l, j: (l, 0, j)),
                  pl.BlockSpec((1, 1, tn), lambda l, j: (l, 0, j))],
        out_specs=pl.BlockSpec((1, cvec.shape[0], tn), lambda l, j: (l, 0, j)),
        out_shape=jax.ShapeDtypeStruct((depth, cvec.shape[0], n), f32),
        compiler_params=_cparams(2), name="modulation",
    )(cvec, w_mod, b_mod.reshape(depth, 1, n))


FFN_CHUNKS = 11


def _ffn_half_step(h, mod, idx, w1_ref, w2_ref):
    hn = _modulated_norm(h, mod[idx:idx + 1], mod[idx + 1:idx + 2]).astype(bf16)
    ck = D_FF // FFN_CHUNKS
    acc = None
    for c in range(FFN_CHUNKS):
        g = jnp.dot(hn, w1_ref[:, c * ck:(c + 1) * ck], preferred_element_type=f32)
        u = jnp.dot(hn, w1_ref[:, D_FF + c * ck:D_FF + (c + 1) * ck], preferred_element_type=f32)
        a = (_silu(g) * u).astype(bf16)
        part = jnp.dot(a, w2_ref[c * ck:(c + 1) * ck, :], preferred_element_type=f32)
        acc = part if acc is None else acc + part
    return h + (0.5 * mod[idx + 2:idx + 3]) * acc


def _ffn_kernel(*refs, idx, n_ctx_tiles):
    mod_ref, w1_ref, w2_ref, o_ref = refs[-4:]
    h = refs[-5][...]
    if n_ctx_tiles:
        h = jnp.where(pl.program_id(0) < n_ctx_tiles, refs[0][...], h)
    o_ref[...] = _ffn_half_step(h, mod_ref[0], idx, w1_ref, w2_ref)


def _ffn(h, mod, w1, w2, idx, lay):
    tm, d = lay["tm"], D_MODEL
    pair = isinstance(h, tuple)
    nct = lay["nc"] // tm
    n_tiles = (h[0].shape[0] + h[1].shape[0] if pair else h.shape[0]) // tm
    group = functools.partial(_mod_group, lay=lay, off=0)
    if pair:
        h_specs = [pl.BlockSpec((tm, d), lambda i: (jnp.minimum(i, nct - 1), 0)),
                   pl.BlockSpec((tm, d), lambda i: (jnp.maximum(i - nct, 0), 0))]
    else:
        h_specs = [pl.BlockSpec((tm, d), lambda i: (i, 0))]
    return pl.pallas_call(
        functools.partial(_ffn_kernel, idx=idx, n_ctx_tiles=nct if pair else 0),
        grid=(n_tiles,),
        in_specs=h_specs + [pl.BlockSpec((1, 9, d), lambda i: (group(i), 0, 0)),
                            pl.BlockSpec((d, 2 * D_FF), lambda i: (0, 0), pipeline_mode=pl.Buffered(1)),
                            pl.BlockSpec((D_FF, d), lambda i: (0, 0), pipeline_mode=pl.Buffered(1))],
        out_specs=pl.BlockSpec((tm, d), lambda i: (i, 0)),
        out_shape=jax.ShapeDtypeStruct((n_tiles * tm, d), f32),
        compiler_params=_cparams(1), name=f"ffn_half_step_{idx}",
    )(*(h if pair else (h,)), mod, w1, w2)


def _mod_group(i, lay, off):
    r = (i + off) * lay["tm"]
    return jnp.where(r < lay["nc"], lay["b"], (r - lay["nc"]) // lay["seq"])


def _rope(x, cos, sin, quarter):
    w = x.shape[1]
    reps = w // LANE
    if reps > 1:
        cos = jnp.concatenate([cos] * reps, axis=1)
        sin = jnp.concatenate([sin] * reps, axis=1)
    first = (_iota((1, w), 1) % (2 * quarter)) < quarter
    swapped = jnp.where(first, pltpu.roll(x, w - quarter, 1), pltpu.roll(x, quarter, 1))
    return x * cos + swapped * sin


def _inproj_kernel(h_ref, mod_ref, w_ref, nw_ref, ca_ref, sa_ref, cb_ref, sb_ref, bd32_ref, bd64_ref,
                   qa_o, ka_o, va_o, qb_o, kb_o, vb_o, cq_o, gate_o, a_o, b_o):
    mod = mod_ref[0]
    hn = _modulated_norm(h_ref[...], mod[3:4], mod[4:5]).astype(bf16)

    def proj(name):
        off, width = SEG[name]
        cols = [jnp.dot(hn, w_ref[:, c:min(c + 256, off + width)], preferred_element_type=f32)
                for c in range(off, off + width, 256)]
        return cols[0] if len(cols) == 1 else jnp.concatenate(cols, axis=1)

    def normed(x, bd, group, wrow):
        s = _group_sum(x * x, bd, terms=1)
        return x * lax.rsqrt(s * (1.0 / group) + EPS) * wrow

    nw = nw_ref[...]
    bd32, bd64 = bd32_ref[...], bd64_ref[...]
    ca, sa, cb, sb = ca_ref[...], sa_ref[...], cb_ref[...], sb_ref[...]
    qa_o[...] = _rope(normed(proj("qa"), bd32, A_DIM, nw[0:1, :256]), ca, sa, A_DIM // 4).astype(bf16)
    ka_o[...] = _rope(normed(proj("ka"), bd32, A_DIM, nw[1:2, :256]), ca, sa, A_DIM // 4).astype(bf16)
    va_o[...] = proj("va").astype(bf16)
    qb = proj("qb")
    wq = jnp.concatenate([nw[2:3, :256]] * (B_WIDE // 256), axis=1)
    qb_o[...] = _rope(normed(qb, bd64, B_DIM, wq), cb, sb, B_DIM // 4).astype(bf16)
    kb_o[...] = _rope(normed(proj("kb"), bd64[:LANE, :LANE], B_DIM, nw[3:4, :LANE]), cb, sb, B_DIM // 4).astype(bf16)
    vb_o[...] = proj("vb").astype(bf16)
    cq_o[...] = proj("cq")
    gate_o[...] = proj("gate")
    a_o[...] = proj("a")
    b_o[...] = proj("b")


def _inproj(h, mod, w_in, nw, tabs, bd32, bd64, lay):
    tm, d = lay["tm"], D_MODEL
    n = h.shape[0]
    n_tiles = n // tm
    nct = lay["nc"] // tm
    spt = lay["seq"] // tm
    group = functools.partial(_mod_group, lay=lay, off=0)

    def tab_idx(i):
        return jnp.where(i < nct, 0, 1 + (i - nct) % spt)

    row = lambda w: pl.BlockSpec((tm, w), lambda i: (i, 0))
    const = lambda shape: pl.BlockSpec(shape, lambda i: (0,) * len(shape))
    tab = pl.BlockSpec((tm, LANE), lambda i: (tab_idx(i), 0))
    names = ("qa", "ka", "va", "qb", "kb", "vb", "cq", "gate", "a", "b")
    dts = (bf16,) * 6 + (f32,) * 4
    return pl.pallas_call(
        _inproj_kernel,
        grid=(n_tiles,),
        in_specs=[row(d), pl.BlockSpec((1, 9, d), lambda i: (group(i), 0, 0)),
                  pl.BlockSpec((d, IN_WIDE), lambda i: (0, 0), pipeline_mode=pl.Buffered(1)),
                  const(nw.shape), tab, tab, tab, tab, const(bd32.shape), const(bd64.shape)],
        out_specs=[row(SEG[k][1]) for k in names],
        out_shape=[jax.ShapeDtypeStruct((n, SEG[k][1]), dt) for k, dt in zip(names, dts)],
        compiler_params=_cparams(1), name="mixer_in_proj",
    )(h, mod, w_in, nw, *tabs, bd32, bd64)


A_TQ = 512
A_KBLOCK = 256
A_INTERLEAVE = 8


def _attn_a_kernel(lam_ref, q_ref, *refs, n_seg):
    k_refs, v_refs = refs[:n_seg], refs[n_seg:2 * n_seg]
    o_ref, vt_ref = refs[2 * n_seg], refs[2 * n_seg + 1]
    tq = q_ref.shape[0]
    n_maps = 2 * A_HEADS

    @pl.when(pl.program_id(1) == 0)
    def _():
        off = 0
        for v_ref in v_refs:
            for r in range(0, v_ref.shape[0], A_KBLOCK):
                vt_ref[:, off + r:off + r + A_KBLOCK] = v_ref[r:r + A_KBLOCK, :].astype(f32).T.astype(bf16)
            off += v_ref.shape[0]

    lam = lam_ref[0]
    qt = q_ref[...].astype(f32).T
    feat = _iota((A_W, tq), 0)
    qms = [jnp.where(feat // A_DIM == n, qt, 0.0).astype(bf16) for n in range(n_maps)]
    normed = [None] * n_maps
    for g0 in range(0, n_maps, A_INTERLEAVE):
        grp = range(g0, g0 + A_INTERLEAVE)
        m_run = {n: jnp.full((1, tq), NEG_INF, f32) for n in grp}
        l_run = {n: jnp.zeros((1, tq), f32) for n in grp}
        acc = {n: jnp.zeros((A_VDIM, tq), f32) for n in grp}
        off = 0
        for k_ref in k_refs:
            for r in range(0, k_ref.shape[0], A_KBLOCK):
                kb = k_ref[r:r + A_KBLOCK, :]
                st = {n: jnp.dot(kb, qms[n], preferred_element_type=f32) for n in grp}
                m_new = {n: jnp.maximum(m_run[n], jnp.max(st[n], axis=0, keepdims=True)) for n in grp}
                alpha = {n: jnp.exp2(m_run[n] - m_new[n]) for n in grp}
                e = {n: jnp.exp2(st[n] - m_new[n]) for n in grp}
                l_run = {n: alpha[n] * l_run[n] + jnp.sum(e[n], axis=0, keepdims=True) for n in grp}
                pv = {n: jnp.dot(vt_ref[(n // 2) * A_VDIM:(n // 2 + 1) * A_VDIM, off + r:off + r + A_KBLOCK],
                                 e[n].astype(bf16), preferred_element_type=f32) for n in grp}
                acc = {n: alpha[n] * acc[n] + pv[n] for n in grp}
                m_run = m_new
            off += k_ref.shape[0]
        for n in grp:
            normed[n] = acc[n] * ((lam if n % 2 else 1.0) / l_run[n])
    heads = [normed[2 * h] - normed[2 * h + 1] for h in range(A_HEADS)]
    o_ref[...] = jnp.concatenate(heads, axis=0).T


def _attn_a(lam, qa, ka, va, lay, latent):
    b, ctx, seq, nc = lay["b"], lay["ctx"], lay["seq"], lay["nc"]
    smem = pl.BlockSpec(memory_space=pltpu.SMEM)
    if latent:
        tq = A_TQ
        qpb = seq // tq
        grid = (b, qpb)
        qspec = pl.BlockSpec((tq, A_W), lambda i, j: (nc // tq + i * qpb + j, 0))
        kv = [pl.BlockSpec((ctx, A_W), lambda i, j: (i, 0)), pl.BlockSpec((seq, A_W), lambda i, j: (nc // seq + i, 0))]
        ospec = pl.BlockSpec((tq, A_W), lambda i, j: (i * qpb + j, 0))
        rows = b * seq
    else:
        grid = (b, 1)
        qspec = pl.BlockSpec((ctx, A_W), lambda i, j: (i, 0))
        kv = [pl.BlockSpec((ctx, A_W), lambda i, j: (i, 0))]
        ospec = qspec
        rows = nc
    n_seg = len(kv)
    return pl.pallas_call(
        functools.partial(_attn_a_kernel, n_seg=n_seg),
        grid=grid,
        in_specs=[smem, qspec] + kv + kv,
        out_specs=ospec,
        out_shape=jax.ShapeDtypeStruct((rows, A_W), f32),
        scratch_shapes=[pltpu.VMEM((A_W, ctx + seq if latent else ctx), bf16)],
        compiler_params=_cparams(2), name="diff_attention_lat" if latent else "diff_attention_ctx",
    )(lam, qa, *([ka] * n_seg), *([va] * n_seg))


B_QSUB = 4


def _attn_b_kernel(sink_ref, q_ref, kc_ref, vc_ref, *refs, latent):
    o_ref = refs[-1]
    lane = _iota((1, LANE), 1)
    n_ctx = kc_ref.shape[0]
    kc, vc = kc_ref[...], vc_ref[...]
    rows = B_BLOCK if latent else q_ref.shape[0]
    n_sub = q_ref.shape[0] // rows
    for sub in range(n_sub):
        q = q_ref[sub * rows:(sub + 1) * rows, :]
        keys, vals, band_ok = kc, vc, None
        if latent:
            kl_ref, vl_ref = refs[0], refs[1]
            t = pl.program_id(1) * n_sub + sub
            nb = pl.num_programs(1) * n_sub
            starts = [pl.multiple_of(jnp.clip(t + off, 0, nb - 1) * B_BLOCK, B_BLOCK) for off in (-1, 0, 1)]
            keys = jnp.concatenate([kc] + [kl_ref[pl.ds(st, B_BLOCK), :] for st in starts], axis=0)
            vals = jnp.concatenate([vc] + [vl_ref[pl.ds(st, B_BLOCK), :] for st in starts], axis=0)
            r = _iota((B_GROUP * rows, 3 * B_BLOCK), 0) % rows
            c = _iota((B_GROUP * rows, 3 * B_BLOCK), 1)
            cc = c % B_BLOCK
            before = jnp.logical_and(jnp.logical_and(c < B_BLOCK, cc >= r), t >= 1)
            after = jnp.logical_and(jnp.logical_and(c >= 2 * B_BLOCK, cc <= r), t + 1 < nb)
            same = jnp.logical_and(c >= B_BLOCK, c < 2 * B_BLOCK)
            band_ok = jnp.logical_or(same, jnp.logical_or(before, after))
        for g in range(B_KV_HEADS):
            heads = range(g * B_GROUP, (g + 1) * B_GROUP)
            qg = jnp.concatenate([q[:, h * LANE:(h + 1) * LANE] for h in heads], axis=0)
            sink = jnp.concatenate([jnp.full((rows, 1), sink_ref[h], f32) for h in heads], axis=0)
            x = _dot_nt(qg, keys)
            s = [x] if band_ok is None else [x[:, :n_ctx], jnp.where(band_ok, x[:, n_ctx:], NEG_INF)]
            mx = functools.reduce(jnp.maximum, [jnp.max(x, axis=-1, keepdims=True) for x in s] + [sink])
            e = [jnp.exp2(x - mx) for x in s]
            den = functools.reduce(jnp.add, [jnp.sum(x, axis=-1, keepdims=True) for x in e]) + jnp.exp2(sink - mx)
            p = e[0] if len(e) == 1 else jnp.concatenate(e, axis=1)
            og = jnp.dot(p.astype(bf16), vals, preferred_element_type=f32) * (1.0 / den)
            og = jnp.where(lane // B_DIM == g, og, 0.0).astype(bf16)
            for n, h in enumerate(heads):
                o_ref[sub * rows:(sub + 1) * rows, h * LANE:(h + 1) * LANE] = og[n * rows:(n + 1) * rows]


def _attn_b(sink, qb, kb, vb, lay, latent):
    b, ctx, seq, nc = lay["b"], lay["ctx"], lay["seq"], lay["nc"]
    smem = pl.BlockSpec(memory_space=pltpu.SMEM)
    cspec = pl.BlockSpec((ctx, LANE), lambda i, j: (i, 0))
    if latent:
        tq = B_QSUB * B_BLOCK
        nb = seq // tq
        grid = (b, nb)
        qspec = pl.BlockSpec((tq, B_WIDE), lambda i, j: (nc // tq + i * nb + j, 0))
        lspec = pl.BlockSpec((seq, LANE), lambda i, j: (nc // seq + i, 0))
        in_specs = [smem, qspec, cspec, cspec, lspec, lspec]
        args = (sink, qb, kb, vb, kb, vb)
        ospec = pl.BlockSpec((tq, B_WIDE), lambda i, j: (i * nb + j, 0))
        rows = b * seq
    else:
        grid = (b, 1)
        qspec = pl.BlockSpec((ctx, B_WIDE), lambda i, j: (i, 0))
        in_specs = [smem, qspec, cspec, cspec]
        args = (sink, qb, kb, vb)
        ospec = qspec
        rows = nc
    return pl.pallas_call(
        functools.partial(_attn_b_kernel, latent=latent),
        grid=grid, in_specs=in_specs, out_specs=ospec,
        out_shape=jax.ShapeDtypeStruct((rows, B_WIDE), bf16),
        compiler_params=_cparams(2), name="window_attention_lat" if latent else "sink_attention_ctx",
    )(*args)


HALO = 8


def _gdn_prep_kernel(x_ref, prev_ref, next_ref, a_ref, b_ref, cw_ref, par_ref, bd64_ref,
                     q_o, k_o, v_o, g_o, beta_o, *, nct, cpt, spt):
    i = pl.program_id(0)
    tm = x_ref.shape[0]
    j = jnp.where(i < nct, i % cpt, (i - nct) % spt)
    per_seq = jnp.where(i < nct, cpt, spt)
    first = j == 0
    last = j == per_seq - 1
    prev = jnp.where(first, 0.0, prev_ref[...])
    nxt = jnp.where(last, 0.0, next_ref[...])
    xx = jnp.concatenate([prev, x_ref[...], nxt], axis=0)
    cw = cw_ref[...]
    rows = xx.shape[0]
    y = None
    for tap in range(C_CONV):
        shift = (C_CONV // 2 - tap) % rows
        sh = xx if shift == 0 else pltpu.roll(xx, shift, 0)
        term = sh[HALO:HALO + tm] * cw[tap:tap + 1]
        y = term if y is None else y + term
    y = _silu(y)
    bd64 = bd64_ref[...]

    def l2n(t):
        return t * lax.rsqrt(_group_sum(t * t, bd64, terms=1) + EPS)

    q_o[...] = (l2n(y[:, :C_W]) * (C_DK ** -0.5)).astype(bf16)
    k_o[...] = l2n(y[:, C_W:2 * C_W]).astype(bf16)
    v_o[...] = y[:, 2 * C_W:].astype(bf16)
    par = par_ref[...]
    z = a_ref[...] + par[1:2]
    softplus = jnp.maximum(z, 0.0) + jnp.log1p(jnp.exp(-jnp.abs(z)))
    g_o[...] = -par[0:1] * softplus
    beta_o[...] = jax.nn.sigmoid(b_ref[...])


def _gdn_prep(cq, a, bb, cw, par, bd64, lay):
    tm = 256
    n = cq.shape[0]
    n_tiles = n // tm
    nct, spt = lay["nc"] // tm, lay["seq"] // tm
    hb = tm // HALO
    last_blk = n // HALO - 1
    row = lambda w: pl.BlockSpec((tm, w), lambda i: (i, 0))
    const = lambda shape: pl.BlockSpec(shape, lambda i: (0,) * len(shape))
    return pl.pallas_call(
        functools.partial(_gdn_prep_kernel, nct=nct, cpt=lay["ctx"] // tm, spt=spt),
        grid=(n_tiles,),
        in_specs=[row(3 * C_W),
                  pl.BlockSpec((HALO, 3 * C_W), lambda i: (jnp.maximum(i * hb - 1, 0), 0)),
                  pl.BlockSpec((HALO, 3 * C_W), lambda i: (jnp.minimum((i + 1) * hb, last_blk), 0)),
                  row(LANE), row(LANE), const(cw.shape), const(par.shape), const(bd64.shape)],
        out_specs=[row(C_W), row(C_W), row(C_W), row(LANE), row(LANE)],
        out_shape=[jax.ShapeDtypeStruct((n, C_W), bf16)] * 3 + [jax.ShapeDtypeStruct((n, LANE), f32)] * 2,
        compiler_params=_cparams(1), name="gdn_inputs",
    )(cq, cq, cq, a, bb, cw, par, bd64)


GW = 4 * C_DK


def _bd(x):
    t = jnp.concatenate([x.astype(bf16)] * 4, axis=0)
    same = _iota((GW, GW), 0) // C_DK == _iota((GW, GW), 1) // C_DK
    return jnp.where(same, t, jnp.zeros_like(t))


N_GROUPS = 3
GDN_W = N_GROUPS * GW
PREP_UNROLL = 4


def _per_head_lanes(x, first):
    rows = x.shape[0]
    left = _iota((1, LANE), 1) < C_DK
    cols = []
    for pair in range(C_HEADS // 2):
        a = jnp.broadcast_to(x[:, first + 2 * pair:first + 2 * pair + 1], (rows, LANE))
        b = jnp.broadcast_to(x[:, first + 2 * pair + 1:first + 2 * pair + 2], (rows, LANE))
        cols.append(jnp.where(left, a, b))
    return jnp.concatenate(cols, axis=1)


def _per_head_rows(x, first):
    rows = x.shape[0]
    xt = jnp.concatenate([x, jnp.zeros_like(x)], axis=0).T
    xt = xt + pltpu.roll(xt, C_DK, 1)
    left = _iota((1, LANE), 1) < C_DK
    cols = []
    for pair in range(C_HEADS // 2):
        a = jnp.broadcast_to(xt[first + 2 * pair:first + 2 * pair + 1, :], (rows, LANE))
        b = jnp.broadcast_to(xt[first + 2 * pair + 1:first + 2 * pair + 2, :], (rows, LANE))
        cols.append(jnp.where(left, a, b))
    return jnp.concatenate(cols, axis=1)


def _gdn_prepare(groups, fwd_lanes):
    L = C_CHUNK
    n = range(len(groups))
    ri = _iota((L, GW), 0)
    ci = _iota((L, GW), 1) % C_DK
    eye = ci == ri
    ahead = [jnp.where(f, ri - ci, ci - ri) for f in fwd_lanes]
    qs, ks, vs, gcs, betas, gr = zip(*groups)
    kq =[_dot_nt(jnp.concatenate([ks[i], qs[i]], axis=0), _bd(ks[i])) for i in n]
    dec = [jnp.exp(jnp.where(ahead[i] >= 0, gcs[i] - gr[i], NEG_INF)) for i in n]
    a = [jnp.where(ahead[i] > 0, betas[i] * kq[i][:L] * dec[i], 0.0) for i in n]
    tm = [jnp.where(eye, 1.0, 0.0) - jnp.where(ri // 2 == ci // 2, a[i], 0.0) for i in n]
    s = 2
    while s < L:
        off = jnp.logical_and(ri // (2 * s) == ci // (2 * s), ri // s != ci // s)
        y = [_dot(jnp.where(off, a[i], 0.0), _bd(tm[i])) for i in n]
        tm = [tm[i] - _dot(tm[i], _bd(y[i])) for i in n]
        s *= 2
    eg = [jnp.exp(gcs[i]) for i in n]
    uw = [_dot(tm[i], jnp.concatenate([_bd(betas[i] * vs[i]), _bd(betas[i] * eg[i] * ks[i])], axis=1)) for i in n]
    out = []
    for i in n:
        g_last = jnp.where(fwd_lanes[i], gcs[i][L - 1:L], gcs[i][0:1])
        wq = jnp.concatenate([uw[i][:, GW:], qs[i] * eg[i]], axis=0).astype(bf16)
        qk = jnp.where(ahead[i] >= 0, kq[i][L:] * dec[i], 0.0).astype(bf16)
        kd = (ks[i] * jnp.exp(g_last - gcs[i])).astype(bf16)
        out.append((uw[i][:, :GW], wq, qk, kd, jnp.exp(g_last)))
    return out


def _gdn_scan(chunks, s_ref):
    L = C_CHUNK
    n = range(len(chunks))
    same = _iota((GW, GW), 0) // C_DK == _iota((GW, GW), 1) // C_DK
    s = [s_ref[i] for i in n]
    ws = [_dot(chunks[i][1], s[i]) for i in n]
    v_new = [chunks[i][0] - ws[i][:L] for i in n]
    o = [ws[i][L:] + _dot(chunks[i][2], _bd(v_new[i])) for i in n]
    upd = [_dot_tn(chunks[i][3], v_new[i]) for i in n]
    for i in n:
        s_ref[i] = s[i] * chunks[i][4] + jnp.where(same, upd[i], 0.0)
    return o


def _gdn_kernel(qc, kc, vc, gc, bc, ql, kl, vl, gl, bl, oc_ref, ol_ref, s_ref, *slot_refs):
    L = C_CHUNK
    slots = (slot_refs[:5], slot_refs[5:])
    s_ref[...] = jnp.zeros(s_ref.shape, f32)
    lane = _iota((1, GW), 1)
    fwd_lanes = [lane >= 0, lane < 0, lane < LANE]
    r64 = _iota((L, L), 0)
    c64 = _iota((L, L), 1)
    tri_l = (c64 <= r64).astype(bf16)
    tri_u = (c64 >= r64).astype(bf16)

    nb = PREP_UNROLL

    def prepare_block(refs, blk, slot):
        q_ref, k_ref, v_ref, g_ref, b_ref = refs
        n_chunks = q_ref.shape[0] // L
        groups = []
        for sub in range(nb):
            t = blk * nb + sub
            sl_f = pl.ds(pl.multiple_of(t * L, L), L)
            sl_b = pl.ds(pl.multiple_of((n_chunks - 1 - t) * L, L), L)
            gc_f = _dot_lx(tri_l, g_ref[sl_f, :])
            gc_b = _dot_lx(tri_u, g_ref[sl_b, :])
            data_f = ([r[sl_f, :].astype(f32) for r in (q_ref, k_ref, v_ref)]
                      + [_per_head_lanes(gc_f, 0), _per_head_lanes(b_ref[sl_f, :], 0), _per_head_rows(gc_f, 0)])
            data_b = ([r[sl_b, :].astype(f32) for r in (q_ref, k_ref, v_ref)]
                      + [_per_head_lanes(gc_b, C_HEADS), _per_head_lanes(b_ref[sl_b, :], C_HEADS),
                         _per_head_rows(gc_b, C_HEADS)])
            groups += [tuple(x[:, :GW] for x in data_f), tuple(x[:, :GW] for x in data_b),
                       tuple(jnp.concatenate([xf[:, GW:], xb[:, GW:]], axis=1) for xf, xb in zip(data_f, data_b))]
        u_s, wq_s, qk_s, kd_s, dl_s = slots[slot]
        for n, (u, wq, qk, kd, dl) in enumerate(_gdn_prepare(groups, fwd_lanes * nb)):
            it = n // N_GROUPS
            cols = slice((n % N_GROUPS) * GW, (n % N_GROUPS + 1) * GW)
            u_s[it * L:(it + 1) * L, cols] = u
            wq_s[it * 2 * L:(it + 1) * 2 * L, cols] = wq
            qk_s[it * L:(it + 1) * L, cols] = qk
            kd_s[it * L:(it + 1) * L, cols] = kd
            dl_s[it * 8:(it + 1) * 8, cols] = jnp.broadcast_to(dl, (8, GW))

    def scan_block(o_ref, blk, slot):
        n_chunks = o_ref.shape[0] // L
        u_s, wq_s, qk_s, kd_s, dl_s = slots[slot]
        for sub in range(nb):
            t = blk * nb + sub
            sl_f = pl.ds(pl.multiple_of(t * L, L), L)
            sl_b = pl.ds(pl.multiple_of((n_chunks - 1 - t) * L, L), L)
            r1 = slice(sub * L, (sub + 1) * L)
            r2 = slice(sub * 2 * L, (sub + 1) * 2 * L)
            r8 = slice(sub * 8, sub * 8 + 1)
            chunks = []
            for i in range(N_GROUPS):
                cols = slice(i * GW, (i + 1) * GW)
                chunks.append((u_s[r1, cols], wq_s[r2, cols], qk_s[r1, cols], kd_s[r1, cols], dl_s[r8, cols]))
            o0, o1, o2 = _gdn_scan(chunks, s_ref)
            o_ref[sl_f, :] += jnp.concatenate([o0, o2[:, :LANE]], axis=1)
            o_ref[sl_b, :] += jnp.concatenate([o1, o2[:, LANE:]], axis=1)

    ctx_refs, lat_refs = (qc, kc, vc, gc, bc), (ql, kl, vl, gl, bl)
    cb, lb = qc.shape[0] // (L * nb), ql.shape[0] // (L * nb)
    oc_ref[...] = jnp.zeros(oc_ref.shape, f32)
    ol_ref[...] = jnp.zeros(ol_ref.shape, f32)
    prepare_block(ctx_refs, 0, 0)
    for j in range(cb):
        if j + 1 < cb:
            prepare_block(ctx_refs, j + 1, (j + 1) % 2)
        else:
            prepare_block(lat_refs, 0, (j + 1) % 2)
        scan_block(oc_ref, j, j % 2)

    def body(pair, carry):
        for half in range(2):
            tt = 2 * pair + half
            scan_block(ol_ref, tt, (cb + half) % 2)
            prepare_block(lat_refs, tt + 1, (cb + half + 1) % 2)
        return carry

    lax.fori_loop(0, (lb - 1) // 2, body, 0)
    for tt in range(2 * ((lb - 1) // 2), lb - 1):
        prepare_block(lat_refs, tt + 1, (cb + tt + 1) % 2)
        scan_block(ol_ref, tt, (cb + tt) % 2)
    scan_block(ol_ref, lb - 1, (cb + lb - 1) % 2)


def _gdn(q, k, v, g, beta, lay):
    b, ctx, seq, nc = lay["b"], lay["ctx"], lay["seq"], lay["nc"]
    cs = lambda w: pl.BlockSpec((ctx, w), lambda i: (i, 0))
    ls = lambda w: pl.BlockSpec((seq, w), lambda i: (nc // seq + i, 0))
    widths = (C_W, C_W, C_W, LANE, LANE)
    n_it = PREP_UNROLL
    assert ctx % (C_CHUNK * PREP_UNROLL) == 0 and seq % (C_CHUNK * PREP_UNROLL) == 0
    return pl.pallas_call(
        _gdn_kernel,
        grid=(b,),
        in_specs=[cs(w) for w in widths] + [ls(w) for w in widths],
        out_specs=[pl.BlockSpec((ctx, C_W), lambda i: (i, 0)), pl.BlockSpec((seq, C_W), lambda i: (i, 0))],
        out_shape=[jax.ShapeDtypeStruct((nc, C_W), f32), jax.ShapeDtypeStruct((b * seq, C_W), f32)],
        scratch_shapes=[pltpu.VMEM((N_GROUPS, GW, GW), f32)] + 2 * [
            pltpu.VMEM((n_it * C_CHUNK, GDN_W), f32),
            pltpu.VMEM((n_it * 2 * C_CHUNK, GDN_W), bf16),
            pltpu.VMEM((n_it * C_CHUNK, GDN_W), bf16),
            pltpu.VMEM((n_it * C_CHUNK, GDN_W), bf16),
            pltpu.VMEM((n_it * 8, GDN_W), f32)],
        compiler_params=_cparams(1), name="gated_deltanet",
    )(q, k, v, g, beta, q, k, v, g, beta)


def _outproj_kernel(h_ref, mod_ref, *refs, n_ctx_tiles):
    gate_ref, w_ref, nw_ref, bd64_ref, w1_ref, w2_ref, o_ref = refs[-7:]
    mixed = [r[...] for r in refs[-10:-7]]
    if n_ctx_tiles:
        is_ctx = pl.program_id(0) < n_ctx_tiles
        mixed = [jnp.where(is_ctx, c[...], x) for c, x in zip(refs[:3], mixed)]
    oa, ob, oc = mixed
    bd64 = bd64_ref[...]
    nw = nw_ref[...]

    def normed(x, wrow):
        s = _group_sum(x * x, bd64, terms=1)
        return x * lax.rsqrt(s * (1.0 / C_DK) + EPS) * wrow

    ya = normed(oa, nw[0:1, :A_W]).astype(bf16)
    yc = (normed(oc, nw[1:2, :C_W]) * _silu(gate_ref[...])).astype(bf16)
    y = jnp.dot(ya, w_ref[:A_W, :], preferred_element_type=f32)
    y += jnp.dot(ob, w_ref[A_W:A_W + B_WIDE, :], preferred_element_type=f32)
    y += jnp.dot(yc, w_ref[A_W + B_WIDE:, :], preferred_element_type=f32)
    mod = mod_ref[0]
    o_ref[...] = _ffn_half_step(h_ref[...] + mod[5:6] * y, mod, 6, w1_ref, w2_ref)


def _outproj_ffn(h, mod, ctx_outs, lat_outs, gate, w_out, nw, bd128, w1, w2, lay):
    tm, d = lay["tm"], D_MODEL
    nct = lay["nc"] // tm
    off = 0 if ctx_outs else nct
    n_tiles = h.shape[0] // tm - off
    group = functools.partial(_mod_group, lay=lay, off=off)
    rowo = lambda w: pl.BlockSpec((tm, w), lambda i: (i + off, 0))
    rowc = lambda w: pl.BlockSpec((tm, w), lambda i: (jnp.minimum(i, nct - 1), 0))
    rowl = lambda w: pl.BlockSpec((tm, w), lambda i: (jnp.maximum(i + off - nct, 0), 0))
    const = lambda shape: pl.BlockSpec(shape, lambda i: (0,) * len(shape))
    widths = (A_W, B_WIDE, C_W)
    ctx_specs = [rowc(w) for w in widths] if ctx_outs else []
    return pl.pallas_call(
        functools.partial(_outproj_kernel, n_ctx_tiles=nct if ctx_outs else 0),
        grid=(n_tiles,),
        in_specs=[rowo(d), pl.BlockSpec((1, 9, d), lambda i: (group(i), 0, 0))] + ctx_specs
                 + [rowl(w) for w in widths]
                 + [rowo(C_W), pl.BlockSpec((OUT_WIDE, d), lambda i: (0, 0), pipeline_mode=pl.Buffered(1)),
                    const(nw.shape), const(bd128.shape),
                    pl.BlockSpec((d, 2 * D_FF), lambda i: (0, 0), pipeline_mode=pl.Buffered(1)),
                    pl.BlockSpec((D_FF, d), lambda i: (0, 0), pipeline_mode=pl.Buffered(1))],
        out_specs=pl.BlockSpec((tm, d), lambda i: (i, 0)),
        out_shape=jax.ShapeDtypeStruct((n_tiles * tm, d), f32),
        compiler_params=_cparams(1), name="mixer_out_proj_ffn",
    )(h, mod, *(ctx_outs or ()), *lat_outs, gate, w_out, nw, bd128, w1, w2)


def _block_ones(n, group):
    idx = np.arange(n) // group
    return jnp.asarray(idx[:, None] == idx[None, :], dtype=bf16)


def _rope_table(seq, d, pad_rows):
    half, quarter = d // 2, d // 4
    rows = seq // GRID_W
    row = jnp.repeat(jnp.arange(rows, dtype=f32), GRID_W)
    col = jnp.tile(jnp.arange(GRID_W, dtype=f32), rows)
    inv = ROPE_THETA ** (-jnp.arange(0, half, 2, dtype=f32) / half)
    ld = np.arange(LANE) % d
    pos = jnp.where(jnp.asarray(ld < half)[None, :], row[:, None], col[:, None])
    ang = pos * inv[np.asarray((ld % half) % quarter)][None, :]
    sign = jnp.asarray(np.where((ld % half) < quarter, -1.0, 1.0), dtype=f32)[None, :]
    cos = jnp.concatenate([jnp.ones((pad_rows, LANE), f32), jnp.cos(ang)], axis=0)
    sin = jnp.concatenate([jnp.zeros((pad_rows, LANE), f32), jnp.sin(ang) * sign], axis=0)
    return cos, sin


def _arrange_w_in(w):
    offs = np.concatenate([[0], np.cumsum(IN_SIZES)])
    part = lambda n: w[:, offs[n]:offs[n + 1]]
    zeros = lambda width: jnp.zeros((w.shape[0], width), w.dtype)
    qb = part(3)
    pieces = [part(0), part(1), part(2)]
    for h in range(B_HEADS):
        head = qb[:, h * B_DIM:(h + 1) * B_DIM]
        pieces += [head, zeros(B_DIM)] if h // B_GROUP == 0 else [zeros(B_DIM), head]
    pieces += [part(4), part(5), part(6), part(7), part(8), zeros(LANE - IN_SIZES[8]), part(9), zeros(LANE - IN_SIZES[9])]
    out = jnp.concatenate(pieces, axis=1).astype(bf16)
    assert out.shape[1] == IN_WIDE
    return out


def _arrange_w_out(w):
    zeros = jnp.zeros((B_DIM, w.shape[1]), w.dtype)
    pieces = [w[:A_W]]
    for h in range(B_HEADS):
        head = w[A_W + h * B_DIM:A_W + (h + 1) * B_DIM]
        pieces += [head, zeros] if h // B_GROUP == 0 else [zeros, head]
    pieces.append(w[A_W + B_HEADS * B_DIM:])
    out = jnp.concatenate(pieces, axis=0).astype(bf16)
    assert out.shape[0] == OUT_WIDE
    return out


def _pad_lanes(x, width):
    return jnp.pad(x, ((0, 0), (0, width - x.shape[1])))


def kernel(x, c, ctx, c_ctx, w_mod, b_mod, ffn1_w1, ffn1_w2, ffn2_w1, ffn2_w2, w_in, w_out,
           a_qnorm, a_knorm, a_lambda, a_subln, b_qnorm, b_knorm, b_sink,
           c_conv, c_A_log, c_dt_bias, c_onorm):
    b, seq, d = x.shape
    n_ctx = ctx.shape[1]
    nc = b * n_ctx
    tm = 512 if (nc % 512 == 0 and seq % 512 == 0) else 256
    lay = dict(b=b, ctx=n_ctx, seq=seq, nc=nc, tm=tm)
    assert d == D_MODEL and seq % 256 == 0 and n_ctx % 256 == 0 and nc % seq == 0 and seq % GRID_W == 0

    h = (ctx.reshape(nc, d), x.reshape(b * seq, d))
    cvec = jnp.zeros((16, d), f32).at[:b].set(c).at[b].set(c_ctx)
    mod_all = _modulation(cvec, w_mod, b_mod).reshape(DEPTH, 16, 9, d)

    bd32, bd64, bd64s = _block_ones(256, A_DIM), _block_ones(256, B_DIM), _block_ones(LANE, C_DK)
    tabs = _rope_table(seq, A_DIM, tm) + _rope_table(seq, B_DIM, tm)

    for l in range(DEPTH):
        last = l == DEPTH - 1
        lam_init = 0.8 - 0.6 * float(np.exp(-0.3 * l))
        mod = mod_all[l]
        lf = a_lambda[l].astype(f32)
        lam = (jnp.exp(jnp.sum(lf[0] * lf[1])) - jnp.exp(jnp.sum(lf[2] * lf[3])) + lam_init).reshape(1)
        nw_in = jnp.stack([jnp.tile(a_qnorm[l], 8) * (A_DIM ** -0.5 * LOG2E),
                           jnp.tile(a_knorm[l], 8),
                           jnp.tile(b_qnorm[l], 4) * (B_DIM ** -0.5 * LOG2E),
                           jnp.tile(b_knorm[l], 4)] + [jnp.zeros((256,), f32)] * 4)
        nw_out = jnp.stack([_pad_lanes((jnp.tile(a_subln[l], 4) * (1.0 - lam_init))[None], C_W)[0],
                            jnp.tile(c_onorm[l], C_HEADS)] + [jnp.zeros((C_W,), f32)] * 6)
        gpar = jnp.stack([_pad_lanes(jnp.exp(c_A_log[l].astype(f32)).reshape(1, -1), LANE)[0],
                          _pad_lanes(c_dt_bias[l].astype(f32).reshape(1, -1), LANE)[0]] + [jnp.zeros((LANE,), f32)] * 6)
        cw = jnp.concatenate([c_conv[l], jnp.zeros((8 - C_CONV, 3 * C_W), f32)], axis=0)

        h = _ffn(h, mod, ffn1_w1[l].astype(bf16), ffn1_w2[l].astype(bf16), 0, lay)
        qa, ka, va, qb, kb, vb, cq, gate, pa, pb = _inproj(h, mod, _arrange_w_in(w_in[l]), nw_in, tabs, bd32, bd64, lay)
        sink = b_sink[l].astype(f32) * LOG2E
        oa = _attn_a(lam, qa, ka, va, lay, True)
        ob = _attn_b(sink, qb, kb, vb, lay, True)
        gq, gk, gv, gg, gbeta = _gdn_prep(cq, pa, pb, cw, gpar, bd64s, lay)
        oc_ctx, oc = _gdn(gq, gk, gv, gg, gbeta, lay)
        ctx_outs = None
        if not last:
            ctx_outs = (_attn_a(lam, qa, ka, va, lay, False), _attn_b(sink, qb, kb, vb, lay, False), oc_ctx)
        h = _outproj_ffn(h, mod, ctx_outs, (oa, ob, oc), gate, _arrange_w_out(w_out[l]), nw_out, bd64s,
                         ffn2_w1[l].astype(bf16), ffn2_w2[l].astype(bf16), lay)
    return h.reshape(b, seq, d)
```

```python
import functools

import numpy as np
import jax
import jax.numpy as jnp
from jax import lax
from jax.experimental import pallas as pl
from jax.experimental.pallas import tpu as pltpu

f32 = jnp.float32
bf16 = jnp.bfloat16

D_MODEL = 1024
DEPTH = 2
GRID_W = 64
EPS = 1e-6
NEG_INF = -1e30
LOG2E = 1.4426950408889634
ROPE_THETA = 10000.0
D_FF = 2816
A_HEADS, A_DIM, A_VDIM = 4, 32, 64
A_W = A_HEADS * A_VDIM
B_HEADS, B_KV_HEADS, B_DIM = 6, 2, 64
B_GROUP = B_HEADS // B_KV_HEADS
B_BLOCK = 128
C_HEADS, C_DK, C_CONV, C_CHUNK = 6, 64, 5, 64
C_W = C_HEADS * C_DK
IN_SIZES = (256, 256, 256, 384, 128, 128, 1152, 384, 12, 12)
LANE = 128
B_WIDE = B_HEADS * LANE
SEG = dict(qa=(0, 256), ka=(256, 256), va=(512, 256), qb=(768, B_WIDE), kb=(1536, 128), vb=(1664, 128),
           cq=(1792, 1152), gate=(2944, 384), a=(3328, 128), b=(3456, 128))
IN_WIDE = 3584
OUT_WIDE = A_W + B_WIDE + C_W
VMEM_LIMIT = 56 * 1024 * 1024


def _cparams(n_axes):
    return pltpu.CompilerParams(dimension_semantics=("arbitrary",) * n_axes, vmem_limit_bytes=VMEM_LIMIT)


def _dot(a, b):
    return jnp.dot(a.astype(bf16), b.astype(bf16), preferred_element_type=f32)


def _dot_nt(a, b):
    return lax.dot_general(a.astype(bf16), b.astype(bf16), (((1,), (1,)), ((), ())), preferred_element_type=f32)


def _dot_tn(a, b):
    return lax.dot_general(a.astype(bf16), b.astype(bf16), (((0,), (0,)), ((), ())), preferred_element_type=f32)


def _split(x, n):
    parts = []
    for _ in range(n - 1):
        p = x.astype(bf16)
        parts.append(p)
        x = x - p.astype(f32)
    parts.append(x.astype(bf16))
    return parts


def _dot_xl(x, m, n=3):
    return sum(jnp.dot(p, m, preferred_element_type=f32) for p in _split(x, n))


def _dot_lx(m, x, n=3):
    return sum(jnp.dot(m, p, preferred_element_type=f32) for p in _split(x, n))


def _group_sum(xx, bd, terms=2):
    w = bd.shape[0]
    cols = [_dot_xl(xx[:, j:j + w], bd, terms) for j in range(0, xx.shape[1], w)]
    return cols[0] if len(cols) == 1 else jnp.concatenate(cols, axis=1)


def _silu(x):
    return x * jax.nn.sigmoid(x)


def _iota(shape, dim):
    return lax.broadcasted_iota(jnp.int32, shape, dim)


def _modulated_norm(h, shift, scale):
    hn = h * lax.rsqrt(jnp.mean(h * h, axis=-1, keepdims=True) + EPS)
    return hn * (1.0 + scale) + shift


def _mod_kernel(c_ref, w_ref, b_ref, o_ref):
    s = _silu(c_ref[...])
    w = w_ref[0]
    s_hi, s_lo = _split(s, 2)
    w_hi, w_lo = _split(w, 2)
    acc = jnp.dot(s_hi, w_hi, preferred_element_type=f32)
    acc += jnp.dot(s_hi, w_lo, preferred_element_type=f32)
    acc += jnp.dot(s_lo, w_hi, preferred_element_type=f32)
    o_ref[0] = acc + b_ref[0]


def _modulation(cvec, w_mod, b_mod):
    depth, d, n = w_mod.shape
    tn = 1024
    return pl.pallas_call(
        _mod_kernel,
        grid=(depth, n // tn),
        in_specs=[pl.BlockSpec((cvec.shape[0], d), lambda l, j: (0, 0)),
                  pl.BlockSpec((1, d, tn), lambda l, j: (l, 0, j)),
                  pl.BlockSpec((1, 1, tn), lambda l, j: (l, 0, j))],
        out_specs=pl.BlockSpec((1, cvec.shape[0], tn), lambda l, j: (l, 0, j)),
        out_shape=jax.ShapeDtypeStruct((depth, cvec.shape[0], n), f32),
        compiler_params=_cparams(2), name="modulation",
    )(cvec, w_mod, b_mod.reshape(depth, 1, n))


FFN_CHUNKS = 11


def _ffn_half_step(h, mod, idx, w1_ref, w2_ref):
    hn = _modulated_norm(h, mod[idx:idx + 1], mod[idx + 1:idx + 2]).astype(bf16)
    ck = D_FF // FFN_CHUNKS
    acc = None
    for c in range(FFN_CHUNKS):
        g = jnp.dot(hn, w1_ref[:, c * ck:(c + 1) * ck], preferred_element_type=f32)
        u = jnp.dot(hn, w1_ref[:, D_FF + c * ck:D_FF + (c + 1) * ck], preferred_element_type=f32)
        a = (_silu(g) * u).astype(bf16)
        part = jnp.dot(a, w2_ref[c * ck:(c + 1) * ck, :], preferred_element_type=f32)
        acc = part if acc is None else acc + part
    return h + (0.5 * mod[idx + 2:idx + 3]) * acc


def _ffn_kernel(*refs, idx, n_ctx_tiles):
    mod_ref, w1_ref, w2_ref, o_ref = refs[-4:]
    h = refs[-5][...]
    if n_ctx_tiles:
        h = jnp.where(pl.program_id(0) < n_ctx_tiles, refs[0][...], h)
    o_ref[...] = _ffn_half_step(h, mod_ref[0], idx, w1_ref, w2_ref)


def _ffn(h, mod, w1, w2, idx, lay):
    tm, d = lay["tm"], D_MODEL
    pair = isinstance(h, tuple)
    nct = lay["nc"] // tm
    n_tiles = (h[0].shape[0] + h[1].shape[0] if pair else h.shape[0]) // tm
    group = functools.partial(_mod_group, lay=lay, off=0)
    if pair:
        h_specs = [pl.BlockSpec((tm, d), lambda i: (jnp.minimum(i, nct - 1), 0)),
                   pl.BlockSpec((tm, d), lambda i: (jnp.maximum(i - nct, 0), 0))]
    else:
        h_specs = [pl.BlockSpec((tm, d), lambda i: (i, 0))]
    return pl.pallas_call(
        functools.partial(_ffn_kernel, idx=idx, n_ctx_tiles=nct if pair else 0),
        grid=(n_tiles,),
        in_specs=h_specs + [pl.BlockSpec((1, 9, d), lambda i: (group(i), 0, 0)),
                            pl.BlockSpec((d, 2 * D_FF), lambda i: (0, 0), pipeline_mode=pl.Buffered(1)),
                            pl.BlockSpec((D_FF, d), lambda i: (0, 0), pipeline_mode=pl.Buffered(1))],
        out_specs=pl.BlockSpec((tm, d), lambda i: (i, 0)),
        out_shape=jax.ShapeDtypeStruct((n_tiles * tm, d), f32),
        compiler_params=_cparams(1), name=f"ffn_half_step_{idx}",
    )(*(h if pair else (h,)), mod, w1, w2)


def _mod_group(i, lay, off):
    r = (i + off) * lay["tm"]
    return jnp.where(r < lay["nc"], lay["b"], (r - lay["nc"]) // lay["seq"])


def _rope(x, cos, sin, quarter):
    w = x.shape[1]
    reps = w // LANE
    if reps > 1:
        cos = jnp.concatenate([cos] * reps, axis=1)
        sin = jnp.concatenate([sin] * reps, axis=1)
    first = (_iota((1, w), 1) % (2 * quarter)) < quarter
    swapped = jnp.where(first, pltpu.roll(x, w - quarter, 1), pltpu.roll(x, quarter, 1))
    return x * cos + swapped * sin


def _inproj_kernel(h_ref, mod_ref, w_ref, nw_ref, ca_ref, sa_ref, cb_ref, sb_ref, bd32_ref, bd64_ref,
                   qa_o, ka_o, va_o, qb_o, kb_o, vb_o, cq_o, gate_o, a_o, b_o):
    mod = mod_ref[0]
    hn = _modulated_norm(h_ref[...], mod[3:4], mod[4:5]).astype(bf16)

    def proj(name):
        off, width = SEG[name]
        cols = [jnp.dot(hn, w_ref[:, c:min(c + 256, off + width)], preferred_element_type=f32)
                for c in range(off, off + width, 256)]
        return cols[0] if len(cols) == 1 else jnp.concatenate(cols, axis=1)

    def normed(x, bd, group, wrow):
        s = _group_sum(x * x, bd, terms=1)
        return x * lax.rsqrt(s * (1.0 / group) + EPS) * wrow

    nw = nw_ref[...]
    bd32, bd64 = bd32_ref[...], bd64_ref[...]
    ca, sa, cb, sb = ca_ref[...], sa_ref[...], cb_ref[...], sb_ref[...]
    qa, ka, qb, kb = proj("qa"), proj("ka"), proj("qb"), proj("kb")
    wq = jnp.concatenate([nw[2:3, :256]] * (B_WIDE // 256), axis=1)
    qa_o[...] = _rope(normed(qa, bd32, A_DIM, nw[0:1, :256]), ca, sa, A_DIM // 4).astype(bf16)
    cq_o[...] = proj("cq")
    ka_o[...] = _rope(normed(ka, bd32, A_DIM, nw[1:2, :256]), ca, sa, A_DIM // 4).astype(bf16)
    gate_o[...] = proj("gate")
    qb_o[...] = _rope(normed(qb, bd64, B_DIM, wq), cb, sb, B_DIM // 4).astype(bf16)
    va_o[...] = proj("va").astype(bf16)
    vb_o[...] = proj("vb").astype(bf16)
    kb_o[...] = _rope(normed(kb, bd64[:LANE, :LANE], B_DIM, nw[3:4, :LANE]), cb, sb, B_DIM // 4).astype(bf16)
    a_o[...] = proj("a")
    b_o[...] = proj("b")


def _inproj(h, mod, w_in, nw, tabs, bd32, bd64, lay):
    tm, d = lay["tm"], D_MODEL
    n = h.shape[0]
    n_tiles = n // tm
    nct = lay["nc"] // tm
    spt = lay["seq"] // tm
    group = functools.partial(_mod_group, lay=lay, off=0)

    def tab_idx(i):
        return jnp.where(i < nct, 0, 1 + (i - nct) % spt)

    row = lambda w: pl.BlockSpec((tm, w), lambda i: (i, 0))
    const = lambda shape: pl.BlockSpec(shape, lambda i: (0,) * len(shape))
    tab = pl.BlockSpec((tm, LANE), lambda i: (tab_idx(i), 0))
    names = ("qa", "ka", "va", "qb", "kb", "vb", "cq", "gate", "a", "b")
    dts = (bf16,) * 6 + (f32,) * 4
    return pl.pallas_call(
        _inproj_kernel,
        grid=(n_tiles,),
        in_specs=[row(d), pl.BlockSpec((1, 9, d), lambda i: (group(i), 0, 0)),
                  pl.BlockSpec((d, IN_WIDE), lambda i: (0, 0), pipeline_mode=pl.Buffered(1)),
                  const(nw.shape), tab, tab, tab, tab, const(bd32.shape), const(bd64.shape)],
        out_specs=[row(SEG[k][1]) for k in names],
        out_shape=[jax.ShapeDtypeStruct((n, SEG[k][1]), dt) for k, dt in zip(names, dts)],
        compiler_params=_cparams(1), name="mixer_in_proj",
    )(h, mod, w_in, nw, *tabs, bd32, bd64)


A_TQ = 512
A_KBLOCK = 256
A_INTERLEAVE = 8


def _attn_a_kernel(lam_ref, q_ref, *refs, n_seg):
    k_refs, v_refs = refs[:n_seg], refs[n_seg:2 * n_seg]
    o_ref, vt_ref = refs[2 * n_seg], refs[2 * n_seg + 1]
    tq = q_ref.shape[0]
    n_maps = 2 * A_HEADS

    @pl.when(pl.program_id(1) == 0)
    def _():
        off = 0
        for v_ref in v_refs:
            for r in range(0, v_ref.shape[0], A_KBLOCK):
                vt_ref[:, off + r:off + r + A_KBLOCK] = v_ref[r:r + A_KBLOCK, :].astype(f32).T.astype(bf16)
            off += v_ref.shape[0]

    lam = lam_ref[0]
    qt = q_ref[...].astype(f32).T
    feat = _iota((A_W, tq), 0)
    qms = [jnp.where(feat // A_DIM == n, qt, 0.0).astype(bf16) for n in range(n_maps)]
    normed = [None] * n_maps
    for g0 in range(0, n_maps, A_INTERLEAVE):
        grp = range(g0, g0 + A_INTERLEAVE)
        m_run = {n: jnp.full((1, tq), NEG_INF, f32) for n in grp}
        l_run = {n: jnp.zeros((1, tq), f32) for n in grp}
        acc = {n: jnp.zeros((A_VDIM, tq), f32) for n in grp}
        off = 0
        for k_ref in k_refs:
            for r in range(0, k_ref.shape[0], A_KBLOCK):
                kb = k_ref[r:r + A_KBLOCK, :]
                st = {n: jnp.dot(kb, qms[n], preferred_element_type=f32) for n in grp}
                m_new = {n: jnp.maximum(m_run[n], jnp.max(st[n], axis=0, keepdims=True)) for n in grp}
                alpha = {n: jnp.exp2(m_run[n] - m_new[n]) for n in grp}
                e = {n: jnp.exp2(st[n] - m_new[n]) for n in grp}
                l_run = {n: alpha[n] * l_run[n] + jnp.sum(e[n], axis=0, keepdims=True) for n in grp}
                pv = {n: jnp.dot(vt_ref[(n // 2) * A_VDIM:(n // 2 + 1) * A_VDIM, off + r:off + r + A_KBLOCK],
                                 e[n].astype(bf16), preferred_element_type=f32) for n in grp}
                acc = {n: alpha[n] * acc[n] + pv[n] for n in grp}
                m_run = m_new
            off += k_ref.shape[0]
        for n in grp:
            normed[n] = acc[n] * ((lam if n % 2 else 1.0) / l_run[n])
    heads = [normed[2 * h] - normed[2 * h + 1] for h in range(A_HEADS)]
    o_ref[...] = jnp.concatenate(heads, axis=0).T


def _attn_a(lam, qa, ka, va, lay, latent):
    b, ctx, seq, nc = lay["b"], lay["ctx"], lay["seq"], lay["nc"]
    smem = pl.BlockSpec(memory_space=pltpu.SMEM)
    if latent:
        tq = A_TQ
        qpb = seq // tq
        grid = (b, qpb)
        qspec = pl.BlockSpec((tq, A_W), lambda i, j: (nc // tq + i * qpb + j, 0))
        kv = [pl.BlockSpec((ctx, A_W), lambda i, j: (i, 0)), pl.BlockSpec((seq, A_W), lambda i, j: (nc // seq + i, 0))]
        ospec = pl.BlockSpec((tq, A_W), lambda i, j: (i * qpb + j, 0))
        rows = b * seq
    else:
        grid = (b, 1)
        qspec = pl.BlockSpec((ctx, A_W), lambda i, j: (i, 0))
        kv = [pl.BlockSpec((ctx, A_W), lambda i, j: (i, 0))]
        ospec = qspec
        rows = nc
    n_seg = len(kv)
    return pl.pallas_call(
        functools.partial(_attn_a_kernel, n_seg=n_seg),
        grid=grid,
        in_specs=[smem, qspec] + kv + kv,
        out_specs=ospec,
        out_shape=jax.ShapeDtypeStruct((rows, A_W), f32),
        scratch_shapes=[pltpu.VMEM((A_W, ctx + seq if latent else ctx), bf16)],
        compiler_params=_cparams(2), name="diff_attention_lat" if latent else "diff_attention_ctx",
    )(lam, qa, *([ka] * n_seg), *([va] * n_seg))


B_QSUB = 8


def _attn_b_kernel(sink_ref, q_ref, kc_ref, vc_ref, *refs, latent):
    o_ref = refs[-1]
    lane = _iota((1, LANE), 1)
    n_ctx = kc_ref.shape[0]
    kc, vc = kc_ref[...], vc_ref[...]
    rows = B_BLOCK if latent else q_ref.shape[0]
    n_sub = q_ref.shape[0] // rows
    for sub in range(n_sub):
        q = q_ref[sub * rows:(sub + 1) * rows, :]
        keys, vals, band_ok = kc, vc, None
        if latent:
            kl_ref, vl_ref = refs[0], refs[1]
            t = pl.program_id(1) * n_sub + sub
            nb = pl.num_programs(1) * n_sub
            starts = [pl.multiple_of(jnp.clip(t + off, 0, nb - 1) * B_BLOCK, B_BLOCK) for off in (-1, 0, 1)]
            keys = jnp.concatenate([kc] + [kl_ref[pl.ds(st, B_BLOCK), :] for st in starts], axis=0)
            vals = jnp.concatenate([vc] + [vl_ref[pl.ds(st, B_BLOCK), :] for st in starts], axis=0)
            r = _iota((B_GROUP * rows, 3 * B_BLOCK), 0) % rows
            c = _iota((B_GROUP * rows, 3 * B_BLOCK), 1)
            cc = c % B_BLOCK
            before = jnp.logical_and(jnp.logical_and(c < B_BLOCK, cc >= r), t >= 1)
            after = jnp.logical_and(jnp.logical_and(c >= 2 * B_BLOCK, cc <= r), t + 1 < nb)
            same = jnp.logical_and(c >= B_BLOCK, c < 2 * B_BLOCK)
            band_ok = jnp.logical_or(same, jnp.logical_or(before, after))
        for g in range(B_KV_HEADS):
            heads = range(g * B_GROUP, (g + 1) * B_GROUP)
            qg = jnp.concatenate([q[:, h * LANE:(h + 1) * LANE] for h in heads], axis=0)
            sink = jnp.concatenate([jnp.full((rows, 1), sink_ref[h], f32) for h in heads], axis=0)
            x = _dot_nt(qg, keys)
            s = [x] if band_ok is None else [x[:, :n_ctx], jnp.where(band_ok, x[:, n_ctx:], NEG_INF)]
            mx = functools.reduce(jnp.maximum, [jnp.max(x, axis=-1, keepdims=True) for x in s] + [sink])
            e = [jnp.exp2(x - mx) for x in s]
            den = functools.reduce(jnp.add, [jnp.sum(x, axis=-1, keepdims=True) for x in e]) + jnp.exp2(sink - mx)
            p = e[0] if len(e) == 1 else jnp.concatenate(e, axis=1)
            og = jnp.dot(p.astype(bf16), vals, preferred_element_type=f32) * (1.0 / den)
            og = jnp.where(lane // B_DIM == g, og, 0.0).astype(bf16)
            for n, h in enumerate(heads):
                o_ref[sub * rows:(sub + 1) * rows, h * LANE:(h + 1) * LANE] = og[n * rows:(n + 1) * rows]


def _attn_b(sink, qb, kb, vb, lay, latent):
    b, ctx, seq, nc = lay["b"], lay["ctx"], lay["seq"], lay["nc"]
    smem = pl.BlockSpec(memory_space=pltpu.SMEM)
    cspec = pl.BlockSpec((ctx, LANE), lambda i, j: (i, 0))
    if latent:
        tq = B_QSUB * B_BLOCK
        assert seq % tq == 0 and nc % tq == 0
        nb = seq // tq
        grid = (b, nb)
        qspec = pl.BlockSpec((tq, B_WIDE), lambda i, j: (nc // tq + i * nb + j, 0))
        lspec = pl.BlockSpec((seq, LANE), lambda i, j: (nc // seq + i, 0))
        in_specs = [smem, qspec, cspec, cspec, lspec, lspec]
        args = (sink, qb, kb, vb, kb, vb)
        ospec = pl.BlockSpec((tq, B_WIDE), lambda i, j: (i * nb + j, 0))
        rows = b * seq
    else:
        grid = (b, 1)
        qspec = pl.BlockSpec((ctx, B_WIDE), lambda i, j: (i, 0))
        in_specs = [smem, qspec, cspec, cspec]
        args = (sink, qb, kb, vb)
        ospec = qspec
        rows = nc
    return pl.pallas_call(
        functools.partial(_attn_b_kernel, latent=latent),
        grid=grid, in_specs=in_specs, out_specs=ospec,
        out_shape=jax.ShapeDtypeStruct((rows, B_WIDE), bf16),
        compiler_params=_cparams(2), name="window_attention_lat" if latent else "sink_attention_ctx",
    )(*args)


HALO = 8


def _gdn_prep_kernel(x_ref, prev_ref, next_ref, a_ref, b_ref, cw_ref, par_ref, bd64_ref,
                     q_o, k_o, v_o, g_o, beta_o, *, nct, cpt, spt):
    i = pl.program_id(0)
    tm = x_ref.shape[0]
    j = jnp.where(i < nct, i % cpt, (i - nct) % spt)
    per_seq = jnp.where(i < nct, cpt, spt)
    first = j == 0
    last = j == per_seq - 1
    prev = jnp.where(first, 0.0, prev_ref[...])
    nxt = jnp.where(last, 0.0, next_ref[...])
    xx = jnp.concatenate([prev, x_ref[...], nxt], axis=0)
    cw = cw_ref[...]
    rows = xx.shape[0]
    y = None
    for tap in range(C_CONV):
        shift = (C_CONV // 2 - tap) % rows
        sh = xx if shift == 0 else pltpu.roll(xx, shift, 0)
        term = sh[HALO:HALO + tm] * cw[tap:tap + 1]
        y = term if y is None else y + term
    y = _silu(y)
    bd64 = bd64_ref[...]

    def l2n(t):
        return t * lax.rsqrt(_group_sum(t * t, bd64, terms=1) + EPS)

    q_o[...] = (l2n(y[:, :C_W]) * (C_DK ** -0.5)).astype(bf16)
    k_o[...] = l2n(y[:, C_W:2 * C_W]).astype(bf16)
    v_o[...] = y[:, 2 * C_W:].astype(bf16)
    par = par_ref[...]
    z = a_ref[...] + par[1:2]
    softplus = jnp.maximum(z, 0.0) + jnp.log1p(jnp.exp(-jnp.abs(z)))
    g_o[...] = -par[0:1] * softplus
    beta_o[...] = jax.nn.sigmoid(b_ref[...])


def _gdn_prep(cq, a, bb, cw, par, bd64, lay):
    tm = 256
    n = cq.shape[0]
    n_tiles = n // tm
    nct, spt = lay["nc"] // tm, lay["seq"] // tm
    hb = tm // HALO
    last_blk = n // HALO - 1
    row = lambda w: pl.BlockSpec((tm, w), lambda i: (i, 0))
    const = lambda shape: pl.BlockSpec(shape, lambda i: (0,) * len(shape))
    return pl.pallas_call(
        functools.partial(_gdn_prep_kernel, nct=nct, cpt=lay["ctx"] // tm, spt=spt),
        grid=(n_tiles,),
        in_specs=[row(3 * C_W),
                  pl.BlockSpec((HALO, 3 * C_W), lambda i: (jnp.maximum(i * hb - 1, 0), 0)),
                  pl.BlockSpec((HALO, 3 * C_W), lambda i: (jnp.minimum((i + 1) * hb, last_blk), 0)),
                  row(LANE), row(LANE), const(cw.shape), const(par.shape), const(bd64.shape)],
        out_specs=[row(C_W), row(C_W), row(C_W), row(LANE), row(LANE)],
        out_shape=[jax.ShapeDtypeStruct((n, C_W), bf16)] * 3 + [jax.ShapeDtypeStruct((n, LANE), f32)] * 2,
        compiler_params=_cparams(1), name="gdn_inputs",
    )(cq, cq, cq, a, bb, cw, par, bd64)


GW = 4 * C_DK


def _bd(x):
    t = jnp.concatenate([x.astype(bf16)] * 4, axis=0)
    same = _iota((GW, GW), 0) // C_DK == _iota((GW, GW), 1) // C_DK
    return jnp.where(same, t, jnp.zeros_like(t))


N_GROUPS = 3
GDN_W = N_GROUPS * GW
PREP_UNROLL = 4


def _per_head_lanes(x, first):
    rows = x.shape[0]
    left = _iota((1, LANE), 1) < C_DK
    cols = []
    for pair in range(C_HEADS // 2):
        a = jnp.broadcast_to(x[:, first + 2 * pair:first + 2 * pair + 1], (rows, LANE))
        b = jnp.broadcast_to(x[:, first + 2 * pair + 1:first + 2 * pair + 2], (rows, LANE))
        cols.append(jnp.where(left, a, b))
    return jnp.concatenate(cols, axis=1)


def _per_head_rows(x, first):
    rows = x.shape[0]
    xt = jnp.concatenate([x, jnp.zeros_like(x)], axis=0).T
    xt = xt + pltpu.roll(xt, C_DK, 1)
    left = _iota((1, LANE), 1) < C_DK
    cols = []
    for pair in range(C_HEADS // 2):
        a = jnp.broadcast_to(xt[first + 2 * pair:first + 2 * pair + 1, :], (rows, LANE))
        b = jnp.broadcast_to(xt[first + 2 * pair + 1:first + 2 * pair + 2, :], (rows, LANE))
        cols.append(jnp.where(left, a, b))
    return jnp.concatenate(cols, axis=1)


def _gdn_prepare(groups, fwd_lanes):
    L = C_CHUNK
    n = range(len(groups))
    ri = _iota((L, GW), 0)
    ci = _iota((L, GW), 1) % C_DK
    eye = ci == ri
    ahead = [jnp.where(f, ri - ci, ci - ri) for f in fwd_lanes]
    qs, ks, vs, gcs, betas, gr = zip(*groups)
    kq =[_dot_nt(jnp.concatenate([ks[i], qs[i]], axis=0), _bd(ks[i])) for i in n]
    dec = [jnp.exp(jnp.where(ahead[i] >= 0, gcs[i] - gr[i], NEG_INF)) for i in n]
    a = [jnp.where(ahead[i] > 0, betas[i] * kq[i][:L] * dec[i], 0.0) for i in n]
    tm = [jnp.where(eye, 1.0, 0.0) - jnp.where(ri // 2 == ci // 2, a[i], 0.0) for i in n]
    s = 2
    while s < L:
        off = jnp.logical_and(ri // (2 * s) == ci // (2 * s), ri // s != ci // s)
        y = [_dot(jnp.where(off, a[i], 0.0), _bd(tm[i])) for i in n]
        tm = [tm[i] - _dot(tm[i], _bd(y[i])) for i in n]
        s *= 2
    eg = [jnp.exp(gcs[i]) for i in n]
    uw = [_dot(tm[i], jnp.concatenate([_bd(betas[i] * vs[i]), _bd(betas[i] * eg[i] * ks[i])], axis=1)) for i in n]
    out = []
    for i in n:
        g_last = jnp.where(fwd_lanes[i], gcs[i][L - 1:L], gcs[i][0:1])
        wq = jnp.concatenate([uw[i][:, GW:], qs[i] * eg[i]], axis=0).astype(bf16)
        qk = jnp.where(ahead[i] >= 0, kq[i][L:] * dec[i], 0.0).astype(bf16)
        kd = (ks[i] * jnp.exp(g_last - gcs[i])).astype(bf16)
        out.append((uw[i][:, :GW], wq, qk, kd, jnp.exp(g_last)))
    return out


def _gdn_scan(chunks, s_ref):
    L = C_CHUNK
    n = range(len(chunks))
    same = _iota((GW, GW), 0) // C_DK == _iota((GW, GW), 1) // C_DK
    s = [s_ref[i] for i in n]
    ws = [_dot(chunks[i][1], s[i]) for i in n]
    v_new = [chunks[i][0] - ws[i][:L] for i in n]
    o = [ws[i][L:] + _dot(chunks[i][2], _bd(v_new[i])) for i in n]
    upd = [_dot_tn(chunks[i][3], v_new[i]) for i in n]
    for i in n:
        s_ref[i] = s[i] * chunks[i][4] + jnp.where(same, upd[i], 0.0)
    return o


def _gdn_kernel(qc, kc, vc, gc, bc, ql, kl, vl, gl, bl, oc_ref, ol_ref, s_ref, *slot_refs):
    L = C_CHUNK
    slots = (slot_refs[:5], slot_refs[5:])
    s_ref[...] = jnp.zeros(s_ref.shape, f32)
    lane = _iota((1, GW), 1)
    fwd_lanes = [lane >= 0, lane < 0, lane < LANE]
    r64 = _iota((L, L), 0)
    c64 = _iota((L, L), 1)
    tri_l = (c64 <= r64).astype(bf16)
    tri_u = (c64 >= r64).astype(bf16)

    nb = PREP_UNROLL

    def prepare_block(refs, blk, slot):
        q_ref, k_ref, v_ref, g_ref, b_ref = refs
        n_chunks = q_ref.shape[0] // L
        groups = []
        for sub in range(nb):
            t = blk * nb + sub
            sl_f = pl.ds(pl.multiple_of(t * L, L), L)
            sl_b = pl.ds(pl.multiple_of((n_chunks - 1 - t) * L, L), L)
            gc_f = _dot_lx(tri_l, g_ref[sl_f, :])
            gc_b = _dot_lx(tri_u, g_ref[sl_b, :])
            data_f = ([r[sl_f, :].astype(f32) for r in (q_ref, k_ref, v_ref)]
                      + [_per_head_lanes(gc_f, 0), _per_head_lanes(b_ref[sl_f, :], 0), _per_head_rows(gc_f, 0)])
            data_b = ([r[sl_b, :].astype(f32) for r in (q_ref, k_ref, v_ref)]
                      + [_per_head_lanes(gc_b, C_HEADS), _per_head_lanes(b_ref[sl_b, :], C_HEADS),
                         _per_head_rows(gc_b, C_HEADS)])
            groups += [tuple(x[:, :GW] for x in data_f), tuple(x[:, :GW] for x in data_b),
                       tuple(jnp.concatenate([xf[:, GW:], xb[:, GW:]], axis=1) for xf, xb in zip(data_f, data_b))]
        u_s, wq_s, qk_s, kd_s, dl_s = slots[slot]
        for n, (u, wq, qk, kd, dl) in enumerate(_gdn_prepare(groups, fwd_lanes * nb)):
            it = n // N_GROUPS
            cols = slice((n % N_GROUPS) * GW, (n % N_GROUPS + 1) * GW)
            u_s[it * L:(it + 1) * L, cols] = u
            wq_s[it * 2 * L:(it + 1) * 2 * L, cols] = wq
            qk_s[it * L:(it + 1) * L, cols] = qk
            kd_s[it * L:(it + 1) * L, cols] = kd
            dl_s[it * 8:(it + 1) * 8, cols] = jnp.broadcast_to(dl, (8, GW))

    def scan_block(o_ref, blk, slot):
        n_chunks = o_ref.shape[0] // L
        u_s, wq_s, qk_s, kd_s, dl_s = slots[slot]
        for sub in range(nb):
            t = blk * nb + sub
            sl_f = pl.ds(pl.multiple_of(t * L, L), L)
            sl_b = pl.ds(pl.multiple_of((n_chunks - 1 - t) * L, L), L)
            r1 = slice(sub * L, (sub + 1) * L)
            r2 = slice(sub * 2 * L, (sub + 1) * 2 * L)
            r8 = slice(sub * 8, sub * 8 + 1)
            chunks = []
            for i in range(N_GROUPS):
                cols = slice(i * GW, (i + 1) * GW)
                chunks.append((u_s[r1, cols], wq_s[r2, cols], qk_s[r1, cols], kd_s[r1, cols], dl_s[r8, cols]))
            o0, o1, o2 = _gdn_scan(chunks, s_ref)
            o_ref[sl_f, :] += jnp.concatenate([o0, o2[:, :LANE]], axis=1)
            o_ref[sl_b, :] += jnp.concatenate([o1, o2[:, LANE:]], axis=1)

    ctx_refs, lat_refs = (qc, kc, vc, gc, bc), (ql, kl, vl, gl, bl)
    cb, lb = qc.shape[0] // (L * nb), ql.shape[0] // (L * nb)
    oc_ref[...] = jnp.zeros(oc_ref.shape, f32)
    ol_ref[...] = jnp.zeros(ol_ref.shape, f32)
    prepare_block(ctx_refs, 0, 0)
    for j in range(cb):
        if j + 1 < cb:
            prepare_block(ctx_refs, j + 1, (j + 1) % 2)
        else:
            prepare_block(lat_refs, 0, (j + 1) % 2)
        scan_block(oc_ref, j, j % 2)

    def body(pair, carry):
        for half in range(2):
            tt = 2 * pair + half
            scan_block(ol_ref, tt, (cb + half) % 2)
            prepare_block(lat_refs, tt + 1, (cb + half + 1) % 2)
        return carry

    lax.fori_loop(0, (lb - 1) // 2, body, 0)
    for tt in range(2 * ((lb - 1) // 2), lb - 1):
        prepare_block(lat_refs, tt + 1, (cb + tt + 1) % 2)
        scan_block(ol_ref, tt, (cb + tt) % 2)
    scan_block(ol_ref, lb - 1, (cb + lb - 1) % 2)


def _gdn(q, k, v, g, beta, lay):
    b, ctx, seq, nc = lay["b"], lay["ctx"], lay["seq"], lay["nc"]
    cs = lambda w: pl.BlockSpec((ctx, w), lambda i: (i, 0))
    ls = lambda w: pl.BlockSpec((seq, w), lambda i: (nc // seq + i, 0))
    widths = (C_W, C_W, C_W, LANE, LANE)
    n_it = PREP_UNROLL
    assert ctx % (C_CHUNK * PREP_UNROLL) == 0 and seq % (C_CHUNK * PREP_UNROLL) == 0
    return pl.pallas_call(
        _gdn_kernel,
        grid=(b,),
        in_specs=[cs(w) for w in widths] + [ls(w) for w in widths],
        out_specs=[pl.BlockSpec((ctx, C_W), lambda i: (i, 0)), pl.BlockSpec((seq, C_W), lambda i: (i, 0))],
        out_shape=[jax.ShapeDtypeStruct((nc, C_W), f32), jax.ShapeDtypeStruct((b * seq, C_W), f32)],
        scratch_shapes=[pltpu.VMEM((N_GROUPS, GW, GW), f32)] + 2 * [
            pltpu.VMEM((n_it * C_CHUNK, GDN_W), f32),
            pltpu.VMEM((n_it * 2 * C_CHUNK, GDN_W), bf16),
            pltpu.VMEM((n_it * C_CHUNK, GDN_W), bf16),
            pltpu.VMEM((n_it * C_CHUNK, GDN_W), bf16),
            pltpu.VMEM((n_it * 8, GDN_W), f32)],
        compiler_params=_cparams(1), name="gated_deltanet",
    )(q, k, v, g, beta, q, k, v, g, beta)


def _outproj_kernel(h_ref, mod_ref, *refs, n_ctx_tiles):
    gate_ref, w_ref, nw_ref, bd64_ref, w1_ref, w2_ref, o_ref = refs[-7:]
    mixed = [r[...] for r in refs[-10:-7]]
    if n_ctx_tiles:
        is_ctx = pl.program_id(0) < n_ctx_tiles
        mixed = [jnp.where(is_ctx, c[...], x) for c, x in zip(refs[:3], mixed)]
    oa, ob, oc = mixed
    bd64 = bd64_ref[...]
    nw = nw_ref[...]

    def normed(x, wrow):
        s = _group_sum(x * x, bd64, terms=1)
        return x * lax.rsqrt(s * (1.0 / C_DK) + EPS) * wrow

    ya = normed(oa, nw[0:1, :A_W]).astype(bf16)
    yc = (normed(oc, nw[1:2, :C_W]) * _silu(gate_ref[...])).astype(bf16)
    y = jnp.dot(ya, w_ref[:A_W, :], preferred_element_type=f32)
    y += jnp.dot(ob, w_ref[A_W:A_W + B_WIDE, :], preferred_element_type=f32)
    y += jnp.dot(yc, w_ref[A_W + B_WIDE:, :], preferred_element_type=f32)
    mod = mod_ref[0]
    o_ref[...] = _ffn_half_step(h_ref[...] + mod[5:6] * y, mod, 6, w1_ref, w2_ref)


def _outproj_ffn(h, mod, ctx_outs, lat_outs, gate, w_out, nw, bd128, w1, w2, lay):
    tm, d = lay["tm"], D_MODEL
    nct = lay["nc"] // tm
    off = 0 if ctx_outs else nct
    n_tiles = h.shape[0] // tm - off
    group = functools.partial(_mod_group, lay=lay, off=off)
    rowo = lambda w: pl.BlockSpec((tm, w), lambda i: (i + off, 0))
    rowc = lambda w: pl.BlockSpec((tm, w), lambda i: (jnp.minimum(i, nct - 1), 0))
    rowl = lambda w: pl.BlockSpec((tm, w), lambda i: (jnp.maximum(i + off - nct, 0), 0))
    const = lambda shape: pl.BlockSpec(shape, lambda i: (0,) * len(shape))
    widths = (A_W, B_WIDE, C_W)
    ctx_specs = [rowc(w) for w in widths] if ctx_outs else []
    return pl.pallas_call(
        functools.partial(_outproj_kernel, n_ctx_tiles=nct if ctx_outs else 0),
        grid=(n_tiles,),
        in_specs=[rowo(d), pl.BlockSpec((1, 9, d), lambda i: (group(i), 0, 0))] + ctx_specs
                 + [rowl(w) for w in widths]
                 + [rowo(C_W), pl.BlockSpec((OUT_WIDE, d), lambda i: (0, 0), pipeline_mode=pl.Buffered(1)),
                    const(nw.shape), const(bd128.shape),
                    pl.BlockSpec((d, 2 * D_FF), lambda i: (0, 0), pipeline_mode=pl.Buffered(1)),
                    pl.BlockSpec((D_FF, d), lambda i: (0, 0), pipeline_mode=pl.Buffered(1))],
        out_specs=pl.BlockSpec((tm, d), lambda i: (i, 0)),
        out_shape=jax.ShapeDtypeStruct((n_tiles * tm, d), f32),
        compiler_params=_cparams(1), name="mixer_out_proj_ffn",
    )(h, mod, *(ctx_outs or ()), *lat_outs, gate, w_out, nw, bd128, w1, w2)


def _block_ones(n, group):
    idx = np.arange(n) // group
    return jnp.asarray(idx[:, None] == idx[None, :], dtype=bf16)


def _rope_table(seq, d, pad_rows):
    half, quarter = d // 2, d // 4
    rows = seq // GRID_W
    row = jnp.repeat(jnp.arange(rows, dtype=f32), GRID_W)
    col = jnp.tile(jnp.arange(GRID_W, dtype=f32), rows)
    inv = ROPE_THETA ** (-jnp.arange(0, half, 2, dtype=f32) / half)
    ld = np.arange(LANE) % d
    pos = jnp.where(jnp.asarray(ld < half)[None, :], row[:, None], col[:, None])
    ang = pos * inv[np.asarray((ld % half) % quarter)][None, :]
    sign = jnp.asarray(np.where((ld % half) < quarter, -1.0, 1.0), dtype=f32)[None, :]
    cos = jnp.concatenate([jnp.ones((pad_rows, LANE), f32), jnp.cos(ang)], axis=0)
    sin = jnp.concatenate([jnp.zeros((pad_rows, LANE), f32), jnp.sin(ang) * sign], axis=0)
    return cos, sin


def _arrange_w_in(w):
    offs = np.concatenate([[0], np.cumsum(IN_SIZES)])
    part = lambda n: w[:, offs[n]:offs[n + 1]]
    zeros = lambda width: jnp.zeros((w.shape[0], width), w.dtype)
    qb = part(3)
    pieces = [part(0), part(1), part(2)]
    for h in range(B_HEADS):
        head = qb[:, h * B_DIM:(h + 1) * B_DIM]
        pieces += [head, zeros(B_DIM)] if h // B_GROUP == 0 else [zeros(B_DIM), head]
    pieces += [part(4), part(5), part(6), part(7), part(8), zeros(LANE - IN_SIZES[8]), part(9), zeros(LANE - IN_SIZES[9])]
    out = jnp.concatenate(pieces, axis=1).astype(bf16)
    assert out.shape[1] == IN_WIDE
    return out


def _arrange_w_out(w):
    zeros = jnp.zeros((B_DIM, w.shape[1]), w.dtype)
    pieces = [w[:A_W]]
    for h in range(B_HEADS):
        head = w[A_W + h * B_DIM:A_W + (h + 1) * B_DIM]
        pieces += [head, zeros] if h // B_GROUP == 0 else [zeros, head]
    pieces.append(w[A_W + B_HEADS * B_DIM:])
    out = jnp.concatenate(pieces, axis=0).astype(bf16)
    assert out.shape[0] == OUT_WIDE
    return out


def _pad_lanes(x, width):
    return jnp.pad(x, ((0, 0), (0, width - x.shape[1])))


def kernel(x, c, ctx, c_ctx, w_mod, b_mod, ffn1_w1, ffn1_w2, ffn2_w1, ffn2_w2, w_in, w_out,
           a_qnorm, a_knorm, a_lambda, a_subln, b_qnorm, b_knorm, b_sink,
           c_conv, c_A_log, c_dt_bias, c_onorm):
    b, seq, d = x.shape
    n_ctx = ctx.shape[1]
    nc = b * n_ctx
    tm = 512 if (nc % 512 == 0 and seq % 512 == 0) else 256
    lay = dict(b=b, ctx=n_ctx, seq=seq, nc=nc, tm=tm)
    assert d == D_MODEL and seq % 256 == 0 and n_ctx % 256 == 0 and nc % seq == 0 and seq % GRID_W == 0

    h = (ctx.reshape(nc, d), x.reshape(b * seq, d))
    cvec = jnp.zeros((16, d), f32).at[:b].set(c).at[b].set(c_ctx)
    mod_all = _modulation(cvec, w_mod, b_mod).reshape(DEPTH, 16, 9, d)

    bd32, bd64, bd64s = _block_ones(256, A_DIM), _block_ones(256, B_DIM), _block_ones(LANE, C_DK)
    tabs = _rope_table(seq, A_DIM, tm) + _rope_table(seq, B_DIM, tm)

    for l in range(DEPTH):
        last = l == DEPTH - 1
        lam_init = 0.8 - 0.6 * float(np.exp(-0.3 * l))
        mod = mod_all[l]
        lf = a_lambda[l].astype(f32)
        lam = (jnp.exp(jnp.sum(lf[0] * lf[1])) - jnp.exp(jnp.sum(lf[2] * lf[3])) + lam_init).reshape(1)
        nw_in = jnp.stack([jnp.tile(a_qnorm[l], 8) * (A_DIM ** -0.5 * LOG2E),
                           jnp.tile(a_knorm[l], 8),
                           jnp.tile(b_qnorm[l], 4) * (B_DIM ** -0.5 * LOG2E),
                           jnp.tile(b_knorm[l], 4)] + [jnp.zeros((256,), f32)] * 4)
        nw_out = jnp.stack([_pad_lanes((jnp.tile(a_subln[l], 4) * (1.0 - lam_init))[None], C_W)[0],
                            jnp.tile(c_onorm[l], C_HEADS)] + [jnp.zeros((C_W,), f32)] * 6)
        gpar = jnp.stack([_pad_lanes(jnp.exp(c_A_log[l].astype(f32)).reshape(1, -1), LANE)[0],
                          _pad_lanes(c_dt_bias[l].astype(f32).reshape(1, -1), LANE)[0]] + [jnp.zeros((LANE,), f32)] * 6)
        cw = jnp.concatenate([c_conv[l], jnp.zeros((8 - C_CONV, 3 * C_W), f32)], axis=0)

        h = _ffn(h, mod, ffn1_w1[l].astype(bf16), ffn1_w2[l].astype(bf16), 0, lay)
        qa, ka, va, qb, kb, vb, cq, gate, pa, pb = _inproj(h, mod, _arrange_w_in(w_in[l]), nw_in, tabs, bd32, bd64, lay)
        sink = b_sink[l].astype(f32) * LOG2E
        oa = _attn_a(lam, qa, ka, va, lay, True)
        ob = _attn_b(sink, qb, kb, vb, lay, True)
        gq, gk, gv, gg, gbeta = _gdn_prep(cq, pa, pb, cw, gpar, bd64s, lay)
        oc_ctx, oc = _gdn(gq, gk, gv, gg, gbeta, lay)
        ctx_outs = None
        if not last:
            ctx_outs = (_attn_a(lam, qa, ka, va, lay, False), _attn_b(sink, qb, kb, vb, lay, False), oc_ctx)
        h = _outproj_ffn(h, mod, ctx_outs, (oa, ob, oc), gate, _arrange_w_out(w_out[l]), nw_out, bd64s,
                         ffn2_w1[l].astype(bf16), ffn2_w2[l].astype(bf16), lay)
    return h.reshape(b, seq, d)
```

```python
import functools

import numpy as np
import jax
import jax.numpy as jnp
from jax import lax
from jax.experimental import pallas as pl
from jax.experimental.pallas import tpu as pltpu

f32 = jnp.float32
bf16 = jnp.bfloat16

D_MODEL = 1024
DEPTH = 2
GRID_W = 64
EPS = 1e-6
NEG_INF = -1e30
LOG2E = 1.4426950408889634
ROPE_THETA = 10000.0
D_FF = 2816
A_HEADS, A_DIM, A_VDIM = 4, 32, 64
A_W = A_HEADS * A_VDIM
B_HEADS, B_KV_HEADS, B_DIM = 6, 2, 64
B_GROUP = B_HEADS // B_KV_HEADS
B_BLOCK = 128
C_HEADS, C_DK, C_CONV, C_CHUNK = 6, 64, 5, 64
C_W = C_HEADS * C_DK
IN_SIZES = (256, 256, 256, 384, 128, 128, 1152, 384, 12, 12)
LANE = 128
B_WIDE = B_HEADS * LANE
SEG = dict(qa=(0, 256), ka=(256, 256), va=(512, 256), qb=(768, B_WIDE), kb=(1536, 128), vb=(1664, 128),
           cq=(1792, 1152), gate=(2944, 384), a=(3328, 128), b=(3456, 128))
IN_WIDE = 3584
OUT_WIDE = A_W + B_WIDE + C_W
VMEM_LIMIT = 56 * 1024 * 1024


def _cparams(n_axes):
    return pltpu.CompilerParams(dimension_semantics=("arbitrary",) * n_axes, vmem_limit_bytes=VMEM_LIMIT)


def _dot(a, b):
    return jnp.dot(a.astype(bf16), b.astype(bf16), preferred_element_type=f32)


def _dot_nt(a, b):
    return lax.dot_general(a.astype(bf16), b.astype(bf16), (((1,), (1,)), ((), ())), preferred_element_type=f32)


def _dot_tn(a, b):
    return lax.dot_general(a.astype(bf16), b.astype(bf16), (((0,), (0,)), ((), ())), preferred_element_type=f32)


def _split(x, n):
    parts = []
    for _ in range(n - 1):
        p = x.astype(bf16)
        parts.append(p)
        x = x - p.astype(f32)
    parts.append(x.astype(bf16))
    return parts


def _dot_xl(x, m, n=3):
    return sum(jnp.dot(p, m, preferred_element_type=f32) for p in _split(x, n))


def _dot_lx(m, x, n=3):
    return sum(jnp.dot(m, p, preferred_element_type=f32) for p in _split(x, n))


def _group_sum(xx, bd, terms=2):
    w = bd.shape[0]
    cols = [_dot_xl(xx[:, j:j + w], bd, terms) for j in range(0, xx.shape[1], w)]
    return cols[0] if len(cols) == 1 else jnp.concatenate(cols, axis=1)


def _silu(x):
    return x * jax.nn.sigmoid(x)


def _iota(shape, dim):
    return lax.broadcasted_iota(jnp.int32, shape, dim)


def _modulated_norm(h, shift, scale):
    hn = h * lax.rsqrt(jnp.mean(h * h, axis=-1, keepdims=True) + EPS)
    return hn * (1.0 + scale) + shift


def _mod_kernel(c_ref, w_ref, b_ref, o_ref):
    s = _silu(c_ref[...])
    w = w_ref[0]
    s_hi, s_lo = _split(s, 2)
    w_hi, w_lo = _split(w, 2)
    acc = jnp.dot(s_hi, w_hi, preferred_element_type=f32)
    acc += jnp.dot(s_hi, w_lo, preferred_element_type=f32)
    acc += jnp.dot(s_lo, w_hi, preferred_element_type=f32)
    o_ref[0] = acc + b_ref[0]


def _modulation(cvec, w_mod, b_mod):
    depth, d, n = w_mod.shape
    tn = 1024
    return pl.pallas_call(
        _mod_kernel,
        grid=(depth, n // tn),
        in_specs=[pl.BlockSpec((cvec.shape[0], d), lambda l, j: (0, 0)),
                  pl.BlockSpec((1, d, tn), lambda l, j: (l, 0, j)),
                  pl.BlockSpec((1, 1, tn), lambda l, j: (l, 0, j))],
        out_specs=pl.BlockSpec((1, cvec.shape[0], tn), lambda l, j: (l, 0, j)),
        out_shape=jax.ShapeDtypeStruct((depth, cvec.shape[0], n), f32),
        compiler_params=_cparams(2), name="modulation",
    )(cvec, w_mod, b_mod.reshape(depth, 1, n))


FFN_CHUNKS = 11


def _ffn_half_step(h, mod, idx, w1_ref, w2_ref):
    hn = _modulated_norm(h, mod[idx:idx + 1], mod[idx + 1:idx + 2]).astype(bf16)
    ck = D_FF // FFN_CHUNKS
    acc = None
    for c in range(FFN_CHUNKS):
        g = jnp.dot(hn, w1_ref[:, c * ck:(c + 1) * ck], preferred_element_type=f32)
        u = jnp.dot(hn, w1_ref[:, D_FF + c * ck:D_FF + (c + 1) * ck], preferred_element_type=f32)
        a = (_silu(g) * u).astype(bf16)
        part = jnp.dot(a, w2_ref[c * ck:(c + 1) * ck, :], preferred_element_type=f32)
        acc = part if acc is None else acc + part
    return h + (0.5 * mod[idx + 2:idx + 3]) * acc


def _ffn_kernel(*refs, idx, n_ctx_tiles):
    mod_ref, w1_ref, w2_ref, o_ref = refs[-4:]
    h = refs[-5][...]
    if n_ctx_tiles:
        h = jnp.where(pl.program_id(0) < n_ctx_tiles, refs[0][...], h)
    o_ref[...] = _ffn_half_step(h, mod_ref[0], idx, w1_ref, w2_ref)


def _ffn(h, mod, w1, w2, idx, lay):
    tm, d = lay["tm"], D_MODEL
    pair = isinstance(h, tuple)
    nct = lay["nc"] // tm
    n_tiles = (h[0].shape[0] + h[1].shape[0] if pair else h.shape[0]) // tm
    group = functools.partial(_mod_group, lay=lay, off=0)
    if pair:
        h_specs = [pl.BlockSpec((tm, d), lambda i: (jnp.minimum(i, nct - 1), 0)),
                   pl.BlockSpec((tm, d), lambda i: (jnp.maximum(i - nct, 0), 0))]
    else:
        h_specs = [pl.BlockSpec((tm, d), lambda i: (i, 0))]
    return pl.pallas_call(
        functools.partial(_ffn_kernel, idx=idx, n_ctx_tiles=nct if pair else 0),
        grid=(n_tiles,),
        in_specs=h_specs + [pl.BlockSpec((1, 9, d), lambda i: (group(i), 0, 0)),
                            pl.BlockSpec((d, 2 * D_FF), lambda i: (0, 0), pipeline_mode=pl.Buffered(1)),
                            pl.BlockSpec((D_FF, d), lambda i: (0, 0), pipeline_mode=pl.Buffered(1))],
        out_specs=pl.BlockSpec((tm, d), lambda i: (i, 0)),
        out_shape=jax.ShapeDtypeStruct((n_tiles * tm, d), f32),
        compiler_params=_cparams(1), name=f"ffn_half_step_{idx}",
    )(*(h if pair else (h,)), mod, w1, w2)


def _mod_group(i, lay, off):
    r = (i + off) * lay["tm"]
    return jnp.where(r < lay["nc"], lay["b"], (r - lay["nc"]) // lay["seq"])


def _rope(x, cos, sin, quarter):
    w = x.shape[1]
    reps = w // LANE
    if reps > 1:
        cos = jnp.concatenate([cos] * reps, axis=1)
        sin = jnp.concatenate([sin] * reps, axis=1)
    first = (_iota((1, w), 1) % (2 * quarter)) < quarter
    swapped = jnp.where(first, pltpu.roll(x, w - quarter, 1), pltpu.roll(x, quarter, 1))
    return x * cos + swapped * sin


def _inproj_kernel(h_ref, mod_ref, w_ref, nw_ref, ca_ref, sa_ref, cb_ref, sb_ref, bd32_ref, bd64_ref,
                   *refs, fuse_ffn):
    mod = mod_ref[0]
    h = h_ref[...]
    if fuse_ffn:
        h = _ffn_half_step(h, mod, 0, refs[0], refs[1])
        refs[-1][...] = h
        refs = refs[2:-1]
    qa_o, ka_o, va_o, qb_o, kb_o, vb_o, cq_o, gate_o, a_o, b_o = refs
    hn = _modulated_norm(h, mod[3:4], mod[4:5]).astype(bf16)

    def proj(name):
        off, width = SEG[name]
        cols = [jnp.dot(hn, w_ref[:, c:min(c + 256, off + width)], preferred_element_type=f32)
                for c in range(off, off + width, 256)]
        return cols[0] if len(cols) == 1 else jnp.concatenate(cols, axis=1)

    def normed(x, bd, group, wrow):
        s = _group_sum(x * x, bd, terms=1)
        return x * lax.rsqrt(s * (1.0 / group) + EPS) * wrow

    nw = nw_ref[...]
    bd32, bd64 = bd32_ref[...], bd64_ref[...]
    ca, sa, cb, sb = ca_ref[...], sa_ref[...], cb_ref[...], sb_ref[...]
    qa, ka, qb, kb = proj("qa"), proj("ka"), proj("qb"), proj("kb")
    wq = jnp.concatenate([nw[2:3, :256]] * (B_WIDE // 256), axis=1)
    qa_o[...] = _rope(normed(qa, bd32, A_DIM, nw[0:1, :256]), ca, sa, A_DIM // 4).astype(bf16)
    cq_o[...] = proj("cq")
    ka_o[...] = _rope(normed(ka, bd32, A_DIM, nw[1:2, :256]), ca, sa, A_DIM // 4).astype(bf16)
    gate_o[...] = proj("gate")
    qb_o[...] = _rope(normed(qb, bd64, B_DIM, wq), cb, sb, B_DIM // 4).astype(bf16)
    va_o[...] = proj("va").astype(bf16)
    vb_o[...] = proj("vb").astype(bf16)
    kb_o[...] = _rope(normed(kb, bd64[:LANE, :LANE], B_DIM, nw[3:4, :LANE]), cb, sb, B_DIM // 4).astype(bf16)
    a_o[...] = proj("a")
    b_o[...] = proj("b")


def _inproj(h, mod, w_in, nw, tabs, bd32, bd64, lay, ffn=None):
    tm, d = lay["tm"], D_MODEL
    n = h.shape[0]
    n_tiles = n // tm
    nct = lay["nc"] // tm
    spt = lay["seq"] // tm
    group = functools.partial(_mod_group, lay=lay, off=0)

    def tab_idx(i):
        return jnp.where(i < nct, 0, 1 + (i - nct) % spt)

    row = lambda w: pl.BlockSpec((tm, w), lambda i: (i, 0))
    const = lambda shape: pl.BlockSpec(shape, lambda i: (0,) * len(shape))
    tab = pl.BlockSpec((tm, LANE), lambda i: (tab_idx(i), 0))
    names = ("qa", "ka", "va", "qb", "kb", "vb", "cq", "gate", "a", "b")
    dts = (bf16,) * 6 + (f32,) * 4
    ffn_specs = [pl.BlockSpec((d, 2 * D_FF), lambda i: (0, 0), pipeline_mode=pl.Buffered(1)),
                 pl.BlockSpec((D_FF, d), lambda i: (0, 0), pipeline_mode=pl.Buffered(1))] if ffn else []
    return pl.pallas_call(
        functools.partial(_inproj_kernel, fuse_ffn=bool(ffn)),
        grid=(n_tiles,),
        in_specs=[row(d), pl.BlockSpec((1, 9, d), lambda i: (group(i), 0, 0)),
                  pl.BlockSpec((d, IN_WIDE), lambda i: (0, 0), pipeline_mode=pl.Buffered(1)),
                  const(nw.shape), tab, tab, tab, tab, const(bd32.shape), const(bd64.shape)] + ffn_specs,
        out_specs=[row(SEG[k][1]) for k in names] + ([row(d)] if ffn else []),
        out_shape=[jax.ShapeDtypeStruct((n, SEG[k][1]), dt) for k, dt in zip(names, dts)]
                  + ([jax.ShapeDtypeStruct((n, d), f32)] if ffn else []),
        compiler_params=_cparams(1), name="ffn_in_proj" if ffn else "mixer_in_proj",
    )(h, mod, w_in, nw, *tabs, bd32, bd64, *(ffn or ()))


A_TQ = 512
A_KBLOCK = 256
A_INTERLEAVE = 8


def _attn_a_kernel(lam_ref, q_ref, *refs, n_seg):
    k_refs, v_refs = refs[:n_seg], refs[n_seg:2 * n_seg]
    o_ref, vt_ref = refs[2 * n_seg], refs[2 * n_seg + 1]
    tq = q_ref.shape[0]
    n_maps = 2 * A_HEADS

    @pl.when(pl.program_id(1) == 0)
    def _():
        off = 0
        for v_ref in v_refs:
            for r in range(0, v_ref.shape[0], A_KBLOCK):
                vt_ref[:, off + r:off + r + A_KBLOCK] = v_ref[r:r + A_KBLOCK, :].astype(f32).T.astype(bf16)
            off += v_ref.shape[0]

    lam = lam_ref[0]
    qt = q_ref[...].astype(f32).T
    feat = _iota((A_W, tq), 0)
    qms = [jnp.where(feat // A_DIM == n, qt, 0.0).astype(bf16) for n in range(n_maps)]
    normed = [None] * n_maps
    for g0 in range(0, n_maps, A_INTERLEAVE):
        grp = range(g0, g0 + A_INTERLEAVE)
        m_run = {n: jnp.full((1, tq), NEG_INF, f32) for n in grp}
        l_run = {n: jnp.zeros((1, tq), f32) for n in grp}
        acc = {n: jnp.zeros((A_VDIM, tq), f32) for n in grp}
        off = 0
        for k_ref in k_refs:
            for r in range(0, k_ref.shape[0], A_KBLOCK):
                kb = k_ref[r:r + A_KBLOCK, :]
                st = {n: jnp.dot(kb, qms[n], preferred_element_type=f32) for n in grp}
                m_new = {n: jnp.maximum(m_run[n], jnp.max(st[n], axis=0, keepdims=True)) for n in grp}
                alpha = {n: jnp.exp2(m_run[n] - m_new[n]) for n in grp}
                e = {n: jnp.exp2(st[n] - m_new[n]) for n in grp}
                l_run = {n: alpha[n] * l_run[n] + jnp.sum(e[n], axis=0, keepdims=True) for n in grp}
                pv = {n: jnp.dot(vt_ref[(n // 2) * A_VDIM:(n // 2 + 1) * A_VDIM, off + r:off + r + A_KBLOCK],
                                 e[n].astype(bf16), preferred_element_type=f32) for n in grp}
                acc = {n: alpha[n] * acc[n] + pv[n] for n in grp}
                m_run = m_new
            off += k_ref.shape[0]
        for n in grp:
            normed[n] = acc[n] * ((lam if n % 2 else 1.0) / l_run[n])
    heads = [normed[2 * h] - normed[2 * h + 1] for h in range(A_HEADS)]
    o_ref[...] = jnp.concatenate(heads, axis=0).T


def _attn_a(lam, qa, ka, va, lay, latent):
    b, ctx, seq, nc = lay["b"], lay["ctx"], lay["seq"], lay["nc"]
    smem = pl.BlockSpec(memory_space=pltpu.SMEM)
    if latent:
        tq = A_TQ
        qpb = seq // tq
        grid = (b, qpb)
        qspec = pl.BlockSpec((tq, A_W), lambda i, j: (nc // tq + i * qpb + j, 0))
        kv = [pl.BlockSpec((ctx, A_W), lambda i, j: (i, 0)), pl.BlockSpec((seq, A_W), lambda i, j: (nc // seq + i, 0))]
        ospec = pl.BlockSpec((tq, A_W), lambda i, j: (i * qpb + j, 0))
        rows = b * seq
    else:
        grid = (b, 1)
        qspec = pl.BlockSpec((ctx, A_W), lambda i, j: (i, 0))
        kv = [pl.BlockSpec((ctx, A_W), lambda i, j: (i, 0))]
        ospec = qspec
        rows = nc
    n_seg = len(kv)
    return pl.pallas_call(
        functools.partial(_attn_a_kernel, n_seg=n_seg),
        grid=grid,
        in_specs=[smem, qspec] + kv + kv,
        out_specs=ospec,
        out_shape=jax.ShapeDtypeStruct((rows, A_W), f32),
        scratch_shapes=[pltpu.VMEM((A_W, ctx + seq if latent else ctx), bf16)],
        compiler_params=_cparams(2), name="diff_attention_lat" if latent else "diff_attention_ctx",
    )(lam, qa, *([ka] * n_seg), *([va] * n_seg))


B_QSUB = 8


def _attn_b_kernel(sink_ref, q_ref, kc_ref, vc_ref, *refs, latent):
    o_ref = refs[-1]
    lane = _iota((1, LANE), 1)
    n_ctx = kc_ref.shape[0]
    kc, vc = kc_ref[...], vc_ref[...]
    rows = B_BLOCK if latent else q_ref.shape[0]
    n_sub = q_ref.shape[0] // rows
    for sub in range(n_sub):
        q = q_ref[sub * rows:(sub + 1) * rows, :]
        keys, vals, band_ok = kc, vc, None
        if latent:
            kl_ref, vl_ref = refs[0], refs[1]
            t = pl.program_id(1) * n_sub + sub
            nb = pl.num_programs(1) * n_sub
            starts = [pl.multiple_of(jnp.clip(t + off, 0, nb - 1) * B_BLOCK, B_BLOCK) for off in (-1, 0, 1)]
            keys = jnp.concatenate([kc] + [kl_ref[pl.ds(st, B_BLOCK), :] for st in starts], axis=0)
            vals = jnp.concatenate([vc] + [vl_ref[pl.ds(st, B_BLOCK), :] for st in starts], axis=0)
            r = _iota((B_GROUP * rows, 3 * B_BLOCK), 0) % rows
            c = _iota((B_GROUP * rows, 3 * B_BLOCK), 1)
            cc = c % B_BLOCK
            before = jnp.logical_and(jnp.logical_and(c < B_BLOCK, cc >= r), t >= 1)
            after = jnp.logical_and(jnp.logical_and(c >= 2 * B_BLOCK, cc <= r), t + 1 < nb)
            same = jnp.logical_and(c >= B_BLOCK, c < 2 * B_BLOCK)
            band_ok = jnp.logical_or(same, jnp.logical_or(before, after))
        for g in range(B_KV_HEADS):
            heads = range(g * B_GROUP, (g + 1) * B_GROUP)
            qg = jnp.concatenate([q[:, h * LANE:(h + 1) * LANE] for h in heads], axis=0)
            sink = jnp.concatenate([jnp.full((rows, 1), sink_ref[h], f32) for h in heads], axis=0)
            x = _dot_nt(qg, keys)
            s = [x] if band_ok is None else [x[:, :n_ctx], jnp.where(band_ok, x[:, n_ctx:], NEG_INF)]
            mx = functools.reduce(jnp.maximum, [jnp.max(x, axis=-1, keepdims=True) for x in s] + [sink])
            e = [jnp.exp2(x - mx) for x in s]
            den = functools.reduce(jnp.add, [jnp.sum(x, axis=-1, keepdims=True) for x in e]) + jnp.exp2(sink - mx)
            p = e[0] if len(e) == 1 else jnp.concatenate(e, axis=1)
            og = jnp.dot(p.astype(bf16), vals, preferred_element_type=f32) * (1.0 / den)
            og = jnp.where(lane // B_DIM == g, og, 0.0).astype(bf16)
            for n, h in enumerate(heads):
                o_ref[sub * rows:(sub + 1) * rows, h * LANE:(h + 1) * LANE] = og[n * rows:(n + 1) * rows]


def _attn_b(sink, qb, kb, vb, lay, latent):
    b, ctx, seq, nc = lay["b"], lay["ctx"], lay["seq"], lay["nc"]
    smem = pl.BlockSpec(memory_space=pltpu.SMEM)
    cspec = pl.BlockSpec((ctx, LANE), lambda i, j: (i, 0))
    if latent:
        tq = B_QSUB * B_BLOCK
        assert seq % tq == 0 and nc % tq == 0
        nb = seq // tq
        grid = (b, nb)
        qspec = pl.BlockSpec((tq, B_WIDE), lambda i, j: (nc // tq + i * nb + j, 0))
        lspec = pl.BlockSpec((seq, LANE), lambda i, j: (nc // seq + i, 0))
        in_specs = [smem, qspec, cspec, cspec, lspec, lspec]
        args = (sink, qb, kb, vb, kb, vb)
        ospec = pl.BlockSpec((tq, B_WIDE), lambda i, j: (i * nb + j, 0))
        rows = b * seq
    else:
        grid = (b, 1)
        qspec = pl.BlockSpec((ctx, B_WIDE), lambda i, j: (i, 0))
        in_specs = [smem, qspec, cspec, cspec]
        args = (sink, qb, kb, vb)
        ospec = qspec
        rows = nc
    return pl.pallas_call(
        functools.partial(_attn_b_kernel, latent=latent),
        grid=grid, in_specs=in_specs, out_specs=ospec,
        out_shape=jax.ShapeDtypeStruct((rows, B_WIDE), bf16),
        compiler_params=_cparams(2), name="window_attention_lat" if latent else "sink_attention_ctx",
    )(*args)


HALO = 8


def _gdn_prep_kernel(x_ref, prev_ref, next_ref, a_ref, b_ref, cw_ref, par_ref, bd64_ref,
                     q_o, k_o, v_o, g_o, beta_o, *, nct, cpt, spt):
    i = pl.program_id(0)
    tm = x_ref.shape[0]
    j = jnp.where(i < nct, i % cpt, (i - nct) % spt)
    per_seq = jnp.where(i < nct, cpt, spt)
    first = j == 0
    last = j == per_seq - 1
    prev = jnp.where(first, 0.0, prev_ref[...])
    nxt = jnp.where(last, 0.0, next_ref[...])
    xx = jnp.concatenate([prev, x_ref[...], nxt], axis=0)
    cw = cw_ref[...]
    rows = xx.shape[0]
    y = None
    for tap in range(C_CONV):
        shift = (C_CONV // 2 - tap) % rows
        sh = xx if shift == 0 else pltpu.roll(xx, shift, 0)
        term = sh[HALO:HALO + tm] * cw[tap:tap + 1]
        y = term if y is None else y + term
    y = _silu(y)
    bd64 = bd64_ref[...]

    def l2n(t):
        return t * lax.rsqrt(_group_sum(t * t, bd64, terms=1) + EPS)

    q_o[...] = (l2n(y[:, :C_W]) * (C_DK ** -0.5)).astype(bf16)
    k_o[...] = l2n(y[:, C_W:2 * C_W]).astype(bf16)
    v_o[...] = y[:, 2 * C_W:].astype(bf16)
    par = par_ref[...]
    z = a_ref[...] + par[1:2]
    softplus = jnp.maximum(z, 0.0) + jnp.log1p(jnp.exp(-jnp.abs(z)))
    g_o[...] = -par[0:1] * softplus
    beta_o[...] = jax.nn.sigmoid(b_ref[...])


def _gdn_prep(cq, a, bb, cw, par, bd64, lay):
    tm = 256
    n = cq.shape[0]
    n_tiles = n // tm
    nct, spt = lay["nc"] // tm, lay["seq"] // tm
    hb = tm // HALO
    last_blk = n // HALO - 1
    row = lambda w: pl.BlockSpec((tm, w), lambda i: (i, 0))
    const = lambda shape: pl.BlockSpec(shape, lambda i: (0,) * len(shape))
    return pl.pallas_call(
        functools.partial(_gdn_prep_kernel, nct=nct, cpt=lay["ctx"] // tm, spt=spt),
        grid=(n_tiles,),
        in_specs=[row(3 * C_W),
                  pl.BlockSpec((HALO, 3 * C_W), lambda i: (jnp.maximum(i * hb - 1, 0), 0)),
                  pl.BlockSpec((HALO, 3 * C_W), lambda i: (jnp.minimum((i + 1) * hb, last_blk), 0)),
                  row(LANE), row(LANE), const(cw.shape), const(par.shape), const(bd64.shape)],
        out_specs=[row(C_W), row(C_W), row(C_W), row(LANE), row(LANE)],
        out_shape=[jax.ShapeDtypeStruct((n, C_W), bf16)] * 3 + [jax.ShapeDtypeStruct((n, LANE), f32)] * 2,
        compiler_params=_cparams(1), name="gdn_inputs",
    )(cq, cq, cq, a, bb, cw, par, bd64)


GW = 4 * C_DK


def _bd(x):
    t = jnp.concatenate([x.astype(bf16)] * 4, axis=0)
    same = _iota((GW, GW), 0) // C_DK == _iota((GW, GW), 1) // C_DK
    return jnp.where(same, t, jnp.zeros_like(t))


N_GROUPS = 3
GDN_W = N_GROUPS * GW
PREP_UNROLL = 4


def _per_head_lanes(x, first):
    rows = x.shape[0]
    left = _iota((1, LANE), 1) < C_DK
    cols = []
    for pair in range(C_HEADS // 2):
        a = jnp.broadcast_to(x[:, first + 2 * pair:first + 2 * pair + 1], (rows, LANE))
        b = jnp.broadcast_to(x[:, first + 2 * pair + 1:first + 2 * pair + 2], (rows, LANE))
        cols.append(jnp.where(left, a, b))
    return jnp.concatenate(cols, axis=1)


def _per_head_rows(x, first):
    rows = x.shape[0]
    xt = jnp.concatenate([x, jnp.zeros_like(x)], axis=0).T
    xt = xt + pltpu.roll(xt, C_DK, 1)
    left = _iota((1, LANE), 1) < C_DK
    cols = []
    for pair in range(C_HEADS // 2):
        a = jnp.broadcast_to(xt[first + 2 * pair:first + 2 * pair + 1, :], (rows, LANE))
        b = jnp.broadcast_to(xt[first + 2 * pair + 1:first + 2 * pair + 2, :], (rows, LANE))
        cols.append(jnp.where(left, a, b))
    return jnp.concatenate(cols, axis=1)


def _gdn_prepare(groups, fwd_lanes):
    L = C_CHUNK
    n = range(len(groups))
    ri = _iota((L, GW), 0)
    ci = _iota((L, GW), 1) % C_DK
    eye = ci == ri
    ahead = [jnp.where(f, ri - ci, ci - ri) for f in fwd_lanes]
    qs, ks, vs, gcs, betas, gr = zip(*groups)
    kq =[_dot_nt(jnp.concatenate([ks[i], qs[i]], axis=0), _bd(ks[i])) for i in n]
    dec = [jnp.exp(jnp.where(ahead[i] >= 0, gcs[i] - gr[i], NEG_INF)) for i in n]
    a = [jnp.where(ahead[i] > 0, betas[i] * kq[i][:L] * dec[i], 0.0) for i in n]
    tm = [jnp.where(eye, 1.0, 0.0) - jnp.where(ri // 2 == ci // 2, a[i], 0.0) for i in n]
    s = 2
    while s < L:
        off = jnp.logical_and(ri // (2 * s) == ci // (2 * s), ri // s != ci // s)
        y = [_dot(jnp.where(off, a[i], 0.0), _bd(tm[i])) for i in n]
        tm = [tm[i] - _dot(tm[i], _bd(y[i])) for i in n]
        s *= 2
    eg = [jnp.exp(gcs[i]) for i in n]
    uw = [_dot(tm[i], jnp.concatenate([_bd(betas[i] * vs[i]), _bd(betas[i] * eg[i] * ks[i])], axis=1)) for i in n]
    out = []
    for i in n:
        g_last = jnp.where(fwd_lanes[i], gcs[i][L - 1:L], gcs[i][0:1])
        wq = jnp.concatenate([uw[i][:, GW:], qs[i] * eg[i]], axis=0).astype(bf16)
        qk = jnp.where(ahead[i] >= 0, kq[i][L:] * dec[i], 0.0).astype(bf16)
        kd = (ks[i] * jnp.exp(g_last - gcs[i])).astype(bf16)
        out.append((uw[i][:, :GW], wq, qk, kd, jnp.exp(g_last)))
    return out


def _gdn_scan(chunks, s_ref):
    L = C_CHUNK
    n = range(len(chunks))
    same = _iota((GW, GW), 0) // C_DK == _iota((GW, GW), 1) // C_DK
    s = [s_ref[i] for i in n]
    ws = [_dot(chunks[i][1], s[i]) for i in n]
    v_new = [chunks[i][0] - ws[i][:L] for i in n]
    o = [ws[i][L:] + _dot(chunks[i][2], _bd(v_new[i])) for i in n]
    upd = [_dot_tn(chunks[i][3], v_new[i]) for i in n]
    for i in n:
        s_ref[i] = s[i] * chunks[i][4] + jnp.where(same, upd[i], 0.0)
    return o


def _gdn_kernel(qc, kc, vc, gc, bc, ql, kl, vl, gl, bl, oc_ref, ol_ref, s_ref, *slot_refs):
    L = C_CHUNK
    slots = (slot_refs[:5], slot_refs[5:])
    s_ref[...] = jnp.zeros(s_ref.shape, f32)
    lane = _iota((1, GW), 1)
    fwd_lanes = [lane >= 0, lane < 0, lane < LANE]
    r64 = _iota((L, L), 0)
    c64 = _iota((L, L), 1)
    tri_l = (c64 <= r64).astype(bf16)
    tri_u = (c64 >= r64).astype(bf16)

    nb = PREP_UNROLL

    def prepare_block(refs, blk, slot):
        q_ref, k_ref, v_ref, g_ref, b_ref = refs
        n_chunks = q_ref.shape[0] // L
        groups = []
        for sub in range(nb):
            t = blk * nb + sub
            sl_f = pl.ds(pl.multiple_of(t * L, L), L)
            sl_b = pl.ds(pl.multiple_of((n_chunks - 1 - t) * L, L), L)
            gc_f = _dot_lx(tri_l, g_ref[sl_f, :])
            gc_b = _dot_lx(tri_u, g_ref[sl_b, :])
            data_f = ([r[sl_f, :].astype(f32) for r in (q_ref, k_ref, v_ref)]
                      + [_per_head_lanes(gc_f, 0), _per_head_lanes(b_ref[sl_f, :], 0), _per_head_rows(gc_f, 0)])
            data_b = ([r[sl_b, :].astype(f32) for r in (q_ref, k_ref, v_ref)]
                      + [_per_head_lanes(gc_b, C_HEADS), _per_head_lanes(b_ref[sl_b, :], C_HEADS),
                         _per_head_rows(gc_b, C_HEADS)])
            groups += [tuple(x[:, :GW] for x in data_f), tuple(x[:, :GW] for x in data_b),
                       tuple(jnp.concatenate([xf[:, GW:], xb[:, GW:]], axis=1) for xf, xb in zip(data_f, data_b))]
        u_s, wq_s, qk_s, kd_s, dl_s = slots[slot]
        for n, (u, wq, qk, kd, dl) in enumerate(_gdn_prepare(groups, fwd_lanes * nb)):
            it = n // N_GROUPS
            cols = slice((n % N_GROUPS) * GW, (n % N_GROUPS + 1) * GW)
            u_s[it * L:(it + 1) * L, cols] = u
            wq_s[it * 2 * L:(it + 1) * 2 * L, cols] = wq
            qk_s[it * L:(it + 1) * L, cols] = qk
            kd_s[it * L:(it + 1) * L, cols] = kd
            dl_s[it * 8:(it + 1) * 8, cols] = jnp.broadcast_to(dl, (8, GW))

    def scan_block(o_ref, blk, slot):
        n_chunks = o_ref.shape[0] // L
        u_s, wq_s, qk_s, kd_s, dl_s = slots[slot]
        for sub in range(nb):
            t = blk * nb + sub
            sl_f = pl.ds(pl.multiple_of(t * L, L), L)
            sl_b = pl.ds(pl.multiple_of((n_chunks - 1 - t) * L, L), L)
            r1 = slice(sub * L, (sub + 1) * L)
            r2 = slice(sub * 2 * L, (sub + 1) * 2 * L)
            r8 = slice(sub * 8, sub * 8 + 1)
            chunks = []
            for i in range(N_GROUPS):
                cols = slice(i * GW, (i + 1) * GW)
                chunks.append((u_s[r1, cols], wq_s[r2, cols], qk_s[r1, cols], kd_s[r1, cols], dl_s[r8, cols]))
            o0, o1, o2 = _gdn_scan(chunks, s_ref)
            o_ref[sl_f, :] += jnp.concatenate([o0, o2[:, :LANE]], axis=1)
            o_ref[sl_b, :] += jnp.concatenate([o1, o2[:, LANE:]], axis=1)

    ctx_refs, lat_refs = (qc, kc, vc, gc, bc), (ql, kl, vl, gl, bl)
    cb, lb = qc.shape[0] // (L * nb), ql.shape[0] // (L * nb)
    oc_ref[...] = jnp.zeros(oc_ref.shape, f32)
    ol_ref[...] = jnp.zeros(ol_ref.shape, f32)
    prepare_block(ctx_refs, 0, 0)
    for j in range(cb):
        if j + 1 < cb:
            prepare_block(ctx_refs, j + 1, (j + 1) % 2)
        else:
            prepare_block(lat_refs, 0, (j + 1) % 2)
        scan_block(oc_ref, j, j % 2)

    def body(pair, carry):
        for half in range(2):
            tt = 2 * pair + half
            scan_block(ol_ref, tt, (cb + half) % 2)
            prepare_block(lat_refs, tt + 1, (cb + half + 1) % 2)
        return carry

    lax.fori_loop(0, (lb - 1) // 2, body, 0)
    for tt in range(2 * ((lb - 1) // 2), lb - 1):
        prepare_block(lat_refs, tt + 1, (cb + tt + 1) % 2)
        scan_block(ol_ref, tt, (cb + tt) % 2)
    scan_block(ol_ref, lb - 1, (cb + lb - 1) % 2)


def _gdn(q, k, v, g, beta, lay):
    b, ctx, seq, nc = lay["b"], lay["ctx"], lay["seq"], lay["nc"]
    cs = lambda w: pl.BlockSpec((ctx, w), lambda i: (i, 0))
    ls = lambda w: pl.BlockSpec((seq, w), lambda i: (nc // seq + i, 0))
    widths = (C_W, C_W, C_W, LANE, LANE)
    n_it = PREP_UNROLL
    assert ctx % (C_CHUNK * PREP_UNROLL) == 0 and seq % (C_CHUNK * PREP_UNROLL) == 0
    return pl.pallas_call(
        _gdn_kernel,
        grid=(b,),
        in_specs=[cs(w) for w in widths] + [ls(w) for w in widths],
        out_specs=[pl.BlockSpec((ctx, C_W), lambda i: (i, 0)), pl.BlockSpec((seq, C_W), lambda i: (i, 0))],
        out_shape=[jax.ShapeDtypeStruct((nc, C_W), f32), jax.ShapeDtypeStruct((b * seq, C_W), f32)],
        scratch_shapes=[pltpu.VMEM((N_GROUPS, GW, GW), f32)] + 2 * [
            pltpu.VMEM((n_it * C_CHUNK, GDN_W), f32),
            pltpu.VMEM((n_it * 2 * C_CHUNK, GDN_W), bf16),
            pltpu.VMEM((n_it * C_CHUNK, GDN_W), bf16),
            pltpu.VMEM((n_it * C_CHUNK, GDN_W), bf16),
            pltpu.VMEM((n_it * 8, GDN_W), f32)],
        compiler_params=_cparams(1), name="gated_deltanet",
    )(q, k, v, g, beta, q, k, v, g, beta)


def _outproj_kernel(h_ref, mod_ref, *refs, n_ctx_tiles):
    gate_ref, w_ref, nw_ref, bd64_ref, w1_ref, w2_ref, o_ref = refs[-7:]
    mixed = [r[...] for r in refs[-10:-7]]
    if n_ctx_tiles:
        is_ctx = pl.program_id(0) < n_ctx_tiles
        mixed = [jnp.where(is_ctx, c[...], x) for c, x in zip(refs[:3], mixed)]
    oa, ob, oc = mixed
    bd64 = bd64_ref[...]
    nw = nw_ref[...]

    def normed(x, wrow):
        s = _group_sum(x * x, bd64, terms=1)
        return x * lax.rsqrt(s * (1.0 / C_DK) + EPS) * wrow

    ya = normed(oa, nw[0:1, :A_W]).astype(bf16)
    yc = (normed(oc, nw[1:2, :C_W]) * _silu(gate_ref[...])).astype(bf16)
    y = jnp.dot(ya, w_ref[:A_W, :], preferred_element_type=f32)
    y += jnp.dot(ob, w_ref[A_W:A_W + B_WIDE, :], preferred_element_type=f32)
    y += jnp.dot(yc, w_ref[A_W + B_WIDE:, :], preferred_element_type=f32)
    mod = mod_ref[0]
    o_ref[...] = _ffn_half_step(h_ref[...] + mod[5:6] * y, mod, 6, w1_ref, w2_ref)


def _outproj_ffn(h, mod, ctx_outs, lat_outs, gate, w_out, nw, bd128, w1, w2, lay):
    tm, d = lay["tm"], D_MODEL
    nct = lay["nc"] // tm
    off = 0 if ctx_outs else nct
    n_tiles = h.shape[0] // tm - off
    group = functools.partial(_mod_group, lay=lay, off=off)
    rowo = lambda w: pl.BlockSpec((tm, w), lambda i: (i + off, 0))
    rowc = lambda w: pl.BlockSpec((tm, w), lambda i: (jnp.minimum(i, nct - 1), 0))
    rowl = lambda w: pl.BlockSpec((tm, w), lambda i: (jnp.maximum(i + off - nct, 0), 0))
    const = lambda shape: pl.BlockSpec(shape, lambda i: (0,) * len(shape))
    widths = (A_W, B_WIDE, C_W)
    ctx_specs = [rowc(w) for w in widths] if ctx_outs else []
    return pl.pallas_call(
        functools.partial(_outproj_kernel, n_ctx_tiles=nct if ctx_outs else 0),
        grid=(n_tiles,),
        in_specs=[rowo(d), pl.BlockSpec((1, 9, d), lambda i: (group(i), 0, 0))] + ctx_specs
                 + [rowl(w) for w in widths]
                 + [rowo(C_W), pl.BlockSpec((OUT_WIDE, d), lambda i: (0, 0), pipeline_mode=pl.Buffered(1)),
                    const(nw.shape), const(bd128.shape),
                    pl.BlockSpec((d, 2 * D_FF), lambda i: (0, 0), pipeline_mode=pl.Buffered(1)),
                    pl.BlockSpec((D_FF, d), lambda i: (0, 0), pipeline_mode=pl.Buffered(1))],
        out_specs=pl.BlockSpec((tm, d), lambda i: (i, 0)),
        out_shape=jax.ShapeDtypeStruct((n_tiles * tm, d), f32),
        compiler_params=_cparams(1), name="mixer_out_proj_ffn",
    )(h, mod, *(ctx_outs or ()), *lat_outs, gate, w_out, nw, bd128, w1, w2)


def _block_ones(n, group):
    idx = np.arange(n) // group
    return jnp.asarray(idx[:, None] == idx[None, :], dtype=bf16)


def _rope_table(seq, d, pad_rows):
    half, quarter = d // 2, d // 4
    rows = seq // GRID_W
    row = jnp.repeat(jnp.arange(rows, dtype=f32), GRID_W)
    col = jnp.tile(jnp.arange(GRID_W, dtype=f32), rows)
    inv = ROPE_THETA ** (-jnp.arange(0, half, 2, dtype=f32) / half)
    ld = np.arange(LANE) % d
    pos = jnp.where(jnp.asarray(ld < half)[None, :], row[:, None], col[:, None])
    ang = pos * inv[np.asarray((ld % half) % quarter)][None, :]
    sign = jnp.asarray(np.where((ld % half) < quarter, -1.0, 1.0), dtype=f32)[None, :]
    cos = jnp.concatenate([jnp.ones((pad_rows, LANE), f32), jnp.cos(ang)], axis=0)
    sin = jnp.concatenate([jnp.zeros((pad_rows, LANE), f32), jnp.sin(ang) * sign], axis=0)
    return cos, sin


def _arrange_w_in(w):
    offs = np.concatenate([[0], np.cumsum(IN_SIZES)])
    part = lambda n: w[:, offs[n]:offs[n + 1]]
    zeros = lambda width: jnp.zeros((w.shape[0], width), w.dtype)
    qb = part(3)
    pieces = [part(0), part(1), part(2)]
    for h in range(B_HEADS):
        head = qb[:, h * B_DIM:(h + 1) * B_DIM]
        pieces += [head, zeros(B_DIM)] if h // B_GROUP == 0 else [zeros(B_DIM), head]
    pieces += [part(4), part(5), part(6), part(7), part(8), zeros(LANE - IN_SIZES[8]), part(9), zeros(LANE - IN_SIZES[9])]
    out = jnp.concatenate(pieces, axis=1).astype(bf16)
    assert out.shape[1] == IN_WIDE
    return out


def _arrange_w_out(w):
    zeros = jnp.zeros((B_DIM, w.shape[1]), w.dtype)
    pieces = [w[:A_W]]
    for h in range(B_HEADS):
        head = w[A_W + h * B_DIM:A_W + (h + 1) * B_DIM]
        pieces += [head, zeros] if h // B_GROUP == 0 else [zeros, head]
    pieces.append(w[A_W + B_HEADS * B_DIM:])
    out = jnp.concatenate(pieces, axis=0).astype(bf16)
    assert out.shape[0] == OUT_WIDE
    return out


def _pad_lanes(x, width):
    return jnp.pad(x, ((0, 0), (0, width - x.shape[1])))


def kernel(x, c, ctx, c_ctx, w_mod, b_mod, ffn1_w1, ffn1_w2, ffn2_w1, ffn2_w2, w_in, w_out,
           a_qnorm, a_knorm, a_lambda, a_subln, b_qnorm, b_knorm, b_sink,
           c_conv, c_A_log, c_dt_bias, c_onorm):
    b, seq, d = x.shape
    n_ctx = ctx.shape[1]
    nc = b * n_ctx
    tm = 512 if (nc % 512 == 0 and seq % 512 == 0) else 256
    lay = dict(b=b, ctx=n_ctx, seq=seq, nc=nc, tm=tm)
    assert d == D_MODEL and seq % 256 == 0 and n_ctx % 256 == 0 and nc % seq == 0 and seq % GRID_W == 0

    h = (ctx.reshape(nc, d), x.reshape(b * seq, d))
    cvec = jnp.zeros((16, d), f32).at[:b].set(c).at[b].set(c_ctx)
    mod_all = _modulation(cvec, w_mod, b_mod).reshape(DEPTH, 16, 9, d)

    bd32, bd64, bd64s = _block_ones(256, A_DIM), _block_ones(256, B_DIM), _block_ones(LANE, C_DK)
    tabs = _rope_table(seq, A_DIM, tm) + _rope_table(seq, B_DIM, tm)

    for l in range(DEPTH):
        last = l == DEPTH - 1
        lam_init = 0.8 - 0.6 * float(np.exp(-0.3 * l))
        mod = mod_all[l]
        lf = a_lambda[l].astype(f32)
        lam = (jnp.exp(jnp.sum(lf[0] * lf[1])) - jnp.exp(jnp.sum(lf[2] * lf[3])) + lam_init).reshape(1)
        nw_in = jnp.stack([jnp.tile(a_qnorm[l], 8) * (A_DIM ** -0.5 * LOG2E),
                           jnp.tile(a_knorm[l], 8),
                           jnp.tile(b_qnorm[l], 4) * (B_DIM ** -0.5 * LOG2E),
                           jnp.tile(b_knorm[l], 4)] + [jnp.zeros((256,), f32)] * 4)
        nw_out = jnp.stack([_pad_lanes((jnp.tile(a_subln[l], 4) * (1.0 - lam_init))[None], C_W)[0],
                            jnp.tile(c_onorm[l], C_HEADS)] + [jnp.zeros((C_W,), f32)] * 6)
        gpar = jnp.stack([_pad_lanes(jnp.exp(c_A_log[l].astype(f32)).reshape(1, -1), LANE)[0],
                          _pad_lanes(c_dt_bias[l].astype(f32).reshape(1, -1), LANE)[0]] + [jnp.zeros((LANE,), f32)] * 6)
        cw = jnp.concatenate([c_conv[l], jnp.zeros((8 - C_CONV, 3 * C_W), f32)], axis=0)

        ffn1 = (ffn1_w1[l].astype(bf16), ffn1_w2[l].astype(bf16))
        if isinstance(h, tuple):
            h = _ffn(h, mod, *ffn1, 0, lay)
            qa, ka, va, qb, kb, vb, cq, gate, pa, pb = _inproj(h, mod, _arrange_w_in(w_in[l]), nw_in, tabs, bd32, bd64, lay)
        else:
            qa, ka, va, qb, kb, vb, cq, gate, pa, pb, h = _inproj(h, mod, _arrange_w_in(w_in[l]), nw_in, tabs, bd32,
                                                                  bd64, lay, ffn=ffn1)
        sink = b_sink[l].astype(f32) * LOG2E
        oa = _attn_a(lam, qa, ka, va, lay, True)
        ob = _attn_b(sink, qb, kb, vb, lay, True)
        gq, gk, gv, gg, gbeta = _gdn_prep(cq, pa, pb, cw, gpar, bd64s, lay)
        oc_ctx, oc = _gdn(gq, gk, gv, gg, gbeta, lay)
        ctx_outs = None
        if not last:
            ctx_outs = (_attn_a(lam, qa, ka, va, lay, False), _attn_b(sink, qb, kb, vb, lay, False), oc_ctx)
        h = _outproj_ffn(h, mod, ctx_outs, (oa, ob, oc), gate, _arrange_w_out(w_out[l]), nw_out, bd64s,
                         ffn2_w1[l].astype(bf16), ffn2_w2[l].astype(bf16), lay)
    return h.reshape(b, seq, d)
```
